```python
import jax, jax.numpy as jnp
from jax import lax
import numpy as np

D_MODEL = 1024
BATCH = 8
SEQ = 2048
DEPTH = 1
DEC_BATCH = 128
DEC_SEQ = 8
PAST_LEN = 2048
PAGE_SIZE = 128

HEAD_DIM = 64
N_ATT_HEADS = 8
ATT_WIDTH = N_ATT_HEADS * HEAD_DIM
N_ML_HEADS = 8
ML_WIDTH = N_ML_HEADS * HEAD_DIM
DIL_GROUPS = ((128, 1), (512, 4), (2048, 16))
W_MAX = max(w for w, _ in DIL_GROUPS)
ML_CHUNK = 64
CONV_W = 4
N_MEM = 256
N_X_HEADS = 4
X_HEAD_DIM = D_MODEL // N_X_HEADS
D_FF = -(-8 * D_MODEL // (3 * 256)) * 256
RMS_EPS = 1e-6
NEG_INIT = -1e30
PROJ_SPLITS = (ATT_WIDTH, ATT_WIDTH, ATT_WIDTH, 2 * ML_WIDTH, ML_WIDTH, ML_WIDTH, N_ML_HEADS, N_ML_HEADS)
N_IN = sum(PROJ_SPLITS)

kernel_name = 'hybrid_dilated_attn_mlstm_decoder_step'


def rmsnorm(x, g):
    xf = x.astype(jnp.float32)
    y = xf * lax.rsqrt(jnp.mean(xf * xf, axis=-1, keepdims=True) + RMS_EPS)
    return (y * g.astype(jnp.float32)).astype(x.dtype)


def heads(t, n):
    return t.reshape(t.shape[:-1] + (n, t.shape[-1] // n))


def split_proj(z):
    parts, o = [], 0
    for w in PROJ_SPLITS:
        parts.append(z[..., o:o + w])
        o += w
    return parts


def alibi_slopes(n):
    return jnp.exp2(-8.0 * (jnp.arange(n, dtype=jnp.float32) + 1.0) / n)


def dilated_prompt(q, k, v, window, dil, slopes):
    B, S, H, Dh = q.shape
    band = window // dil
    L = S // dil
    nb = -(-L // band)
    Lp = nb * band

    def to_res(t):
        t = t.reshape(B, L, dil, H, Dh).transpose(0, 2, 1, 3, 4).reshape(B * dil, L, H, Dh)
        t = jnp.pad(t, ((0, 0), (0, Lp - L), (0, 0), (0, 0)))
        return t.reshape(B * dil, nb, band, H, Dh)

    def with_prev(t):
        prev = jnp.pad(t, ((0, 0), (1, 0), (0, 0), (0, 0), (0, 0)))[:, :-1]
        return jnp.concatenate([prev, t], axis=2)

    qb = to_res(q)
    ke, ve = with_prev(to_res(k)), with_prev(to_res(v))
    s = jnp.einsum('bnqhd,bnkhd->bnhqk', qb, ke).astype(jnp.float32) * (Dh ** -0.5)
    qi = jnp.arange(band)[:, None]
    kj = jnp.arange(2 * band)[None, :]
    delta = band + qi - kj
    key_m = (jnp.arange(nb)[:, None, None] - 1) * band + kj[None]
    valid = ((delta >= 0) & (delta <= band))[None] & (key_m >= 0)
    bias = -slopes[:, None, None] * (delta * dil).astype(jnp.float32)[None]
    s = jnp.where(valid[None, :, None], s + bias[None, None], -jnp.inf)
    lse = jax.nn.logsumexp(s, axis=-1)
    p = jnp.exp(s - lse[..., None])
    o = jnp.einsum('bnhqk,bnkhd->bnqhd', p.astype(v.dtype), ve).astype(jnp.float32)

    def from_res(t):
        t = t.reshape((B, dil, Lp) + t.shape[3:])[:, :, :L]
        t = jnp.moveaxis(t, 1, 2)
        return t.reshape((B, S) + t.shape[3:])

    return from_res(o), from_res(lse.transpose(0, 1, 3, 2))


def dilated_sample(q, kk, vv, wb, window, dil, slopes):
    T, Dh = q.shape[1], q.shape[3]
    n_keys = window // dil + 1
    kidx = jnp.arange(n_keys)
    idx = wb + jnp.arange(T)[:, None] - kidx[None, :] * dil
    valid = idx >= 0
    idx_c = jnp.clip(idx, 0)
    kg = jnp.take(kk, idx_c, axis=1)
    vg = jnp.take(vv, idx_c, axis=1)
    s = jnp.einsum('bthd,btkhd->bthk', q, kg).astype(jnp.float32) * (Dh ** -0.5)
    bias = -slopes[:, None] * (kidx * dil).astype(jnp.float32)[None]
    s = jnp.where(valid[None, :, None, :], s + bias[None, None], -jnp.inf)
    lse = jax.nn.logsumexp(s, axis=-1)
    p = jnp.exp(s - lse[..., None])
    o = jnp.einsum('bthk,btkhd->bthd', p.astype(vv.dtype), vg).astype(jnp.float32)
    return o, lse


def combine_groups(outs, lses):
    w = jax.nn.softmax(jnp.stack(lses), axis=0)
    return jnp.einsum('gbth,gbthd->bthd', w, jnp.stack(outs))


def causal_conv(xp, w, b):
    T = xp.shape[1] - CONV_W + 1
    out = b.astype(xp.dtype)
    for j in range(CONV_W):
        out = out + xp[:, j:j + T] * w[j].astype(xp.dtype)
    return out


def mlstm_inputs(qk_conv, mv, mi, mf, b_ig, b_fg):
    qk = jax.nn.silu(qk_conv.astype(jnp.float32))
    q = heads(qk[..., :ML_WIDTH], N_ML_HEADS)
    k = heads(qk[..., ML_WIDTH:], N_ML_HEADS) * (HEAD_DIM ** -0.5)
    v = heads(mv.astype(jnp.float32), N_ML_HEADS)
    ig = mi.astype(jnp.float32) + b_ig.astype(jnp.float32)
    lf = jax.nn.log_sigmoid(mf.astype(jnp.float32) + b_fg.astype(jnp.float32))
    return q, k, v, ig, lf


def mlstm_chunk(carry, xs):
    C, n, m = carry
    q, k, v, ig, lf = xs
    L = q.shape[1]
    b = jnp.cumsum(lf, axis=1)
    causal = jnp.tril(jnp.ones((L, L), dtype=bool))
    log_d = b[:, :, None, :] - b[:, None, :, :] + ig[:, None, :, :]
    log_d = jnp.where(causal[None, :, :, None], log_d, -jnp.inf)
    log_inter = b + m[:, None, :]
    m_t = jnp.maximum(log_inter, jnp.max(log_d, axis=2))
    d_w = jnp.exp(log_d - m_t[:, :, None, :])
    inter_w = jnp.exp(log_inter - m_t)
    a = jnp.einsum('bthd,bshd->btsh', q, k) * d_w
    num = jnp.einsum('btsh,bshd->bthd', a, v) + inter_w[..., None] * jnp.einsum('bthk,bhkv->bthv', q, C)
    den = jnp.sum(a, axis=2) + inter_w * jnp.einsum('bthk,bhk->bth', q, n)
    h = num / jnp.maximum(jnp.abs(den), jnp.exp(-m_t))[..., None]
    m_new = m_t[:, -1]
    w_s = jnp.exp(b[:, -1:, :] - b + ig - m_new[:, None, :])
    decay = jnp.exp(b[:, -1] + m - m_new)
    C_new = decay[..., None, None] * C + jnp.einsum('bsh,bshk,bshv->bhkv', w_s, k, v)
    n_new = decay[..., None] * n + jnp.einsum('bsh,bshk->bhk', w_s, k)
    return (C_new, n_new, m_new), h


def merge_mixers(att, ml_h, mo, p, dtype):
    B, T = att.shape[:2]
    att = att.reshape(B, T, ATT_WIDTH).astype(dtype)
    ml = (jax.nn.sigmoid(mo.astype(jnp.float32) + p['b_og'].astype(jnp.float32)) * ml_h).astype(dtype)
    cat = jnp.concatenate([rmsnorm(att, p['g_attn_out']), rmsnorm(ml, p['g_mlstm_out'])], axis=-1)
    return cat @ p['w_out']


def cross_attn(hn, mem_k, mem_v, w_cq, w_co):
    B, T, _ = hn.shape
    q = heads(hn @ w_cq, N_X_HEADS)
    s = jnp.einsum('bthd,bmhd->bhtm', q, mem_k.astype(q.dtype)).astype(jnp.float32) * (X_HEAD_DIM ** -0.5)
    pr = jax.nn.softmax(s, axis=-1).astype(hn.dtype)
    o = jnp.einsum('bhtm,bmhd->bthd', pr, mem_v.astype(hn.dtype)).reshape(B, T, N_X_HEADS * X_HEAD_DIM)
    return o @ w_co


def tail(h1, mem_k, mem_v, p):
    h2 = h1 + cross_attn(rmsnorm(h1, p['g_cross']), mem_k, mem_v, p['w_cq'], p['w_co'])
    hn = rmsnorm(h2, p['g_ffn'])
    return h2 + (jax.nn.silu(hn @ p['w_gate']) * (hn @ p['w_up'])) @ p['w_down']


def prompt_layer(x, mem, p):
    B, S, _ = x.shape
    z = rmsnorm(x, p['g_mix']) @ p['w_in']
    aq, ak, av, qk_pre, mv, mo, mi, mf = split_proj(z)
    aq, ak, av = heads(aq, N_ATT_HEADS), heads(ak, N_ATT_HEADS), heads(av, N_ATT_HEADS)
    slopes = alibi_slopes(N_ATT_HEADS)
    outs, lses = [], []
    for window, dil in DIL_GROUPS:
        o, l = dilated_prompt(aq, ak, av, window, dil, slopes)
        outs.append(o)
        lses.append(l)
    att = combine_groups(outs, lses)
    xp = jnp.pad(qk_pre, ((0, 0), (CONV_W - 1, 0), (0, 0)))
    q, k, v, ig, lf = mlstm_inputs(causal_conv(xp, p['conv_w'], p['conv_b']), mv, mi, mf, p['b_ig'], p['b_fg'])
    nc = S // ML_CHUNK

    def to_chunks(t):
        return jnp.moveaxis(t.reshape((B, nc, ML_CHUNK) + t.shape[2:]), 1, 0)

    init = (jnp.zeros((B, N_ML_HEADS, HEAD_DIM, HEAD_DIM), jnp.float32),
            jnp.zeros((B, N_ML_HEADS, HEAD_DIM), jnp.float32),
            jnp.full((B, N_ML_HEADS), NEG_INIT, jnp.float32))
    (C, n, m), hs = lax.scan(mlstm_chunk, init, (to_chunks(q), to_chunks(k), to_chunks(v), to_chunks(ig), to_chunks(lf)))
    ml_h = jnp.moveaxis(hs, 0, 1).reshape(B, S, ML_WIDTH)
    h1 = x + merge_mixers(att, ml_h, mo, p, x.dtype)
    mem_n = rmsnorm(mem, p['g_mem'])
    mem_k = heads(mem_n @ p['w_ck'], N_X_HEADS)
    mem_v = heads(mem_n @ p['w_cv'], N_X_HEADS)
    h = tail(h1, mem_k, mem_v, p)
    keep = min(W_MAX, S)
    return h, (ak[:, S - keep:], av[:, S - keep:], xp[:, -(CONV_W - 1):], C, n, m, mem_k, mem_v)


def sample_layer(x, win_k, win_v, mem_k, mem_v, conv_buf, C0, n0, m0, p):
    B, T, _ = x.shape
    z = rmsnorm(x, p['g_mix']) @ p['w_in']
    aq, ak, av, qk_pre, mv, mo, mi, mf = split_proj(z)
    aq, ak, av = heads(aq, N_ATT_HEADS), heads(ak, N_ATT_HEADS), heads(av, N_ATT_HEADS)
    wb = win_k.shape[1]
    kk = jnp.concatenate([win_k.astype(ak.dtype), ak], axis=1)
    vv = jnp.concatenate([win_v.astype(av.dtype), av], axis=1)
    slopes = alibi_slopes(N_ATT_HEADS)
    outs, lses = [], []
    for window, dil in DIL_GROUPS:
        o, l = dilated_sample(aq, kk, vv, wb, window, dil, slopes)
        outs.append(o)
        lses.append(l)
    att = combine_groups(outs, lses)
    xp = jnp.concatenate([conv_buf.astype(qk_pre.dtype), qk_pre], axis=1)
    q, k, v, ig, lf = mlstm_inputs(causal_conv(xp, p['conv_w'], p['conv_b']), mv, mi, mf, p['b_ig'], p['b_fg'])
    carry = (C0.astype(jnp.float32), n0.astype(jnp.float32), m0.astype(jnp.float32))
    (C, n, m), hch = mlstm_chunk(carry, (q, k, v, ig, lf))
    ml_h = hch.reshape(B, T, ML_WIDTH)
    h1 = x + merge_mixers(att, ml_h, mo, p, x.dtype)
    h = tail(h1, mem_k, mem_v, p)
    keep = min(W_MAX, kk.shape[1])
    return h, (kk[:, kk.shape[1] - keep:], vv[:, vv.shape[1] - keep:], xp[:, -(CONV_W - 1):], C, n, m)


def setup_inputs(seed: int = 0) -> dict:
    key = jax.random.key(seed)
    ks = jax.random.split(key, 40)
    f = jnp.float32

    def nrm(k, shape, scale):
        return jax.random.normal(k, shape, f) * scale

    def gain(k, n):
        return 1.0 + 0.02 * jax.random.normal(k, (n,), f)

    win_len = min(W_MAX, PAST_LEN)
    return {
        'x_prompt': nrm(ks[0], (BATCH, SEQ, D_MODEL), 1.0),
        'x_sample': nrm(ks[1], (DEC_BATCH, DEC_SEQ, D_MODEL), 1.0),
        'mem_prompt': nrm(ks[2], (BATCH, N_MEM, D_MODEL), 1.0),
        'cache_win_k': nrm(ks[3], (DEC_BATCH, win_len, N_ATT_HEADS, HEAD_DIM), 1.0),
        'cache_win_v': nrm(ks[4], (DEC_BATCH, win_len, N_ATT_HEADS, HEAD_DIM), 1.0),
        'cache_mem_k': nrm(ks[5], (DEC_BATCH, N_MEM, N_X_HEADS, X_HEAD_DIM), 1.0),
        'cache_mem_v': nrm(ks[6], (DEC_BATCH, N_MEM, N_X_HEADS, X_HEAD_DIM), 1.0),
        'state_conv': nrm(ks[7], (DEC_BATCH, CONV_W - 1, 2 * ML_WIDTH), 1.0),
        'state_mlstm_C': nrm(ks[8], (DEC_BATCH, N_ML_HEADS, HEAD_DIM, HEAD_DIM), 0.1),
        'state_mlstm_n': nrm(ks[9], (DEC_BATCH, N_ML_HEADS, HEAD_DIM), 0.1),
        'state_mlstm_m': nrm(ks[10], (DEC_BATCH, N_ML_HEADS), 1.0),
        'g_mix': gain(ks[11], D_MODEL),
        'w_in': nrm(ks[12], (D_MODEL, N_IN), D_MODEL ** -0.5),
        'conv_w': nrm(ks[13], (CONV_W, 2 * ML_WIDTH), CONV_W ** -0.5),
        'conv_b': nrm(ks[14], (2 * ML_WIDTH,), 0.02),
        'b_ig': nrm(ks[15], (N_ML_HEADS,), 0.1),
        'b_fg': jnp.linspace(3.0, 6.0, N_ML_HEADS, dtype=f) + nrm(ks[16], (N_ML_HEADS,), 0.01),
        'b_og': nrm(ks[17], (ML_WIDTH,), 0.02),
        'g_attn_out': gain(ks[18], ATT_WIDTH),
        'g_mlstm_out': gain(ks[19], ML_WIDTH),
        'w_out': nrm(ks[20], (ATT_WIDTH + ML_WIDTH, D_MODEL), (ATT_WIDTH + ML_WIDTH) ** -0.5),
        'g_cross': gain(ks[21], D_MODEL),
        'g_mem': gain(ks[22], D_MODEL),
        'w_cq': nrm(ks[23], (D_MODEL, N_X_HEADS * X_HEAD_DIM), D_MODEL ** -0.5),
        'w_ck': nrm(ks[24], (D_MODEL, N_X_HEADS * X_HEAD_DIM), D_MODEL ** -0.5),
        'w_cv': nrm(ks[25], (D_MODEL, N_X_HEADS * X_HEAD_DIM), D_MODEL ** -0.5),
        'w_co': nrm(ks[26], (N_X_HEADS * X_HEAD_DIM, D_MODEL), (N_X_HEADS * X_HEAD_DIM) ** -0.5),
        'g_ffn': gain(ks[27], D_MODEL),
        'w_gate': nrm(ks[28], (D_MODEL, D_FF), D_MODEL ** -0.5),
        'w_up': nrm(ks[29], (D_MODEL, D_FF), D_MODEL ** -0.5),
        'w_down': nrm(ks[30], (D_FF, D_MODEL), D_FF ** -0.5),
        'g_final': gain(ks[31], D_MODEL),
    }


def reference(x_prompt, x_sample, mem_prompt, cache_win_k, cache_win_v, cache_mem_k, cache_mem_v,
              state_conv, state_mlstm_C, state_mlstm_n, state_mlstm_m,
              g_mix, w_in, conv_w, conv_b, b_ig, b_fg, b_og, g_attn_out, g_mlstm_out, w_out,
              g_cross, g_mem, w_cq, w_ck, w_cv, w_co, g_ffn, w_gate, w_up, w_down, g_final):
    p = dict(g_mix=g_mix, w_in=w_in, conv_w=conv_w, conv_b=conv_b, b_ig=b_ig, b_fg=b_fg, b_og=b_og,
             g_attn_out=g_attn_out, g_mlstm_out=g_mlstm_out, w_out=w_out, g_cross=g_cross, g_mem=g_mem,
             w_cq=w_cq, w_ck=w_ck, w_cv=w_cv, w_co=w_co, g_ffn=g_ffn, w_gate=w_gate, w_up=w_up, w_down=w_down)
    h_p, h_s = x_prompt, x_sample
    for _ in range(DEPTH):
        h_p, st_p = prompt_layer(h_p, mem_prompt, p)
        h_s, st_s = sample_layer(h_s, cache_win_k, cache_win_v, cache_mem_k, cache_mem_v,
                                 state_conv, state_mlstm_C, state_mlstm_n, state_mlstm_m, p)
    y_prompt = rmsnorm(h_p, g_final)
    y_sample = rmsnorm(h_s, g_final)
    win_k_p, win_v_p, conv_p, C_p, n_p, m_p, mem_k_p, mem_v_p = st_p
    win_k_s, win_v_s, conv_s, C_s, n_s, m_s = st_s
    return (y_prompt, y_sample, win_k_p, win_v_p, conv_p, C_p, n_p, m_p, mem_k_p, mem_v_p,
            win_k_s, win_v_s, conv_s, C_s, n_s, m_s)
```

```python
import functools

import numpy as np
import jax
import jax.numpy as jnp
from jax import lax
from jax.experimental import pallas as pl
from jax.experimental.pallas import tpu as pltpu

F32 = jnp.float32
BF16 = jnp.bfloat16

D_MODEL = 1024
HEAD_DIM = 64
N_HEADS = 8
MIX_W = N_HEADS * HEAD_DIM
DIL_GROUPS = ((128, 1), (512, 4), (2048, 16))
BAND = 128
CONV_W = 4
N_MEM = 256
N_X_HEADS = 4
X_HEAD_DIM = D_MODEL // N_X_HEADS
D_FF = 2816
RMS_EPS = 1e-6
NEG = -1e30
N_IN = 3 * MIX_W + 2 * MIX_W + MIX_W + MIX_W + 2 * N_HEADS
GATE_PAD = 128

LANES = 128
SUBLANES = 8
VMEM_BYTES_V7X = 64 * 1024 * 1024


def _vmem_limit(nbytes):
    return int(min(max(2 * nbytes, 16 * 1024 * 1024), VMEM_BYTES_V7X - 8 * 1024 * 1024))


def _params(semantics, vmem_estimate):
    return pltpu.CompilerParams(dimension_semantics=semantics,
                                vmem_limit_bytes=_vmem_limit(vmem_estimate))


def _rms(x, g):
    return x * lax.rsqrt(jnp.mean(x * x, axis=-1, keepdims=True) + RMS_EPS) * g


def _sigmoid(x):
    return 1.0 / (1.0 + jnp.exp(-x))


def _log_sigmoid(x):
    return jnp.minimum(x, 0.0) - jnp.log(1.0 + jnp.exp(-jnp.abs(x)))


def _dot(a, b):
    return jnp.dot(a, b, preferred_element_type=F32)


def _dot_nt(a, b):
    return lax.dot_general(a, b, (((1,), (1,)), ((), ())), preferred_element_type=F32)


def _dot_tn(a, b):
    return lax.dot_general(a, b, (((0,), (0,)), ((), ())), preferred_element_type=F32)


def _norm_matmul_kernel(x_ref, g_ref, w_ref, *o_refs, splits):
    xn = _rms(x_ref[...], g_ref[...]).astype(BF16)
    for o_ref, (a, b) in zip(o_refs, splits):
        o_ref[...] = _dot(xn, w_ref[:, a:b])


def _norm_matmul(x, g, w, widths, tm):
    m, k = x.shape
    n = w.shape[1]
    assert sum(widths) == n and m % tm == 0
    offs = np.cumsum([0] + list(widths))
    splits = tuple((int(offs[i]), int(offs[i + 1])) for i in range(len(widths)))
    est = 2 * tm * k * 4 + 2 * k * n * 2 + 2 * tm * n * 4
    return pl.pallas_call(
        functools.partial(_norm_matmul_kernel, splits=splits),
        grid=(m // tm,),
        in_specs=[pl.BlockSpec((tm, k), lambda i: (i, 0)),
                  pl.BlockSpec((1, k), lambda i: (0, 0)),
                  pl.BlockSpec((k, n), lambda i: (0, 0))],
        out_specs=[pl.BlockSpec((tm, wd), lambda i: (i, 0)) for wd in widths],
        out_shape=[jax.ShapeDtypeStruct((m, wd), F32) for wd in widths],
        compiler_params=_params(("parallel",), est),
        name="norm_matmul",
    )(x, g.reshape(1, k), w)


def _matmul_res_kernel(x_ref, w_ref, r_ref, o_ref):
    o_ref[...] = r_ref[...] + _dot(x_ref[...].astype(BF16), w_ref[...])


def _matmul_res(x, w, res, tm):
    m, k = x.shape
    n = w.shape[1]
    est = 2 * tm * k * 4 + 2 * k * n * 2 + 4 * tm * n * 4
    return pl.pallas_call(
        _matmul_res_kernel,
        grid=(m // tm,),
        in_specs=[pl.BlockSpec((tm, k), lambda i: (i, 0)),
                  pl.BlockSpec((k, n), lambda i: (0, 0)),
                  pl.BlockSpec((tm, n), lambda i: (i, 0))],
        out_specs=pl.BlockSpec((tm, n), lambda i: (i, 0)),
        out_shape=jax.ShapeDtypeStruct((m, n), F32),
        compiler_params=_params(("parallel",), est),
        name="matmul_res",
    )(x, w, res)


def _merge_kernel(att_ref, mlh_ref, mo_ref, bog_ref, ga_ref, gm_ref, w_ref, x_ref, o_ref):
    a_n = _rms(att_ref[...], ga_ref[...]).astype(BF16)
    ml = _sigmoid(mo_ref[...] + bog_ref[...]) * mlh_ref[...]
    m_n = _rms(ml, gm_ref[...]).astype(BF16)
    o_ref[...] = x_ref[...] + _dot(a_n, w_ref[0:MIX_W, :]) + _dot(m_n, w_ref[MIX_W:2 * MIX_W, :])


def _merge(att, mlh, mo, b_og, g_a, g_m, w_out, x, tm):
    m = x.shape[0]
    half = lambda: pl.BlockSpec((tm, MIX_W), lambda i: (i, 0))
    vec = lambda: pl.BlockSpec((1, MIX_W), lambda i: (0, 0))
    est = 6 * tm * MIX_W * 4 + 2 * 2 * MIX_W * D_MODEL * 2 + 4 * tm * D_MODEL * 4
    return pl.pallas_call(
        _merge_kernel,
        grid=(m // tm,),
        in_specs=[half(), half(), half(), vec(), vec(), vec(),
                  pl.BlockSpec((2 * MIX_W, D_MODEL), lambda i: (0, 0)),
                  pl.BlockSpec((tm, D_MODEL), lambda i: (i, 0))],
        out_specs=pl.BlockSpec((tm, D_MODEL), lambda i: (i, 0)),
        out_shape=jax.ShapeDtypeStruct((m, D_MODEL), F32),
        compiler_params=_params(("parallel",), est),
        name="merge",
    )(att, mlh, mo, b_og.reshape(1, -1), g_a.reshape(1, -1), g_m.reshape(1, -1), w_out, x)


FF_CHUNK = D_FF // 2


def _ffn_kernel(h_ref, gf_ref, wg_ref, wu_ref, wd_ref, gl_ref, o_ref):
    h = h_ref[...]
    hn = _rms(h, gf_ref[...]).astype(BF16)
    acc = h
    for c in range(D_FF // FF_CHUNK):
        sl = slice(c * FF_CHUNK, (c + 1) * FF_CHUNK)
        gate = _dot(hn, wg_ref[:, sl])
        up = _dot(hn, wu_ref[:, sl])
        act = (gate * _sigmoid(gate) * up).astype(BF16)
        acc = acc + _dot(act, wd_ref[sl, :])
    o_ref[...] = _rms(acc, gl_ref[...])


def _ffn(h, g_ffn, w_gate, w_up, w_down, g_final, tm):
    m = h.shape[0]
    est = 4 * tm * D_MODEL * 4 + 2 * 3 * D_MODEL * D_FF * 2 + 3 * tm * FF_CHUNK * 4
    return pl.pallas_call(
        _ffn_kernel,
        grid=(m // tm,),
        in_specs=[pl.BlockSpec((tm, D_MODEL), lambda i: (i, 0)),
                  pl.BlockSpec((1, D_MODEL), lambda i: (0, 0)),
                  pl.BlockSpec((D_MODEL, D_FF), lambda i: (0, 0)),
                  pl.BlockSpec((D_MODEL, D_FF), lambda i: (0, 0)),
                  pl.BlockSpec((D_FF, D_MODEL), lambda i: (0, 0)),
                  pl.BlockSpec((1, D_MODEL), lambda i: (0, 0))],
        out_specs=pl.BlockSpec((tm, D_MODEL), lambda i: (i, 0)),
        out_shape=jax.ShapeDtypeStruct((m, D_MODEL), F32),
        compiler_params=_params(("parallel",), est),
        name="ffn",
    )(h, g_ffn.reshape(1, -1), w_gate, w_up, w_down, g_final.reshape(1, -1))


def _xattn_kernel(q_ref, k_ref, v_ref, o_ref, *, bt):
    def one(bi, carry):
        for h in range(N_X_HEADS):
            sl = slice(h * X_HEAD_DIM, (h + 1) * X_HEAD_DIM)
            q = (q_ref[bi, :, sl] * (X_HEAD_DIM ** -0.5)).astype(BF16)
            s = _dot_nt(q, k_ref[bi, :, sl].astype(BF16))
            p = jnp.exp(s - jnp.max(s, axis=-1, keepdims=True))
            l = jnp.sum(p, axis=-1, keepdims=True)
            o_ref[bi, :, sl] = _dot(p.astype(BF16), v_ref[bi, :, sl].astype(BF16)) / l
        return carry
    if bt == 1:
        one(0, 0)
    else:
        lax.fori_loop(0, bt, one, 0)


def _xattn(q, mem_k, mem_v, bt, tq):
    b, t, _ = q.shape
    est = 4 * bt * tq * D_MODEL * 4 + 4 * bt * N_MEM * D_MODEL * 4
    return pl.pallas_call(
        functools.partial(_xattn_kernel, bt=bt),
        grid=(b // bt, t // tq),
        in_specs=[pl.BlockSpec((bt, tq, D_MODEL), lambda i, j: (i, j, 0)),
                  pl.BlockSpec((bt, N_MEM, D_MODEL), lambda i, j: (i, 0, 0)),
                  pl.BlockSpec((bt, N_MEM, D_MODEL), lambda i, j: (i, 0, 0))],
        out_specs=pl.BlockSpec((bt, tq, D_MODEL), lambda i, j: (i, j, 0)),
        out_shape=jax.ShapeDtypeStruct((b, t, D_MODEL), F32),
        compiler_params=_params(("parallel", "parallel"), est),
        name="xattn",
    )(q, mem_k, mem_v)


def _alibi_slopes():
    return np.exp2(-8.0 * (np.arange(N_HEADS, dtype=np.float64) + 1.0) / N_HEADS)


def _prompt_bias_table():
    qi = np.arange(BAND)[:, None]
    kj = np.arange(2 * BAND)[None, :]
    delta = BAND + qi - kj
    valid = (delta >= 0) & (delta <= BAND)
    tabs = []
    for _, dil in DIL_GROUPS:
        bias = -_alibi_slopes()[:, None, None] * (delta * dil)[None].astype(np.float64)
        tabs.append(np.where(valid[None], bias, NEG))
    return np.stack(tabs).astype(np.float32)


def _attn_prompt_kernel(q_ref, k_ref, v_ref, bias_ref, o_ref, og0, og1, og2, lg0, lg1, lg2, *, seq):
    og = (og0, og1, og2)
    lg = (lg0, lg1, lg2)
    lane = lax.broadcasted_iota(jnp.int32, (1, LANES), 1)
    first = lane < HEAD_DIM

    def rows(start, n, dil):
        return pl.ds(start, n) if dil == 1 else pl.ds(start, n, stride=dil)

    def unit(g, dil, q_start, k_start, nk):
        q = q_ref[0, rows(q_start, BAND, dil), :] * (HEAD_DIM ** -0.5)
        k = k_ref[0, rows(k_start, nk, dil), :].astype(BF16)
        v = v_ref[0, rows(k_start, nk, dil), :].astype(BF16)
        outs, lses = [], []
        for h in range(2):
            qh = jnp.where(first if h == 0 else jnp.logical_not(first), q, 0.0).astype(BF16)
            if nk == 2 * BAND:
                bias = bias_ref[g, h]
            else:
                bias = bias_ref[g, h, :, BAND:2 * BAND]
            s = _dot_nt(qh, k) + bias
            m = jnp.max(s, axis=-1, keepdims=True)
            p = jnp.exp(s - m)
            l = jnp.sum(p, axis=-1, keepdims=True)
            outs.append(_dot(p.astype(BF16), v) / l)
            lses.append(m + jnp.log(l))
        og[g][rows(q_start, BAND, dil), :] = jnp.where(first, outs[0], outs[1])
        lg[g][rows(q_start, BAND, dil), :] = jnp.where(first, lses[0], lses[1])

    for g, (window, dil) in enumerate(DIL_GROUPS):
        assert window // dil == BAND
        span = BAND * dil
        nblk = seq // span
        for r in range(dil) if nblk > 1 else ():
            unit(g, dil, r, r, BAND)

            def body(i, carry, g=g, dil=dil, r=r, span=span):
                unit(g, dil, r + i * span, r + (i - 1) * span, 2 * BAND)
                return carry
            lax.fori_loop(1, nblk, body, 0)
        if nblk == 1:
            def body1(r, carry, g=g, dil=dil):
                unit(g, dil, r, r, BAND)
                return carry
            lax.fori_loop(0, dil, body1, 0)

    chunk = 256

    def merge(i, carry):
        sl = pl.ds(pl.multiple_of(i * chunk, chunk), chunk)
        l0, l1, l2 = lg0[sl, :], lg1[sl, :], lg2[sl, :]
        mx = jnp.maximum(jnp.maximum(l0, l1), l2)
        e0, e1, e2 = jnp.exp(l0 - mx), jnp.exp(l1 - mx), jnp.exp(l2 - mx)
        o_ref[0, sl, :] = (e0 * og0[sl, :] + e1 * og1[sl, :] + e2 * og2[sl, :]) / (e0 + e1 + e2)
        return carry
    lax.fori_loop(0, seq // chunk, merge, 0)


def _attn_prompt(q, k, v):
    b, s, _ = q.shape
    bias = jnp.asarray(_prompt_bias_table())
    blk = lambda: pl.BlockSpec((1, s, LANES), lambda i, p: (i, 0, p))
    est = 8 * s * LANES * 4 + 6 * s * LANES * 4 + 2 * 3 * 2 * BAND * 2 * BAND * 4
    return pl.pallas_call(
        functools.partial(_attn_prompt_kernel, seq=s),
        grid=(b, MIX_W // LANES),
        in_specs=[blk(), blk(), blk(),
                  pl.BlockSpec((len(DIL_GROUPS), 2, BAND, 2 * BAND), lambda i, p: (0, p, 0, 0))],
        out_specs=blk(),
        out_shape=jax.ShapeDtypeStruct((b, s, MIX_W), F32),
        scratch_shapes=[pltpu.VMEM((s, LANES), F32) for _ in range(6)],
        compiler_params=_params(("parallel", "parallel"), est),
        name="attn_prompt",
    )(q, k, v, bias)


def _sample_bias_tables(wb, t_new):
    slopes = _alibi_slopes()

    def table(dist):
        mult = np.zeros(dist.shape)
        for window, dil in DIL_GROUPS:
            mult += (dist >= 0) & (dist <= window) & (dist % dil == 0)
        logm = np.where(mult > 0, np.log(np.maximum(mult, 1.0)), NEG)
        tab = -slopes[:, None, None] * dist[None].astype(np.float64) + logm[None]
        tab = np.where((mult > 0)[None], tab, NEG)
        return tab.reshape(N_HEADS * t_new, dist.shape[1]).astype(np.float32)

    t = np.arange(t_new)[:, None]
    return table(wb + t - np.arange(wb)[None, :]), table(t - np.arange(t_new)[None, :])


def _attn_sample_kernel(q_ref, kn_ref, vn_ref, wk_ref, wv_ref, bo_ref, bn_ref,
                        att_ref, ok_ref, ov_ref, *, wb, t_new):
    rows_q = N_HEADS * t_new
    q = q_ref[0] * (HEAD_DIM ** -0.5)
    qm = jnp.concatenate([q] * N_HEADS, axis=0)
    row_head = lax.broadcasted_iota(jnp.int32, (rows_q, MIX_W), 0) // t_new
    lane_head = lax.broadcasted_iota(jnp.int32, (rows_q, MIX_W), 1) // HEAD_DIM
    own = row_head == lane_head
    qm = jnp.where(own, qm, 0.0).astype(BF16)
    k_new, v_new = kn_ref[0], vn_ref[0]
    s_old = _dot_nt(qm, wk_ref[0].astype(BF16)) + bo_ref[...]
    s_new = _dot_nt(qm, k_new.astype(BF16)) + bn_ref[...]
    m = jnp.maximum(jnp.max(s_old, axis=-1, keepdims=True), jnp.max(s_new, axis=-1, keepdims=True))
    p_old = jnp.exp(s_old - m)
    p_new = jnp.exp(s_new - m)
    l = jnp.sum(p_old, axis=-1, keepdims=True) + jnp.sum(p_new, axis=-1, keepdims=True)
    o = _dot(p_old.astype(BF16), wv_ref[0].astype(BF16)) + _dot(p_new.astype(BF16), v_new.astype(BF16))
    o = jnp.where(own, o / l, 0.0)
    att = o[0:t_new]
    for h in range(1, N_HEADS):
        att = att + o[h * t_new:(h + 1) * t_new]
    att_ref[0] = att
    ok_ref[0, 0:wb - t_new, :] = wk_ref[0, t_new:wb, :]
    ok_ref[0, wb - t_new:wb, :] = k_new
    ov_ref[0, 0:wb - t_new, :] = wv_ref[0, t_new:wb, :]
    ov_ref[0, wb - t_new:wb, :] = v_new


def _attn_sample(q, k_new, v_new, win_k, win_v):
    b, t_new, _ = q.shape
    wb = win_k.shape[1]
    assert wb >= DIL_GROUPS[-1][0] and t_new % SUBLANES == 0
    b_old, b_new = _sample_bias_tables(wb, t_new)
    small = lambda: pl.BlockSpec((1, t_new, MIX_W), lambda i: (i, 0, 0))
    big = lambda: pl.BlockSpec((1, wb, MIX_W), lambda i: (i, 0, 0))
    est = 8 * wb * MIX_W * 4 + 2 * wb * MIX_W * 2 + 4 * N_HEADS * t_new * wb * 4
    return pl.pallas_call(
        functools.partial(_attn_sample_kernel, wb=wb, t_new=t_new),
        grid=(b,),
        in_specs=[small(), small(), small(), big(), big(),
                  pl.BlockSpec((N_HEADS * t_new, wb), lambda i: (0, 0)),
                  pl.BlockSpec((N_HEADS * t_new, t_new), lambda i: (0, 0))],
        out_specs=[small(), big(), big()],
        out_shape=[jax.ShapeDtypeStruct((b, t_new, MIX_W), F32),
                   jax.ShapeDtypeStruct((b, wb, MIX_W), F32),
                   jax.ShapeDtypeStruct((b, wb, MIX_W), F32)],
        compiler_params=_params(("parallel",), est),
        name="attn_sample",
    )(q, k_new, v_new, win_k, win_v, jnp.asarray(b_old), jnp.asarray(b_new))


def _mlstm_kernel(qk_ref, cinit_ref, cw_ref, cb_ref, mv_ref, gcol_ref, mirow_ref, mfrow_ref,
                  bcol_ref, bigrow_ref, bfgrow_ref, c0_ref, n0_ref, m0_ref,
                  h_ref, cst_ref, c_ref, n_ref, m_ref, xp_scr, *, chunk, bt):
    L = chunk

    @pl.when(pl.program_id(1) == 0)
    def _():
        c_ref[...] = c0_ref[...]
        n_ref[...] = n0_ref[...]
        m_ref[...] = m0_ref[...]
        xp_scr[:, 0:SUBLANES, :] = cinit_ref[...]

    rr = lax.broadcasted_iota(jnp.int32, (L, L), 0)
    cc = lax.broadcasted_iota(jnp.int32, (L, L), 1)
    causal = rr >= cc
    tril = causal.astype(F32)
    triu = (rr <= cc).astype(F32)
    hp = lax.Precision.HIGHEST

    def one(bi, carry):
        xp_scr[bi, SUBLANES:SUBLANES + L, :] = qk_ref[bi]
        conv = cb_ref[...]
        for j in range(CONV_W):
            off = SUBLANES - (CONV_W - 1) + j
            conv = conv + xp_scr[bi, off:off + L, :] * cw_ref[j:j + 1, :]
        tail = xp_scr[bi, L:L + SUBLANES, :]
        xp_scr[bi, 0:SUBLANES, :] = tail
        cst_ref[bi] = tail
        qk = conv * _sigmoid(conv)
        v_all = mv_ref[bi]

        pre_c = gcol_ref[bi] + bcol_ref[...]
        bcum_c = jnp.dot(tril, _log_sigmoid(pre_c), precision=hp, preferred_element_type=F32)
        ig_r_all = mirow_ref[bi] + bigrow_ref[...]
        bcum_r = jnp.dot(_log_sigmoid(mfrow_ref[bi] + bfgrow_ref[...]), triu, precision=hp,
                         preferred_element_type=F32)

        for h in range(N_HEADS):
            sl = slice(h * HEAD_DIM, (h + 1) * HEAD_DIM)
            q = qk[:, sl]
            k = qk[:, MIX_W + h * HEAD_DIM:MIX_W + (h + 1) * HEAD_DIM] * (HEAD_DIM ** -0.5)
            v = v_all[:, sl]
            ig_c = pre_c[:, h:h + 1]
            b_c = bcum_c[:, N_HEADS + h:N_HEADS + h + 1]
            b_r = bcum_r[h:h + 1, :]
            ig_r = ig_r_all[h:h + 1, :]
            c_prev = c_ref[bi, h]
            n_prev = n_ref[bi, h:h + 1, :]
            m_prev = m_ref[bi, :, h:h + 1]

            log_d = jnp.where(causal, b_c - b_r + ig_r, NEG)
            log_inter = b_c + m_prev
            m_t = jnp.maximum(log_inter, jnp.max(log_d, axis=-1, keepdims=True))
            d_w = jnp.exp(log_d - m_t)
            inter_w = jnp.exp(log_inter - m_t)
            qb, kb, vb = q.astype(BF16), k.astype(BF16), v.astype(BF16)
            a = _dot_nt(qb, kb) * d_w
            num = _dot(a.astype(BF16), vb) + inter_w * _dot(qb, c_prev.astype(BF16))
            den = jnp.sum(a, axis=-1, keepdims=True) + inter_w * jnp.sum(q * n_prev, axis=-1, keepdims=True)
            h_ref[bi, :, sl] = num / jnp.maximum(jnp.abs(den), jnp.exp(-m_t))

            m_new = m_t[L - 1:L, :]
            b_last = b_c[L - 1:L, :]
            w_s = jnp.exp(b_last - b_c + ig_c - m_new)
            decay = jnp.exp(b_last + m_prev - m_new)
            kw = k * w_s
            c_ref[bi, h] = decay * c_prev + _dot_tn(kw.astype(BF16), vb)
            n_ref[bi, h:h + 1, :] = decay * n_prev + jnp.sum(kw, axis=0, keepdims=True)
            m_ref[bi, :, h:h + 1] = m_new
        return carry

    if bt == 1:
        one(0, 0)
    else:
        lax.fori_loop(0, bt, one, 0)


def _mlstm(qk_pre, conv_init, conv_w, conv_b, mv, gates, b_ig, b_fg, c0, n0, m0, chunk, bt):
    b, s, _ = qk_pre.shape
    assert s % chunk == 0 and b % bt == 0 and chunk % SUBLANES == 0
    cinit = jnp.pad(conv_init, ((0, 0), (SUBLANES - (CONV_W - 1), 0), (0, 0)))
    cw = jnp.pad(conv_w, ((0, SUBLANES - CONV_W), (0, 0)))
    g_rows = jnp.swapaxes(gates[:, :, :2 * N_HEADS], 1, 2)
    mi_row, mf_row = g_rows[:, :N_HEADS], g_rows[:, N_HEADS:]
    bcol = jnp.concatenate([b_ig, b_fg, jnp.zeros((GATE_PAD - 2 * N_HEADS,), F32)]).reshape(1, GATE_PAD)
    nc = s // chunk
    seq3 = lambda w: pl.BlockSpec((bt, chunk, w), lambda i, c: (i, c, 0))
    fix3 = lambda r, w: pl.BlockSpec((bt, r, w), lambda i, c: (i, 0, 0))
    row3 = lambda: pl.BlockSpec((bt, N_HEADS, chunk), lambda i, c: (i, 0, c))
    const = lambda r, w: pl.BlockSpec((r, w), lambda i, c: (0, 0))
    cspec = lambda: pl.BlockSpec((bt, N_HEADS, HEAD_DIM, HEAD_DIM), lambda i, c: (i, 0, 0, 0))
    est = (2 * bt * chunk * (2 * MIX_W + MIX_W + GATE_PAD + MIX_W) * 4 + bt * (chunk + SUBLANES) * 2 * MIX_W * 4
           + 6 * bt * N_HEADS * HEAD_DIM * HEAD_DIM * 4 + 8 * chunk * chunk * 4 + 6 * chunk * 2 * MIX_W * 4)
    outs = pl.pallas_call(
        functools.partial(_mlstm_kernel, chunk=chunk, bt=bt),
        grid=(b // bt, nc),
        in_specs=[seq3(2 * MIX_W), fix3(SUBLANES, 2 * MIX_W), const(SUBLANES, 2 * MIX_W), const(1, 2 * MIX_W),
                  seq3(MIX_W), seq3(GATE_PAD), row3(), row3(),
                  const(1, GATE_PAD), const(N_HEADS, 1), const(N_HEADS, 1),
                  cspec(), fix3(N_HEADS, HEAD_DIM), fix3(1, N_HEADS)],
        out_specs=[seq3(MIX_W), fix3(SUBLANES, 2 * MIX_W), cspec(), fix3(N_HEADS, HEAD_DIM), fix3(1, N_HEADS)],
        out_shape=[jax.ShapeDtypeStruct((b, s, MIX_W), F32),
                   jax.ShapeDtypeStruct((b, SUBLANES, 2 * MIX_W), F32),
                   jax.ShapeDtypeStruct((b, N_HEADS, HEAD_DIM, HEAD_DIM), F32),
                   jax.ShapeDtypeStruct((b, N_HEADS, HEAD_DIM), F32),
                   jax.ShapeDtypeStruct((b, 1, N_HEADS), F32)],
        scratch_shapes=[pltpu.VMEM((bt, chunk + SUBLANES, 2 * MIX_W), F32)],
        compiler_params=_params(("arbitrary", "arbitrary"), est),
        name="mlstm",
    )(qk_pre, cinit, cw, conv_b.reshape(1, -1), mv, gates, mi_row, mf_row,
      bcol, b_ig.reshape(-1, 1), b_fg.reshape(-1, 1), c0, n0, m0.reshape(b, 1, N_HEADS))
    h, cst, c, n, m = outs
    return h, cst[:, SUBLANES - (CONV_W - 1):], c, n, m.reshape(b, N_HEADS)


IN_WIDTHS = (MIX_W, MIX_W, MIX_W, 2 * MIX_W, MIX_W, MIX_W, GATE_PAD)


def _mixers_to_output(x2, att, mlh, mo, mem_k, mem_v, wts, bt_x, tq_x, tm):
    b = mem_k.shape[0]
    h1 = _merge(att, mlh, mo, wts["b_og"], wts["g_attn_out"], wts["g_mlstm_out"], wts["w_out"], x2, tm)
    (xq,) = _norm_matmul(h1, wts["g_cross"], wts["w_cq"], (D_MODEL,), tm)
    xo = _xattn(xq.reshape(b, -1, D_MODEL), mem_k, mem_v, bt_x, tq_x)
    h2 = _matmul_res(xo.reshape(-1, D_MODEL), wts["w_co"], h1, tm)
    return _ffn(h2, wts["g_ffn"], wts["w_gate"], wts["w_up"], wts["w_down"], wts["g_final"], tm)


def kernel(x_prompt, x_sample, mem_prompt, cache_win_k, cache_win_v, cache_mem_k, cache_mem_v, state_conv, state_mlstm_C, state_mlstm_n, state_mlstm_m, g_mix, w_in, conv_w, conv_b, b_ig, b_fg, b_og, g_attn_out, g_mlstm_out, w_out, g_cross, g_mem, w_cq, w_ck, w_cv, w_co, g_ffn, w_gate, w_up, w_down, g_final):
    bp, sp, _ = x_prompt.shape
    bs, ts, _ = x_sample.shape
    wts = dict(b_og=b_og, g_attn_out=g_attn_out, g_mlstm_out=g_mlstm_out, g_cross=g_cross, g_ffn=g_ffn,
               g_final=g_final, w_out=w_out.astype(BF16), w_cq=w_cq.astype(BF16), w_co=w_co.astype(BF16),
               w_gate=w_gate.astype(BF16), w_up=w_up.astype(BF16), w_down=w_down.astype(BF16))
    w_in_p = jnp.pad(w_in, ((0, 0), (0, sum(IN_WIDTHS) - N_IN))).astype(BF16)
    w_ckv = jnp.concatenate([w_ck, w_cv], axis=1).astype(BF16)
    tm = 512

    xp2 = x_prompt.reshape(bp * sp, D_MODEL)
    aq, ak, av, qk_pre, mv, mo, gates = _norm_matmul(xp2, g_mix, w_in_p, IN_WIDTHS, tm)
    r3 = lambda a: a.reshape(bp, sp, -1)
    att = _attn_prompt(r3(aq), r3(ak), r3(av))
    keep = min(DIL_GROUPS[-1][0], sp)
    win_k_p = r3(ak)[:, sp - keep:].reshape(bp, keep, N_HEADS, HEAD_DIM)
    win_v_p = r3(av)[:, sp - keep:].reshape(bp, keep, N_HEADS, HEAD_DIM)
    mlh, conv_p, c_p, n_p, m_p = _mlstm(
        r3(qk_pre), jnp.zeros((bp, CONV_W - 1, 2 * MIX_W), F32), conv_w, conv_b, r3(mv), r3(gates), b_ig, b_fg,
        jnp.zeros((bp, N_HEADS, HEAD_DIM, HEAD_DIM), F32), jnp.zeros((bp, N_HEADS, HEAD_DIM), F32),
        jnp.full((bp, N_HEADS), NEG, F32), chunk=256, bt=1)
    mem_k_p, mem_v_p = _norm_matmul(mem_prompt.reshape(bp * N_MEM, D_MODEL), g_mem, w_ckv, (D_MODEL, D_MODEL), tm)
    mem_k_p = mem_k_p.reshape(bp, N_MEM, D_MODEL)
    mem_v_p = mem_v_p.reshape(bp, N_MEM, D_MODEL)
    y_p = _mixers_to_output(xp2, att.reshape(-1, MIX_W), mlh.reshape(-1, MIX_W), mo, mem_k_p, mem_v_p, wts,
                            bt_x=1, tq_x=512, tm=tm)

    xs2 = x_sample.reshape(bs * ts, D_MODEL)
    aq, ak, av, qk_pre, mv, mo, gates = _norm_matmul(xs2, g_mix, w_in_p, IN_WIDTHS, tm)
    r3 = lambda a: a.reshape(bs, ts, -1)
    wb = cache_win_k.shape[1]
    att, win_k_s, win_v_s = _attn_sample(r3(aq), r3(ak), r3(av), cache_win_k.reshape(bs, wb, MIX_W),
                                         cache_win_v.reshape(bs, wb, MIX_W))
    mlh, conv_s, c_s, n_s, m_s = _mlstm(r3(qk_pre), state_conv, conv_w, conv_b, r3(mv), r3(gates), b_ig, b_fg,
                                        state_mlstm_C, state_mlstm_n, state_mlstm_m, chunk=ts, bt=8)
    y_s = _mixers_to_output(xs2, att.reshape(-1, MIX_W), mlh.reshape(-1, MIX_W), mo,
                            cache_mem_k.reshape(bs, N_MEM, D_MODEL), cache_mem_v.reshape(bs, N_MEM, D_MODEL), wts,
                            bt_x=4, tq_x=ts, tm=tm)

    return (y_p.reshape(bp, sp, D_MODEL), y_s.reshape(bs, ts, D_MODEL),
            win_k_p, win_v_p, conv_p, c_p, n_p, m_p,
            mem_k_p.reshape(bp, N_MEM, N_X_HEADS, X_HEAD_DIM), mem_v_p.reshape(bp, N_MEM, N_X_HEADS, X_HEAD_DIM),
            win_k_s.reshape(bs, wb, N_HEADS, HEAD_DIM), win_v_s.reshape(bs, wb, N_HEADS, HEAD_DIM),
            conv_s, c_s, n_s, m_s)
```

```python
import functools

import numpy as np
import jax
import jax.numpy as jnp
from jax import lax
from jax.experimental import pallas as pl
from jax.experimental.pallas import tpu as pltpu

F32 = jnp.float32
BF16 = jnp.bfloat16

D_MODEL = 1024
HEAD_DIM = 64
N_HEADS = 8
MIX_W = N_HEADS * HEAD_DIM
DIL_GROUPS = ((128, 1), (512, 4), (2048, 16))
BAND = 128
CONV_W = 4
N_MEM = 256
N_X_HEADS = 4
X_HEAD_DIM = D_MODEL // N_X_HEADS
D_FF = 2816
RMS_EPS = 1e-6
NEG = -1e30
N_IN = 3 * MIX_W + 2 * MIX_W + MIX_W + MIX_W + 2 * N_HEADS
GATE_PAD = 128

LANES = 128
SUBLANES = 8
VMEM_BYTES_V7X = 64 * 1024 * 1024


def _vmem_limit(nbytes):
    return int(min(max(2 * nbytes, 16 * 1024 * 1024), VMEM_BYTES_V7X - 8 * 1024 * 1024))


def _params(semantics, vmem_estimate):
    return pltpu.CompilerParams(dimension_semantics=semantics,
                                vmem_limit_bytes=_vmem_limit(vmem_estimate))


def _rms(x, g):
    return x * lax.rsqrt(jnp.mean(x * x, axis=-1, keepdims=True) + RMS_EPS) * g


def _sigmoid(x):
    return 1.0 / (1.0 + jnp.exp(-x))


def _log_sigmoid(x):
    return jnp.minimum(x, 0.0) - jnp.log(1.0 + jnp.exp(-jnp.abs(x)))


def _dot(a, b):
    return jnp.dot(a, b, preferred_element_type=F32)


def _dot_nt(a, b):
    return lax.dot_general(a, b, (((1,), (1,)), ((), ())), preferred_element_type=F32)


def _dot_tn(a, b):
    return lax.dot_general(a, b, (((0,), (0,)), ((), ())), preferred_element_type=F32)


def _norm_matmul_kernel(x_ref, g_ref, w_ref, *o_refs, splits, transposed):
    xn = _rms(x_ref[...], g_ref[...]).astype(BF16)
    t_refs = o_refs[len(splits):]
    for idx, (o_ref, (a, b)) in enumerate(zip(o_refs, splits)):
        y = _dot(xn, w_ref[:, a:b])
        o_ref[...] = y
        if idx in transposed:
            t_refs[transposed.index(idx)][0] = y.T


def _norm_matmul(x, g, w, widths, tm, transposed=(), seq=None):
    m, k = x.shape
    n = w.shape[1]
    assert sum(widths) == n and m % tm == 0
    offs = np.cumsum([0] + list(widths))
    splits = tuple((int(offs[i]), int(offs[i + 1])) for i in range(len(widths)))
    est = 2 * tm * k * 4 + 2 * k * n * 2 + 2 * tm * n * 4 + 4 * len(transposed) * tm * MIX_W * 4
    out_specs = [pl.BlockSpec((tm, wd), lambda i: (i, 0)) for wd in widths]
    out_shape = [jax.ShapeDtypeStruct((m, wd), F32) for wd in widths]
    if transposed:
        assert seq % tm == 0
        per_seq = seq // tm
        for idx in transposed:
            out_specs.append(pl.BlockSpec((1, widths[idx], tm), lambda i: (i // per_seq, 0, i % per_seq)))
            out_shape.append(jax.ShapeDtypeStruct((m // seq, widths[idx], seq), F32))
    return pl.pallas_call(
        functools.partial(_norm_matmul_kernel, splits=splits, transposed=tuple(transposed)),
        grid=(m // tm,),
        in_specs=[pl.BlockSpec((tm, k), lambda i: (i, 0)),
                  pl.BlockSpec((1, k), lambda i: (0, 0)),
                  pl.BlockSpec((k, n), lambda i: (0, 0))],
        out_specs=out_specs,
        out_shape=out_shape,
        compiler_params=_params(("parallel",), est),
        name="norm_matmul",
    )(x, g.reshape(1, k), w)


def _matmul_res_kernel(x_ref, w_ref, r_ref, o_ref):
    o_ref[...] = r_ref[...] + _dot(x_ref[...].astype(BF16), w_ref[...])


def _matmul_res(x, w, res, tm):
    m, k = x.shape
    n = w.shape[1]
    est = 2 * tm * k * 4 + 2 * k * n * 2 + 4 * tm * n * 4
    return pl.pallas_call(
        _matmul_res_kernel,
        grid=(m // tm,),
        in_specs=[pl.BlockSpec((tm, k), lambda i: (i, 0)),
                  pl.BlockSpec((k, n), lambda i: (0, 0)),
                  pl.BlockSpec((tm, n), lambda i: (i, 0))],
        out_specs=pl.BlockSpec((tm, n), lambda i: (i, 0)),
        out_shape=jax.ShapeDtypeStruct((m, n), F32),
        compiler_params=_params(("parallel",), est),
        name="matmul_res",
    )(x, w, res)


def _merge_kernel(att_ref, mlh_ref, mo_ref, bog_ref, ga_ref, gm_ref, w_ref, x_ref, o_ref):
    a_n = _rms(att_ref[...], ga_ref[...]).astype(BF16)
    ml = _sigmoid(mo_ref[...] + bog_ref[...]) * mlh_ref[...]
    m_n = _rms(ml, gm_ref[...]).astype(BF16)
    o_ref[...] = x_ref[...] + _dot(a_n, w_ref[0:MIX_W, :]) + _dot(m_n, w_ref[MIX_W:2 * MIX_W, :])


def _merge(att, mlh, mo, b_og, g_a, g_m, w_out, x, tm):
    m = x.shape[0]
    half = lambda: pl.BlockSpec((tm, MIX_W), lambda i: (i, 0))
    vec = lambda: pl.BlockSpec((1, MIX_W), lambda i: (0, 0))
    est = 6 * tm * MIX_W * 4 + 2 * 2 * MIX_W * D_MODEL * 2 + 4 * tm * D_MODEL * 4
    return pl.pallas_call(
        _merge_kernel,
        grid=(m // tm,),
        in_specs=[half(), half(), half(), vec(), vec(), vec(),
                  pl.BlockSpec((2 * MIX_W, D_MODEL), lambda i: (0, 0)),
                  pl.BlockSpec((tm, D_MODEL), lambda i: (i, 0))],
        out_specs=pl.BlockSpec((tm, D_MODEL), lambda i: (i, 0)),
        out_shape=jax.ShapeDtypeStruct((m, D_MODEL), F32),
        compiler_params=_params(("parallel",), est),
        name="merge",
    )(att, mlh, mo, b_og.reshape(1, -1), g_a.reshape(1, -1), g_m.reshape(1, -1), w_out, x)


FF_CHUNK = D_FF // 2


def _ffn_kernel(h_ref, gf_ref, wg_ref, wu_ref, wd_ref, gl_ref, o_ref):
    h = h_ref[...]
    hn = _rms(h, gf_ref[...]).astype(BF16)
    acc = h
    for c in range(D_FF // FF_CHUNK):
        sl = slice(c * FF_CHUNK, (c + 1) * FF_CHUNK)
        gate = _dot(hn, wg_ref[:, sl])
        up = _dot(hn, wu_ref[:, sl])
        act = (gate * _sigmoid(gate) * up).astype(BF16)
        acc = acc + _dot(act, wd_ref[sl, :])
    o_ref[...] = _rms(acc, gl_ref[...])


def _ffn(h, g_ffn, w_gate, w_up, w_down, g_final, tm):
    m = h.shape[0]
    est = 4 * tm * D_MODEL * 4 + 2 * 3 * D_MODEL * D_FF * 2 + 3 * tm * FF_CHUNK * 4
    return pl.pallas_call(
        _ffn_kernel,
        grid=(m // tm,),
        in_specs=[pl.BlockSpec((tm, D_MODEL), lambda i: (i, 0)),
                  pl.BlockSpec((1, D_MODEL), lambda i: (0, 0)),
                  pl.BlockSpec((D_MODEL, D_FF), lambda i: (0, 0)),
                  pl.BlockSpec((D_MODEL, D_FF), lambda i: (0, 0)),
                  pl.BlockSpec((D_FF, D_MODEL), lambda i: (0, 0)),
                  pl.BlockSpec((1, D_MODEL), lambda i: (0, 0))],
        out_specs=pl.BlockSpec((tm, D_MODEL), lambda i: (i, 0)),
        out_shape=jax.ShapeDtypeStruct((m, D_MODEL), F32),
        compiler_params=_params(("parallel",), est),
        name="ffn",
    )(h, g_ffn.reshape(1, -1), w_gate, w_up, w_down, g_final.reshape(1, -1))


X_HALVES = X_HEAD_DIM // LANES
X_ROWS = N_X_HEADS * X_HALVES


def _mem_tile_view(a):
    b = a.shape[0]
    a = a.reshape(b, N_MEM, N_X_HEADS, X_HALVES, LANES)
    return jnp.transpose(a, (0, 1, 3, 2, 4)).reshape(b, N_MEM * X_ROWS, LANES)


def _xattn_kernel(q_ref, k_ref, v_ref, o_ref, *, bt):
    def one(bi, carry):
        for h in range(N_X_HEADS):
            kv_rows = [pl.ds(j * N_X_HEADS + h, N_MEM, stride=X_ROWS) for j in range(X_HALVES)]
            s = None
            for j, rows in enumerate(kv_rows):
                sl = slice(h * X_HEAD_DIM + j * LANES, h * X_HEAD_DIM + (j + 1) * LANES)
                q = (q_ref[bi, :, sl] * (X_HEAD_DIM ** -0.5)).astype(BF16)
                part = _dot_nt(q, k_ref[bi, rows, :].astype(BF16))
                s = part if s is None else s + part
            p = jnp.exp(s - jnp.max(s, axis=-1, keepdims=True))
            inv_l = 1.0 / jnp.sum(p, axis=-1, keepdims=True)
            pb = p.astype(BF16)
            for j, rows in enumerate(kv_rows):
                sl = slice(h * X_HEAD_DIM + j * LANES, h * X_HEAD_DIM + (j + 1) * LANES)
                o_ref[bi, :, sl] = _dot(pb, v_ref[bi, rows, :].astype(BF16)) * inv_l
        return carry
    if bt == 1:
        one(0, 0)
    else:
        lax.fori_loop(0, bt, one, 0, unroll=2)


def _xattn(q, mem_k, mem_v, bt, tq):
    b, t, _ = q.shape
    est = 4 * bt * tq * D_MODEL * 4 + 4 * bt * N_MEM * D_MODEL * 4
    kv = lambda: pl.BlockSpec((bt, N_MEM * X_ROWS, LANES), lambda i, j: (i, 0, 0))
    return pl.pallas_call(
        functools.partial(_xattn_kernel, bt=bt),
        grid=(b // bt, t // tq),
        in_specs=[pl.BlockSpec((bt, tq, D_MODEL), lambda i, j: (i, j, 0)), kv(), kv()],
        out_specs=pl.BlockSpec((bt, tq, D_MODEL), lambda i, j: (i, j, 0)),
        out_shape=jax.ShapeDtypeStruct((b, t, D_MODEL), F32),
        compiler_params=_params(("parallel", "parallel"), est),
        name="xattn",
    )(q, mem_k, mem_v)


def _alibi_slopes():
    return np.exp2(-8.0 * (np.arange(N_HEADS, dtype=np.float64) + 1.0) / N_HEADS)


def _prompt_bias_table():
    qi = np.arange(BAND)[:, None]
    kj = np.arange(2 * BAND)[None, :]
    delta = BAND + qi - kj
    valid = (delta >= 0) & (delta <= BAND)
    tabs = []
    for _, dil in DIL_GROUPS:
        bias = -_alibi_slopes()[:, None, None] * (delta * dil)[None].astype(np.float64)
        tabs.append(np.where(valid[None], bias, NEG))
    return np.stack(tabs).astype(np.float32)


UNITS_PER_TRIP = 5


def _largest_divisor(n, cap):
    return max(u for u in range(1, cap + 1) if n % u == 0)


def _attn_prompt_kernel(q_ref, k_ref, v_ref, bias_ref, o_ref, og0, og1, og2, lg0, lg1, lg2, *, seq):
    og = (og0, og1, og2)
    lg = (lg0, lg1, lg2)
    lane = lax.broadcasted_iota(jnp.int32, (1, LANES), 1)
    first = lane < HEAD_DIM

    def rows(start, n, dil):
        return pl.ds(start, n) if dil == 1 else pl.ds(start, n, stride=dil)

    def unit(g, dil, q_start, k_start, nk):
        q = q_ref[0, rows(q_start, BAND, dil), :] * (HEAD_DIM ** -0.5)
        k = k_ref[0, rows(k_start, nk, dil), :].astype(BF16)
        v = v_ref[0, rows(k_start, nk, dil), :].astype(BF16)
        outs, lses = [], []
        for h in range(2):
            qh = jnp.where(first if h == 0 else jnp.logical_not(first), q, 0.0).astype(BF16)
            if nk == 2 * BAND:
                bias = bias_ref[g, h]
            else:
                bias = bias_ref[g, h, :, BAND:2 * BAND]
            s = _dot_nt(qh, k) + bias
            m = jnp.max(s, axis=-1, keepdims=True)
            p = jnp.exp(s - m)
            l = jnp.sum(p, axis=-1, keepdims=True)
            outs.append(_dot(p.astype(BF16), v) * (1.0 / l))
            lses.append(m + jnp.log(l))
        og[g][rows(q_start, BAND, dil), :] = jnp.where(first, outs[0], outs[1])
        lg[g][rows(q_start, BAND, dil), :] = jnp.where(first, lses[0], lses[1])

    for g, (window, dil) in enumerate(DIL_GROUPS):
        assert window // dil == BAND
        span = BAND * dil
        nblk = seq // span
        if dil <= UNITS_PER_TRIP:
            for r in range(dil):
                unit(g, dil, r, r, BAND)
        else:
            def body1(r, carry, g=g, dil=dil):
                unit(g, dil, r, r, BAND)
                return carry
            lax.fori_loop(0, dil, body1, 0, unroll=_largest_divisor(dil, UNITS_PER_TRIP))
        rest = dil * (nblk - 1)
        if rest:
            def body(j, carry, g=g, dil=dil, span=span, nblk=nblk):
                r = j // (nblk - 1)
                i = j % (nblk - 1) + 1
                unit(g, dil, r + i * span, r + (i - 1) * span, 2 * BAND)
                return carry
            lax.fori_loop(0, rest, body, 0, unroll=_largest_divisor(rest, UNITS_PER_TRIP))

    chunk = 256

    def merge(i, carry):
        sl = pl.ds(pl.multiple_of(i * chunk, chunk), chunk)
        l0, l1, l2 = lg0[sl, :], lg1[sl, :], lg2[sl, :]
        mx = jnp.maximum(jnp.maximum(l0, l1), l2)
        e0, e1, e2 = jnp.exp(l0 - mx), jnp.exp(l1 - mx), jnp.exp(l2 - mx)
        o_ref[0, sl, :] = (e0 * og0[sl, :] + e1 * og1[sl, :] + e2 * og2[sl, :]) / (e0 + e1 + e2)
        return carry
    lax.fori_loop(0, seq // chunk, merge, 0)


def _attn_prompt(q, k, v):
    b, s, _ = q.shape
    bias = jnp.asarray(_prompt_bias_table())
    blk = lambda: pl.BlockSpec((1, s, LANES), lambda i, p: (i, 0, p))
    est = 8 * s * LANES * 4 + 6 * s * LANES * 4 + 2 * 3 * 2 * BAND * 2 * BAND * 4
    return pl.pallas_call(
        functools.partial(_attn_prompt_kernel, seq=s),
        grid=(b, MIX_W // LANES),
        in_specs=[blk(), blk(), blk(),
                  pl.BlockSpec((len(DIL_GROUPS), 2, BAND, 2 * BAND), lambda i, p: (0, p, 0, 0))],
        out_specs=blk(),
        out_shape=jax.ShapeDtypeStruct((b, s, MIX_W), F32),
        scratch_shapes=[pltpu.VMEM((s, LANES), F32) for _ in range(6)],
        compiler_params=_params(("parallel", "parallel"), est),
        name="attn_prompt",
    )(q, k, v, bias)


def _sample_bias_tables(wb, t_new):
    slopes = _alibi_slopes()

    def table(dist):
        mult = np.zeros(dist.shape)
        for window, dil in DIL_GROUPS:
            mult += (dist >= 0) & (dist <= window) & (dist % dil == 0)
        logm = np.where(mult > 0, np.log(np.maximum(mult, 1.0)), NEG)
        tab = -slopes[:, None, None] * dist[None].astype(np.float64) + logm[None]
        tab = np.where((mult > 0)[None], tab, NEG)
        return tab.reshape(N_HEADS * t_new, dist.shape[1]).astype(np.float32)

    t = np.arange(t_new)[:, None]
    return table(wb + t - np.arange(wb)[None, :]), table(t - np.arange(t_new)[None, :])


def _attn_sample_kernel(q_ref, kn_ref, vn_ref, wk_ref, wv_ref, bo_ref, bn_ref,
                        att_ref, ok_ref, ov_ref, *, wb, t_new):
    rows_q = N_HEADS * t_new
    q = q_ref[0] * (HEAD_DIM ** -0.5)
    qm = jnp.concatenate([q] * N_HEADS, axis=0)
    row_head = lax.broadcasted_iota(jnp.int32, (rows_q, MIX_W), 0) // t_new
    lane_head = lax.broadcasted_iota(jnp.int32, (rows_q, MIX_W), 1) // HEAD_DIM
    own = row_head == lane_head
    qm = jnp.where(own, qm, 0.0).astype(BF16)
    k_new, v_new = kn_ref[0], vn_ref[0]
    wk, wv = wk_ref[0], wv_ref[0]
    s_old = _dot(qm, wk.astype(BF16)) + bo_ref[...]
    s_new = _dot_nt(qm, k_new.astype(BF16)) + bn_ref[...]
    m = jnp.maximum(jnp.max(s_old, axis=-1, keepdims=True), jnp.max(s_new, axis=-1, keepdims=True))
    p_old = jnp.exp(s_old - m)
    p_new = jnp.exp(s_new - m)
    inv_l = 1.0 / (jnp.sum(p_old, axis=-1, keepdims=True) + jnp.sum(p_new, axis=-1, keepdims=True))
    o = _dot_nt(p_old.astype(BF16), wv.astype(BF16)) + _dot(p_new.astype(BF16), v_new.astype(BF16))
    o = jnp.where(own, o * inv_l, 0.0)
    att = o[0:t_new]
    for h in range(1, N_HEADS):
        att = att + o[h * t_new:(h + 1) * t_new]
    att_ref[0] = att
    lane = lax.broadcasted_iota(jnp.int32, (MIX_W, LANES), 1)
    is_new = lane >= LANES - t_new
    pad = jnp.zeros((LANES - t_new, MIX_W), F32)
    for w, new, o_ref in ((wk, k_new, ok_ref), (wv, v_new, ov_ref)):
        rolled = pltpu.roll(w, wb - t_new, axis=1)
        o_ref[0] = rolled
        new_t = jnp.concatenate([pad, new], axis=0).T
        o_ref[0, :, wb - LANES:wb] = jnp.where(is_new, new_t, rolled[:, wb - LANES:wb])


def _attn_sample(q, k_new, v_new, win_k_t, win_v_t):
    b, t_new, _ = q.shape
    wb = win_k_t.shape[2]
    assert wb >= DIL_GROUPS[-1][0] and t_new % SUBLANES == 0 and wb % LANES == 0
    b_old, b_new = _sample_bias_tables(wb, t_new)
    small = lambda: pl.BlockSpec((1, t_new, MIX_W), lambda i: (i, 0, 0))
    big = lambda: pl.BlockSpec((1, MIX_W, wb), lambda i: (i, 0, 0))
    est = 8 * wb * MIX_W * 4 + 2 * wb * MIX_W * 2 + 4 * N_HEADS * t_new * wb * 4
    return pl.pallas_call(
        functools.partial(_attn_sample_kernel, wb=wb, t_new=t_new),
        grid=(b,),
        in_specs=[small(), small(), small(), big(), big(),
                  pl.BlockSpec((N_HEADS * t_new, wb), lambda i: (0, 0)),
                  pl.BlockSpec((N_HEADS * t_new, t_new), lambda i: (0, 0))],
        out_specs=[small(), big(), big()],
        out_shape=[jax.ShapeDtypeStruct((b, t_new, MIX_W), F32),
                   jax.ShapeDtypeStruct((b, MIX_W, wb), F32),
                   jax.ShapeDtypeStruct((b, MIX_W, wb), F32)],
        compiler_params=_params(("parallel",), est),
        name="attn_sample",
    )(q, k_new, v_new, win_k_t, win_v_t, jnp.asarray(b_old), jnp.asarray(b_new))


def _mlstm_kernel(qk_ref, cinit_ref, cw_ref, cb_ref, mv_ref, gcol_ref, mirow_ref, mfrow_ref,
                  bcol_ref, bigrow_ref, bfgrow_ref, c0_ref, n0_ref, m0_ref,
                  h_ref, cst_ref, c_ref, n_ref, m_ref, xp_scr, *, chunk, bt):
    L = chunk

    @pl.when(pl.program_id(1) == 0)
    def _():
        c_ref[...] = c0_ref[...]
        n_ref[...] = n0_ref[...]
        m_ref[...] = m0_ref[...]
        xp_scr[:, 0:SUBLANES, :] = cinit_ref[...]

    rr = lax.broadcasted_iota(jnp.int32, (L, L), 0)
    cc = lax.broadcasted_iota(jnp.int32, (L, L), 1)
    causal = rr >= cc
    tril = causal.astype(F32)
    triu = (rr <= cc).astype(F32)
    hp = lax.Precision.HIGHEST

    def one(bi, carry):
        xp_scr[bi, SUBLANES:SUBLANES + L, :] = qk_ref[bi]
        conv = cb_ref[...]
        for j in range(CONV_W):
            off = SUBLANES - (CONV_W - 1) + j
            conv = conv + xp_scr[bi, off:off + L, :] * cw_ref[j:j + 1, :]
        tail = xp_scr[bi, L:L + SUBLANES, :]
        xp_scr[bi, 0:SUBLANES, :] = tail
        cst_ref[bi] = tail
        qk = conv * _sigmoid(conv)
        v_all = mv_ref[bi]

        pre_c = gcol_ref[bi] + bcol_ref[...]
        bcum_c = jnp.dot(tril, _log_sigmoid(pre_c), precision=hp, preferred_element_type=F32)
        ig_r_all = mirow_ref[bi] + bigrow_ref[...]
        bcum_r = jnp.dot(_log_sigmoid(mfrow_ref[bi] + bfgrow_ref[...]), triu, precision=hp,
                         preferred_element_type=F32)

        c_all, n_all, m_all = c_ref[bi], n_ref[bi], m_ref[bi]
        new_state = []
        for h in range(N_HEADS):
            sl = slice(h * HEAD_DIM, (h + 1) * HEAD_DIM)
            q = qk[:, sl]
            k = qk[:, MIX_W + h * HEAD_DIM:MIX_W + (h + 1) * HEAD_DIM] * (HEAD_DIM ** -0.5)
            v = v_all[:, sl]
            ig_c = pre_c[:, h:h + 1]
            b_c = bcum_c[:, N_HEADS + h:N_HEADS + h + 1]
            b_r = bcum_r[h:h + 1, :]
            ig_r = ig_r_all[h:h + 1, :]
            c_prev = c_all[h]
            n_prev = n_all[h:h + 1, :]
            m_prev = m_all[:, h:h + 1]

            log_d = jnp.where(causal, b_c - b_r + ig_r, NEG)
            log_inter = b_c + m_prev
            m_t = jnp.maximum(log_inter, jnp.max(log_d, axis=-1, keepdims=True))
            d_w = jnp.exp(log_d - m_t)
            inter_w = jnp.exp(log_inter - m_t)
            qb, kb, vb = q.astype(BF16), k.astype(BF16), v.astype(BF16)
            a = _dot_nt(qb, kb) * d_w
            num = _dot(a.astype(BF16), vb) + inter_w * _dot(qb, c_prev.astype(BF16))
            den = jnp.sum(a, axis=-1, keepdims=True) + inter_w * jnp.sum(q * n_prev, axis=-1, keepdims=True)
            h_ref[bi, :, sl] = num * (1.0 / jnp.maximum(jnp.abs(den), jnp.exp(-m_t)))

            m_new = m_t[L - 1:L, :]
            b_last = b_c[L - 1:L, :]
            w_s = jnp.exp(b_last - b_c + ig_c - m_new)
            decay = jnp.exp(b_last + m_prev - m_new)
            kw = k * w_s
            new_state.append((decay * c_prev + _dot_tn(kw.astype(BF16), vb),
                              decay * n_prev + jnp.sum(kw, axis=0, keepdims=True), m_new))
        for h, (c_new, n_new, m_new) in enumerate(new_state):
            c_ref[bi, h] = c_new
            n_ref[bi, h:h + 1, :] = n_new
            m_ref[bi, :, h:h + 1] = m_new
        return carry

    if bt == 1:
        one(0, 0)
    else:
        lax.fori_loop(0, bt, one, 0, unroll=2)


def _mlstm(qk_pre, conv_init, conv_w, conv_b, mv, gates, b_ig, b_fg, c0, n0, m0, chunk, bt):
    b, s, _ = qk_pre.shape
    assert s % chunk == 0 and b % bt == 0 and chunk % SUBLANES == 0
    cinit = jnp.pad(conv_init, ((0, 0), (SUBLANES - (CONV_W - 1), 0), (0, 0)))
    cw = jnp.pad(conv_w, ((0, SUBLANES - CONV_W), (0, 0)))
    g_rows = jnp.swapaxes(gates[:, :, :2 * N_HEADS], 1, 2)
    mi_row, mf_row = g_rows[:, :N_HEADS], g_rows[:, N_HEADS:]
    bcol = jnp.concatenate([b_ig, b_fg, jnp.zeros((GATE_PAD - 2 * N_HEADS,), F32)]).reshape(1, GATE_PAD)
    nc = s // chunk
    seq3 = lambda w: pl.BlockSpec((bt, chunk, w), lambda i, c: (i, c, 0))
    fix3 = lambda r, w: pl.BlockSpec((bt, r, w), lambda i, c: (i, 0, 0))
    row3 = lambda: pl.BlockSpec((bt, N_HEADS, chunk), lambda i, c: (i, 0, c))
    const = lambda r, w: pl.BlockSpec((r, w), lambda i, c: (0, 0))
    cspec = lambda: pl.BlockSpec((bt, N_HEADS, HEAD_DIM, HEAD_DIM), lambda i, c: (i, 0, 0, 0))
    est = (2 * bt * chunk * (2 * MIX_W + MIX_W + GATE_PAD + MIX_W) * 4 + bt * (chunk + SUBLANES) * 2 * MIX_W * 4
           + 6 * bt * N_HEADS * HEAD_DIM * HEAD_DIM * 4 + 8 * chunk * chunk * 4 + 6 * chunk * 2 * MIX_W * 4)
    outs = pl.pallas_call(
        functools.partial(_mlstm_kernel, chunk=chunk, bt=bt),
        grid=(b // bt, nc),
        in_specs=[seq3(2 * MIX_W), fix3(SUBLANES, 2 * MIX_W), const(SUBLANES, 2 * MIX_W), const(1, 2 * MIX_W),
                  seq3(MIX_W), seq3(GATE_PAD), row3(), row3(),
                  const(1, GATE_PAD), const(N_HEADS, 1), const(N_HEADS, 1),
                  cspec(), fix3(N_HEADS, HEAD_DIM), fix3(1, N_HEADS)],
        out_specs=[seq3(MIX_W), fix3(SUBLANES, 2 * MIX_W), cspec(), fix3(N_HEADS, HEAD_DIM), fix3(1, N_HEADS)],
        out_shape=[jax.ShapeDtypeStruct((b, s, MIX_W), F32),
                   jax.ShapeDtypeStruct((b, SUBLANES, 2 * MIX_W), F32),
                   jax.ShapeDtypeStruct((b, N_HEADS, HEAD_DIM, HEAD_DIM), F32),
                   jax.ShapeDtypeStruct((b, N_HEADS, HEAD_DIM), F32),
                   jax.ShapeDtypeStruct((b, 1, N_HEADS), F32)],
        scratch_shapes=[pltpu.VMEM((bt, chunk + SUBLANES, 2 * MIX_W), F32)],
        compiler_params=_params(("arbitrary", "arbitrary"), est),
        name="mlstm",
    )(qk_pre, cinit, cw, conv_b.reshape(1, -1), mv, gates, mi_row, mf_row,
      bcol, b_ig.reshape(-1, 1), b_fg.reshape(-1, 1), c0, n0, m0.reshape(b, 1, N_HEADS))
    h, cst, c, n, m = outs
    return h, cst[:, SUBLANES - (CONV_W - 1):], c, n, m.reshape(b, N_HEADS)


IN_WIDTHS = (MIX_W, MIX_W, MIX_W, 2 * MIX_W, MIX_W, MIX_W, GATE_PAD)


def _mixers_to_output(x2, att, mlh, mo, mem_k, mem_v, wts, bt_x, tq_x, tm):
    b = mem_k.shape[0]
    h1 = _merge(att, mlh, mo, wts["b_og"], wts["g_attn_out"], wts["g_mlstm_out"], wts["w_out"], x2, tm)
    (xq,) = _norm_matmul(h1, wts["g_cross"], wts["w_cq"], (D_MODEL,), tm)
    xo = _xattn(xq.reshape(b, -1, D_MODEL), _mem_tile_view(mem_k), _mem_tile_view(mem_v), bt_x, tq_x)
    h2 = _matmul_res(xo.reshape(-1, D_MODEL), wts["w_co"], h1, tm)
    return _ffn(h2, wts["g_ffn"], wts["w_gate"], wts["w_up"], wts["w_down"], wts["g_final"], tm)


def kernel(x_prompt, x_sample, mem_prompt, cache_win_k, cache_win_v, cache_mem_k, cache_mem_v, state_conv, state_mlstm_C, state_mlstm_n, state_mlstm_m, g_mix, w_in, conv_w, conv_b, b_ig, b_fg, b_og, g_attn_out, g_mlstm_out, w_out, g_cross, g_mem, w_cq, w_ck, w_cv, w_co, g_ffn, w_gate, w_up, w_down, g_final):
    bp, sp, _ = x_prompt.shape
    bs, ts, _ = x_sample.shape
    wts = dict(b_og=b_og, g_attn_out=g_attn_out, g_mlstm_out=g_mlstm_out, g_cross=g_cross, g_ffn=g_ffn,
               g_final=g_final, w_out=w_out.astype(BF16), w_cq=w_cq.astype(BF16), w_co=w_co.astype(BF16),
               w_gate=w_gate.astype(BF16), w_up=w_up.astype(BF16), w_down=w_down.astype(BF16))
    w_in_p = jnp.pad(w_in, ((0, 0), (0, sum(IN_WIDTHS) - N_IN))).astype(BF16)
    w_ckv = jnp.concatenate([w_ck, w_cv], axis=1).astype(BF16)
    tm = 512

    xp2 = x_prompt.reshape(bp * sp, D_MODEL)
    aq, ak, av, qk_pre, mv, mo, gates, ak_t, av_t = _norm_matmul(xp2, g_mix, w_in_p, IN_WIDTHS, tm,
                                                                 transposed=(1, 2), seq=sp)
    r3 = lambda a: a.reshape(bp, sp, -1)
    att = _attn_prompt(r3(aq), r3(ak), r3(av))
    keep = min(DIL_GROUPS[-1][0], sp)
    from_t = lambda a: jnp.transpose(a.reshape(a.shape[0], N_HEADS, HEAD_DIM, a.shape[2]), (0, 3, 1, 2))
    win_k_p = from_t(ak_t)[:, sp - keep:]
    win_v_p = from_t(av_t)[:, sp - keep:]
    mlh, conv_p, c_p, n_p, m_p = _mlstm(
        r3(qk_pre), jnp.zeros((bp, CONV_W - 1, 2 * MIX_W), F32), conv_w, conv_b, r3(mv), r3(gates), b_ig, b_fg,
        jnp.zeros((bp, N_HEADS, HEAD_DIM, HEAD_DIM), F32), jnp.zeros((bp, N_HEADS, HEAD_DIM), F32),
        jnp.full((bp, N_HEADS), NEG, F32), chunk=256, bt=1)
    mem_k_p, mem_v_p = _norm_matmul(mem_prompt.reshape(bp * N_MEM, D_MODEL), g_mem, w_ckv, (D_MODEL, D_MODEL), tm)
    mem_k_p = mem_k_p.reshape(bp, N_MEM, N_X_HEADS, X_HEAD_DIM)
    mem_v_p = mem_v_p.reshape(bp, N_MEM, N_X_HEADS, X_HEAD_DIM)
    y_p = _mixers_to_output(xp2, att.reshape(-1, MIX_W), mlh.reshape(-1, MIX_W), mo, mem_k_p, mem_v_p, wts,
                            bt_x=1, tq_x=512, tm=tm)

    xs2 = x_sample.reshape(bs * ts, D_MODEL)
    aq, ak, av, qk_pre, mv, mo, gates = _norm_matmul(xs2, g_mix, w_in_p, IN_WIDTHS, tm)
    r3 = lambda a: a.reshape(bs, ts, -1)
    wb = cache_win_k.shape[1]
    to_t = lambda a: jnp.transpose(a, (0, 2, 3, 1)).reshape(a.shape[0], MIX_W, a.shape[1])
    att, win_k_s, win_v_s = _attn_sample(r3(aq), r3(ak), r3(av), to_t(cache_win_k), to_t(cache_win_v))
    mlh, conv_s, c_s, n_s, m_s = _mlstm(r3(qk_pre), state_conv, conv_w, conv_b, r3(mv), r3(gates), b_ig, b_fg,
                                        state_mlstm_C, state_mlstm_n, state_mlstm_m, chunk=ts, bt=8)
    y_s = _mixers_to_output(xs2, att.reshape(-1, MIX_W), mlh.reshape(-1, MIX_W), mo, cache_mem_k, cache_mem_v, wts,
                            bt_x=4, tq_x=ts, tm=tm)

    return (y_p.reshape(bp, sp, D_MODEL), y_s.reshape(bs, ts, D_MODEL),
            win_k_p, win_v_p, conv_p, c_p, n_p, m_p, mem_k_p, mem_v_p,
            from_t(win_k_s), from_t(win_v_s), conv_s, c_s, n_s, m_s)
```

```python
import functools

import numpy as np
import jax
import jax.numpy as jnp
from jax import lax
from jax.experimental import pallas as pl
from jax.experimental.pallas import tpu as pltpu

F32 = jnp.float32
BF16 = jnp.bfloat16

D_MODEL = 1024
HEAD_DIM = 64
N_HEADS = 8
MIX_W = N_HEADS * HEAD_DIM
DIL_GROUPS = ((128, 1), (512, 4), (2048, 16))
BAND = 128
CONV_W = 4
N_MEM = 256
N_X_HEADS = 4
X_HEAD_DIM = D_MODEL // N_X_HEADS
D_FF = 2816
RMS_EPS = 1e-6
NEG = -1e30
N_IN = 3 * MIX_W + 2 * MIX_W + MIX_W + MIX_W + 2 * N_HEADS
GATE_PAD = 128

LANES = 128
SUBLANES = 8
VMEM_BYTES_V7X = 64 * 1024 * 1024


def _vmem_limit(nbytes):
    return int(min(max(2 * nbytes, 16 * 1024 * 1024), VMEM_BYTES_V7X - 8 * 1024 * 1024))


def _params(semantics, vmem_estimate):
    return pltpu.CompilerParams(dimension_semantics=semantics,
                                vmem_limit_bytes=_vmem_limit(vmem_estimate))


def _rms(x, g):
    return x * lax.rsqrt(jnp.mean(x * x, axis=-1, keepdims=True) + RMS_EPS) * g


def _sigmoid(x):
    return 1.0 / (1.0 + jnp.exp(-x))


def _log_sigmoid(x):
    return jnp.minimum(x, 0.0) - jnp.log(1.0 + jnp.exp(-jnp.abs(x)))


def _dot(a, b):
    return jnp.dot(a, b, preferred_element_type=F32)


def _dot_nt(a, b):
    return lax.dot_general(a, b, (((1,), (1,)), ((), ())), preferred_element_type=F32)


def _dot_tn(a, b):
    return lax.dot_general(a, b, (((0,), (0,)), ((), ())), preferred_element_type=F32)


N_CLASSES = max(d for _, d in DIL_GROUPS)


def _norm_matmul_kernel(x_ref, g_ref, w_ref, *refs, splits, transposed, class_major):
    n_nat = len(splits) - len(class_major)
    nat_refs = list(refs[:n_nat])
    cls_refs = list(refs[n_nat:n_nat + len(class_major)])
    t_refs = list(refs[n_nat + len(class_major):n_nat + len(class_major) + len(transposed)])
    scr_refs = list(refs[n_nat + len(class_major) + len(transposed):])
    xn = _rms(x_ref[...], g_ref[...]).astype(BF16)
    tm = x_ref.shape[0]
    for idx, (a, b) in enumerate(splits):
        y = _dot(xn, w_ref[:, a:b])
        if idx in transposed:
            t_refs[transposed.index(idx)][0] = y.T
        if idx not in class_major:
            nat_refs.pop(0)[...] = y
            continue
        c_ref, scr = cls_refs[class_major.index(idx)], scr_refs[class_major.index(idx)]
        for lt in range((b - a) // LANES):
            scr[lt] = y[:, lt * LANES:(lt + 1) * LANES]
        for c in range(N_CLASSES):
            for lt in range((b - a) // LANES):
                c_ref[0, c, :, lt * LANES:(lt + 1) * LANES] = scr[lt, pl.ds(c, tm // N_CLASSES, stride=N_CLASSES), :]


def _norm_matmul(x, g, w, widths, tm, transposed=(), class_major=(), seq=None):
    m, k = x.shape
    n = w.shape[1]
    assert sum(widths) == n and m % tm == 0
    offs = np.cumsum([0] + list(widths))
    splits = tuple((int(offs[i]), int(offs[i + 1])) for i in range(len(widths)))
    est = (2 * tm * k * 4 + 2 * k * n * 2 + 2 * tm * n * 4 + 4 * len(transposed) * tm * MIX_W * 4
           + len(class_major) * tm * MIX_W * 4)
    nat = [i for i in range(len(widths)) if i not in class_major]
    out_specs = [pl.BlockSpec((tm, widths[i]), lambda i_: (i_, 0)) for i in nat]
    out_shape = [jax.ShapeDtypeStruct((m, widths[i]), F32) for i in nat]
    scratch = []
    if transposed or class_major:
        assert seq % tm == 0 and tm % (N_CLASSES * SUBLANES) == 0
        per_seq = seq // tm
        for idx in class_major:
            out_specs.append(pl.BlockSpec((1, N_CLASSES, tm // N_CLASSES, widths[idx]),
                                          lambda i_: (i_ // per_seq, 0, i_ % per_seq, 0)))
            out_shape.append(jax.ShapeDtypeStruct((m // seq, N_CLASSES, seq // N_CLASSES, widths[idx]), F32))
            scratch.append(pltpu.VMEM((widths[idx] // LANES, tm, LANES), F32))
        for idx in transposed:
            out_specs.append(pl.BlockSpec((1, widths[idx], tm), lambda i_: (i_ // per_seq, 0, i_ % per_seq)))
            out_shape.append(jax.ShapeDtypeStruct((m // seq, widths[idx], seq), F32))
    return pl.pallas_call(
        functools.partial(_norm_matmul_kernel, splits=splits, transposed=tuple(transposed),
                          class_major=tuple(class_major)),
        grid=(m // tm,),
        in_specs=[pl.BlockSpec((tm, k), lambda i_: (i_, 0)),
                  pl.BlockSpec((1, k), lambda i_: (0, 0)),
                  pl.BlockSpec((k, n), lambda i_: (0, 0))],
        out_specs=out_specs,
        out_shape=out_shape,
        scratch_shapes=scratch,
        compiler_params=_params(("parallel",), est),
        name="norm_matmul",
    )(x, g.reshape(1, k), w)


def _matmul_res_kernel(x_ref, w_ref, r_ref, o_ref):
    o_ref[...] = r_ref[...] + _dot(x_ref[...].astype(BF16), w_ref[...])


def _matmul_res(x, w, res, tm):
    m, k = x.shape
    n = w.shape[1]
    est = 2 * tm * k * 4 + 2 * k * n * 2 + 4 * tm * n * 4
    return pl.pallas_call(
        _matmul_res_kernel,
        grid=(m // tm,),
        in_specs=[pl.BlockSpec((tm, k), lambda i: (i, 0)),
                  pl.BlockSpec((k, n), lambda i: (0, 0)),
                  pl.BlockSpec((tm, n), lambda i: (i, 0))],
        out_specs=pl.BlockSpec((tm, n), lambda i: (i, 0)),
        out_shape=jax.ShapeDtypeStruct((m, n), F32),
        compiler_params=_params(("parallel",), est),
        name="matmul_res",
    )(x, w, res)


def _merge_kernel(att_ref, mlh_ref, mo_ref, bog_ref, ga_ref, gm_ref, w_ref, x_ref, o_ref):
    a_n = _rms(att_ref[...], ga_ref[...]).astype(BF16)
    ml = _sigmoid(mo_ref[...] + bog_ref[...]) * mlh_ref[...]
    m_n = _rms(ml, gm_ref[...]).astype(BF16)
    o_ref[...] = x_ref[...] + _dot(a_n, w_ref[0:MIX_W, :]) + _dot(m_n, w_ref[MIX_W:2 * MIX_W, :])


def _merge(att, mlh, mo, b_og, g_a, g_m, w_out, x, tm):
    m = x.shape[0]
    half = lambda: pl.BlockSpec((tm, MIX_W), lambda i: (i, 0))
    vec = lambda: pl.BlockSpec((1, MIX_W), lambda i: (0, 0))
    est = 6 * tm * MIX_W * 4 + 2 * 2 * MIX_W * D_MODEL * 2 + 4 * tm * D_MODEL * 4
    return pl.pallas_call(
        _merge_kernel,
        grid=(m // tm,),
        in_specs=[half(), half(), half(), vec(), vec(), vec(),
                  pl.BlockSpec((2 * MIX_W, D_MODEL), lambda i: (0, 0)),
                  pl.BlockSpec((tm, D_MODEL), lambda i: (i, 0))],
        out_specs=pl.BlockSpec((tm, D_MODEL), lambda i: (i, 0)),
        out_shape=jax.ShapeDtypeStruct((m, D_MODEL), F32),
        compiler_params=_params(("parallel",), est),
        name="merge",
    )(att, mlh, mo, b_og.reshape(1, -1), g_a.reshape(1, -1), g_m.reshape(1, -1), w_out, x)


FF_CHUNK = D_FF // 2


def _ffn_kernel(h_ref, gf_ref, wg_ref, wu_ref, wd_ref, gl_ref, o_ref):
    h = h_ref[...]
    hn = _rms(h, gf_ref[...]).astype(BF16)
    acc = h
    for c in range(D_FF // FF_CHUNK):
        sl = slice(c * FF_CHUNK, (c + 1) * FF_CHUNK)
        gate = _dot(hn, wg_ref[:, sl])
        up = _dot(hn, wu_ref[:, sl])
        act = (gate * _sigmoid(gate) * up).astype(BF16)
        acc = acc + _dot(act, wd_ref[sl, :])
    o_ref[...] = _rms(acc, gl_ref[...])


def _ffn(h, g_ffn, w_gate, w_up, w_down, g_final, tm):
    m = h.shape[0]
    est = 4 * tm * D_MODEL * 4 + 2 * 3 * D_MODEL * D_FF * 2 + 3 * tm * FF_CHUNK * 4
    return pl.pallas_call(
        _ffn_kernel,
        grid=(m // tm,),
        in_specs=[pl.BlockSpec((tm, D_MODEL), lambda i: (i, 0)),
                  pl.BlockSpec((1, D_MODEL), lambda i: (0, 0)),
                  pl.BlockSpec((D_MODEL, D_FF), lambda i: (0, 0)),
                  pl.BlockSpec((D_MODEL, D_FF), lambda i: (0, 0)),
                  pl.BlockSpec((D_FF, D_MODEL), lambda i: (0, 0)),
                  pl.BlockSpec((1, D_MODEL), lambda i: (0, 0))],
        out_specs=pl.BlockSpec((tm, D_MODEL), lambda i: (i, 0)),
        out_shape=jax.ShapeDtypeStruct((m, D_MODEL), F32),
        compiler_params=_params(("parallel",), est),
        name="ffn",
    )(h, g_ffn.reshape(1, -1), w_gate, w_up, w_down, g_final.reshape(1, -1))


X_HALVES = X_HEAD_DIM // LANES
X_ROWS = N_X_HEADS * X_HALVES


def _mem_tile_view(a):
    b = a.shape[0]
    a = a.reshape(b, N_MEM, N_X_HEADS, X_HALVES, LANES)
    return jnp.transpose(a, (0, 1, 3, 2, 4)).reshape(b, N_MEM * X_ROWS, LANES)


def _xattn_kernel(q_ref, k_ref, v_ref, o_ref, *, bt):
    def one(bi, carry):
        for h in range(N_X_HEADS):
            kv_rows = [pl.ds(j * N_X_HEADS + h, N_MEM, stride=X_ROWS) for j in range(X_HALVES)]
            s = None
            for j, rows in enumerate(kv_rows):
                sl = slice(h * X_HEAD_DIM + j * LANES, h * X_HEAD_DIM + (j + 1) * LANES)
                q = (q_ref[bi, :, sl] * (X_HEAD_DIM ** -0.5)).astype(BF16)
                part = _dot_nt(q, k_ref[bi, rows, :].astype(BF16))
                s = part if s is None else s + part
            p = jnp.exp(s - jnp.max(s, axis=-1, keepdims=True))
            inv_l = 1.0 / jnp.sum(p, axis=-1, keepdims=True)
            pb = p.astype(BF16)
            for j, rows in enumerate(kv_rows):
                sl = slice(h * X_HEAD_DIM + j * LANES, h * X_HEAD_DIM + (j + 1) * LANES)
                o_ref[bi, :, sl] = _dot(pb, v_ref[bi, rows, :].astype(BF16)) * inv_l
        return carry
    if bt == 1:
        one(0, 0)
    else:
        lax.fori_loop(0, bt, one, 0, unroll=2)


def _xattn(q, mem_k, mem_v, bt, tq):
    b, t, _ = q.shape
    est = 4 * bt * tq * D_MODEL * 4 + 4 * bt * N_MEM * D_MODEL * 4
    kv = lambda: pl.BlockSpec((bt, N_MEM * X_ROWS, LANES), lambda i, j: (i, 0, 0))
    return pl.pallas_call(
        functools.partial(_xattn_kernel, bt=bt),
        grid=(b // bt, t // tq),
        in_specs=[pl.BlockSpec((bt, tq, D_MODEL), lambda i, j: (i, j, 0)), kv(), kv()],
        out_specs=pl.BlockSpec((bt, tq, D_MODEL), lambda i, j: (i, j, 0)),
        out_shape=jax.ShapeDtypeStruct((b, t, D_MODEL), F32),
        compiler_params=_params(("parallel", "parallel"), est),
        name="xattn",
    )(q, mem_k, mem_v)


def _xattn_block_kernel(h_ref, g_ref, wq_ref, k_ref, v_ref, wo_ref, o_ref):
    h1 = h_ref[0]
    q_all = _dot(_rms(h1, g_ref[...]).astype(BF16), wq_ref[...]) * (X_HEAD_DIM ** -0.5)
    o_parts = []
    for h in range(N_X_HEADS):
        kv_rows = [pl.ds(j * N_X_HEADS + h, N_MEM, stride=X_ROWS) for j in range(X_HALVES)]
        lanes = [slice(h * X_HEAD_DIM + j * LANES, h * X_HEAD_DIM + (j + 1) * LANES) for j in range(X_HALVES)]
        s = None
        for rows, sl in zip(kv_rows, lanes):
            part = _dot_nt(q_all[:, sl].astype(BF16), k_ref[0, rows, :].astype(BF16))
            s = part if s is None else s + part
        p = jnp.exp(s - jnp.max(s, axis=-1, keepdims=True))
        inv_l = 1.0 / jnp.sum(p, axis=-1, keepdims=True)
        pb = p.astype(BF16)
        for rows in kv_rows:
            o_parts.append((_dot(pb, v_ref[0, rows, :].astype(BF16)) * inv_l).astype(BF16))
    o_ref[0] = h1 + _dot(jnp.concatenate(o_parts, axis=1), wo_ref[...])


def _xattn_block(h1, g, w_cq, mem_k, mem_v, w_co, tq):
    b, t, _ = h1.shape
    est = 4 * tq * D_MODEL * 4 + 4 * N_MEM * D_MODEL * 4 + 2 * 2 * D_MODEL * D_MODEL * 2 + 4 * tq * D_MODEL * 4
    kv = lambda: pl.BlockSpec((1, N_MEM * X_ROWS, LANES), lambda i, j: (i, 0, 0))
    wspec = lambda: pl.BlockSpec((D_MODEL, D_MODEL), lambda i, j: (0, 0))
    return pl.pallas_call(
        _xattn_block_kernel,
        grid=(b, t // tq),
        in_specs=[pl.BlockSpec((1, tq, D_MODEL), lambda i, j: (i, j, 0)),
                  pl.BlockSpec((1, D_MODEL), lambda i, j: (0, 0)), wspec(), kv(), kv(), wspec()],
        out_specs=pl.BlockSpec((1, tq, D_MODEL), lambda i, j: (i, j, 0)),
        out_shape=jax.ShapeDtypeStruct((b, t, D_MODEL), F32),
        compiler_params=_params(("parallel", "parallel"), est),
        name="xattn_block",
    )(h1, g.reshape(1, -1), w_cq, mem_k, mem_v, w_co)


def _alibi_slopes():
    return np.exp2(-8.0 * (np.arange(N_HEADS, dtype=np.float64) + 1.0) / N_HEADS)


def _prompt_bias_tables():
    full, first = [], []
    for _, dil in DIL_GROUPS:
        jn = N_CLASSES // dil
        rn = BAND // jn
        qj, qm = np.divmod(np.arange(BAND), rn)
        kj, km = np.divmod(np.arange(2 * BAND), 2 * rn)
        for tabs, row_offset in ((full, rn), (first, 0)):
            delta = jn * (row_offset + qm[:, None] - km[None, :]) + (qj[:, None] - kj[None, :])
            valid = (delta >= 0) & (delta <= BAND)
            bias = -_alibi_slopes()[:, None, None] * (delta * dil)[None].astype(np.float64)
            tabs.append(np.where(valid[None], bias, NEG))
    return np.stack(full).astype(np.float32), np.stack(first).astype(np.float32)


UNITS_PER_TRIP = 5


def _largest_divisor(n, cap):
    return max(u for u in range(1, cap + 1) if n % u == 0)


def _attn_prompt_kernel(q_ref, k_ref, v_ref, bfull_ref, bfirst_ref, o_ref, og_ref, lg_ref, *, rows_per_class):
    lane = lax.broadcasted_iota(jnp.int32, (1, LANES), 1)
    first_head = lane < HEAD_DIM

    def unit(g, dil, r_d, blk, full):
        jn = N_CLASSES // dil
        rn = BAND // jn
        classes = [r_d + dil * j for j in range(jn)]
        if full:
            m0 = pl.multiple_of(blk * rn, SUBLANES)
            q_rows = pl.ds(m0, rn)
            k_rows = pl.ds(pl.multiple_of(m0 - rn, SUBLANES), 2 * rn)
        else:
            assert blk == 0
            q_rows = pl.ds(0, rn)
            k_rows = pl.ds(0, min(2 * rn, rows_per_class))
        gather = lambda ref, rows: jnp.concatenate([ref[0, c, rows, :] for c in classes], axis=0)
        q = gather(q_ref, q_rows) * (HEAD_DIM ** -0.5)
        k = gather(k_ref, k_rows).astype(BF16)
        v = gather(v_ref, k_rows).astype(BF16)
        if k.shape[0] < 2 * BAND:
            assert jn == 1
            k = jnp.concatenate([k, k], axis=0)
            v = jnp.concatenate([v, v], axis=0)
        outs, lses = [], []
        for h in range(2):
            qh = jnp.where(first_head if h == 0 else jnp.logical_not(first_head), q, 0.0).astype(BF16)
            s = _dot_nt(qh, k) + (bfull_ref[g, h] if full else bfirst_ref[g, h])
            m = jnp.max(s, axis=-1, keepdims=True)
            p = jnp.exp(s - m)
            l = jnp.sum(p, axis=-1, keepdims=True)
            outs.append(_dot(p.astype(BF16), v) * (1.0 / l))
            lses.append(m + jnp.log(l))
        o = jnp.where(first_head, outs[0], outs[1])
        lse = jnp.broadcast_to(jnp.where(first_head, lses[0], lses[1]), o.shape)
        for j, c in enumerate(classes):
            og_ref[g, c, q_rows, :] = o[j * rn:(j + 1) * rn]
            lg_ref[g, c, q_rows, :] = lse[j * rn:(j + 1) * rn]

    for g, (window, dil) in enumerate(DIL_GROUPS):
        assert window // dil == BAND and N_CLASSES % dil == 0
        nblk = rows_per_class * N_CLASSES // (dil * BAND)
        if dil <= UNITS_PER_TRIP:
            for r in range(dil):
                unit(g, dil, r, 0, False)
        else:
            def body1(r, carry, g=g, dil=dil):
                unit(g, dil, r, 0, False)
                return carry
            lax.fori_loop(0, dil, body1, 0, unroll=_largest_divisor(dil, UNITS_PER_TRIP))
        rest = dil * (nblk - 1)
        if rest:
            def body(j, carry, g=g, dil=dil, nblk=nblk):
                unit(g, dil, j // (nblk - 1), j % (nblk - 1) + 1, True)
                return carry
            lax.fori_loop(0, rest, body, 0, unroll=_largest_divisor(rest, UNITS_PER_TRIP))

    def merge(c, carry):
        l0, l1, l2 = lg_ref[0, c], lg_ref[1, c], lg_ref[2, c]
        mx = jnp.maximum(jnp.maximum(l0, l1), l2)
        e0, e1, e2 = jnp.exp(l0 - mx), jnp.exp(l1 - mx), jnp.exp(l2 - mx)
        merged = (e0 * og_ref[0, c] + e1 * og_ref[1, c] + e2 * og_ref[2, c]) * (1.0 / (e0 + e1 + e2))
        o_ref[0, pl.ds(c, rows_per_class, stride=N_CLASSES), :] = merged
        return carry
    lax.fori_loop(0, N_CLASSES, merge, 0, unroll=2)


def _attn_prompt(q, k, v):
    b, _, rpc, _ = q.shape
    s = rpc * N_CLASSES
    bfull, bfirst = (jnp.asarray(t) for t in _prompt_bias_tables())
    ng = len(DIL_GROUPS)
    blk = lambda: pl.BlockSpec((1, N_CLASSES, rpc, LANES), lambda i, p: (i, 0, 0, p))
    est = 8 * s * LANES * 4 + 6 * s * LANES * 4 + 2 * ng * 2 * BAND * 3 * BAND * 4
    return pl.pallas_call(
        functools.partial(_attn_prompt_kernel, rows_per_class=rpc),
        grid=(b, MIX_W // LANES),
        in_specs=[blk(), blk(), blk(),
                  pl.BlockSpec((ng, 2, BAND, 2 * BAND), lambda i, p: (0, p, 0, 0)),
                  pl.BlockSpec((ng, 2, BAND, 2 * BAND), lambda i, p: (0, p, 0, 0))],
        out_specs=pl.BlockSpec((1, s, LANES), lambda i, p: (i, 0, p)),
        out_shape=jax.ShapeDtypeStruct((b, s, MIX_W), F32),
        scratch_shapes=[pltpu.VMEM((ng, N_CLASSES, rpc, LANES), F32) for _ in range(2)],
        compiler_params=_params(("parallel", "parallel"), est),
        name="attn_prompt",
    )(q, k, v, bfull, bfirst)


def _sample_bias_tables(wb, t_new):
    slopes = _alibi_slopes()

    def table(dist):
        mult = np.zeros(dist.shape)
        for window, dil in DIL_GROUPS:
            mult += (dist >= 0) & (dist <= window) & (dist % dil == 0)
        logm = np.where(mult > 0, np.log(np.maximum(mult, 1.0)), NEG)
        tab = -slopes[:, None, None] * dist[None].astype(np.float64) + logm[None]
        tab = np.where((mult > 0)[None], tab, NEG)
        return tab.reshape(N_HEADS * t_new, dist.shape[1]).astype(np.float32)

    t = np.arange(t_new)[:, None]
    return table(wb + t - np.arange(wb)[None, :]), table(t - np.arange(t_new)[None, :])


def _attn_sample_kernel(q_ref, kn_ref, vn_ref, wk_ref, wv_ref, bo_ref, bn_ref,
                        att_ref, ok_ref, ov_ref, *, wb, t_new):
    rows_q = N_HEADS * t_new
    q = q_ref[0] * (HEAD_DIM ** -0.5)
    qm = jnp.concatenate([q] * N_HEADS, axis=0)
    row_head = lax.broadcasted_iota(jnp.int32, (rows_q, MIX_W), 0) // t_new
    lane_head = lax.broadcasted_iota(jnp.int32, (rows_q, MIX_W), 1) // HEAD_DIM
    own = row_head == lane_head
    qm = jnp.where(own, qm, 0.0).astype(BF16)
    k_new, v_new = kn_ref[0], vn_ref[0]
    wk, wv = wk_ref[0], wv_ref[0]
    s_old = _dot(qm, wk.astype(BF16)) + bo_ref[...]
    s_new = _dot_nt(qm, k_new.astype(BF16)) + bn_ref[...]
    m = jnp.maximum(jnp.max(s_old, axis=-1, keepdims=True), jnp.max(s_new, axis=-1, keepdims=True))
    p_old = jnp.exp(s_old - m)
    p_new = jnp.exp(s_new - m)
    inv_l = 1.0 / (jnp.sum(p_old, axis=-1, keepdims=True) + jnp.sum(p_new, axis=-1, keepdims=True))
    o = _dot_nt(p_old.astype(BF16), wv.astype(BF16)) + _dot(p_new.astype(BF16), v_new.astype(BF16))
    o = jnp.where(own, o * inv_l, 0.0)
    att = o[0:t_new]
    for h in range(1, N_HEADS):
        att = att + o[h * t_new:(h + 1) * t_new]
    att_ref[0] = att
    lane = lax.broadcasted_iota(jnp.int32, (MIX_W, LANES), 1)
    is_new = lane >= LANES - t_new
    pad = jnp.zeros((LANES - t_new, MIX_W), F32)
    for w, new, o_ref in ((wk, k_new, ok_ref), (wv, v_new, ov_ref)):
        rolled = pltpu.roll(w, wb - t_new, axis=1)
        o_ref[0] = rolled
        new_t = jnp.concatenate([pad, new], axis=0).T
        o_ref[0, :, wb - LANES:wb] = jnp.where(is_new, new_t, rolled[:, wb - LANES:wb])


def _attn_sample(q, k_new, v_new, win_k_t, win_v_t):
    b, t_new, _ = q.shape
    wb = win_k_t.shape[2]
    assert wb >= DIL_GROUPS[-1][0] and t_new % SUBLANES == 0 and wb % LANES == 0
    b_old, b_new = _sample_bias_tables(wb, t_new)
    small = lambda: pl.BlockSpec((1, t_new, MIX_W), lambda i: (i, 0, 0))
    big = lambda: pl.BlockSpec((1, MIX_W, wb), lambda i: (i, 0, 0))
    est = 8 * wb * MIX_W * 4 + 2 * wb * MIX_W * 2 + 4 * N_HEADS * t_new * wb * 4
    return pl.pallas_call(
        functools.partial(_attn_sample_kernel, wb=wb, t_new=t_new),
        grid=(b,),
        in_specs=[small(), small(), small(), big(), big(),
                  pl.BlockSpec((N_HEADS * t_new, wb), lambda i: (0, 0)),
                  pl.BlockSpec((N_HEADS * t_new, t_new), lambda i: (0, 0))],
        out_specs=[small(), big(), big()],
        out_shape=[jax.ShapeDtypeStruct((b, t_new, MIX_W), F32),
                   jax.ShapeDtypeStruct((b, MIX_W, wb), F32),
                   jax.ShapeDtypeStruct((b, MIX_W, wb), F32)],
        compiler_params=_params(("parallel",), est),
        name="attn_sample",
    )(q, k_new, v_new, win_k_t, win_v_t, jnp.asarray(b_old), jnp.asarray(b_new))


def _mlstm_kernel(qk_ref, cinit_ref, cw_ref, cb_ref, mv_ref, gcol_ref, mirow_ref, mfrow_ref,
                  bcol_ref, bigrow_ref, bfgrow_ref, c0_ref, n0_ref, m0_ref,
                  h_ref, cst_ref, c_ref, n_ref, m_ref, xp_scr, *, chunk, bt):
    L = chunk

    @pl.when(pl.program_id(1) == 0)
    def _():
        c_ref[...] = c0_ref[...]
        n_ref[...] = n0_ref[...]
        m_ref[...] = m0_ref[...]
        xp_scr[:, 0:SUBLANES, :] = cinit_ref[...]

    rr = lax.broadcasted_iota(jnp.int32, (L, L), 0)
    cc = lax.broadcasted_iota(jnp.int32, (L, L), 1)
    causal = rr >= cc
    tril = causal.astype(F32)
    triu = (rr <= cc).astype(F32)
    hp = lax.Precision.HIGHEST

    def one(bi, carry):
        xp_scr[bi, SUBLANES:SUBLANES + L, :] = qk_ref[bi]
        conv = cb_ref[...]
        for j in range(CONV_W):
            off = SUBLANES - (CONV_W - 1) + j
            conv = conv + xp_scr[bi, off:off + L, :] * cw_ref[j:j + 1, :]
        tail = xp_scr[bi, L:L + SUBLANES, :]
        xp_scr[bi, 0:SUBLANES, :] = tail
        cst_ref[bi] = tail
        qk = conv * _sigmoid(conv)
        v_all = mv_ref[bi]

        pre_c = gcol_ref[bi] + bcol_ref[...]
        bcum_c = jnp.dot(tril, _log_sigmoid(pre_c), precision=hp, preferred_element_type=F32)
        ig_r_all = mirow_ref[bi] + bigrow_ref[...]
        bcum_r = jnp.dot(_log_sigmoid(mfrow_ref[bi] + bfgrow_ref[...]), triu, precision=hp,
                         preferred_element_type=F32)

        c_all, n_all, m_all = c_ref[bi], n_ref[bi], m_ref[bi]
        new_state = []
        for h in range(N_HEADS):
            sl = slice(h * HEAD_DIM, (h + 1) * HEAD_DIM)
            q = qk[:, sl]
            k = qk[:, MIX_W + h * HEAD_DIM:MIX_W + (h + 1) * HEAD_DIM] * (HEAD_DIM ** -0.5)
            v = v_all[:, sl]
            ig_c = pre_c[:, h:h + 1]
            b_c = bcum_c[:, N_HEADS + h:N_HEADS + h + 1]
            b_r = bcum_r[h:h + 1, :]
            ig_r = ig_r_all[h:h + 1, :]
            c_prev = c_all[h]
            n_prev = n_all[h:h + 1, :]
            m_prev = m_all[:, h:h + 1]

            log_d = jnp.where(causal, b_c - b_r + ig_r, NEG)
            log_inter = b_c + m_prev
            m_t = jnp.maximum(log_inter, jnp.max(log_d, axis=-1, keepdims=True))
            d_w = jnp.exp(log_d - m_t)
            inter_w = jnp.exp(log_inter - m_t)
            qb, kb, vb = q.astype(BF16), k.astype(BF16), v.astype(BF16)
            a = _dot_nt(qb, kb) * d_w
            num = _dot(a.astype(BF16), vb) + inter_w * _dot(qb, c_prev.astype(BF16))
            den = jnp.sum(a, axis=-1, keepdims=True) + inter_w * jnp.sum(q * n_prev, axis=-1, keepdims=True)
            h_ref[bi, :, sl] = num * (1.0 / jnp.maximum(jnp.abs(den), jnp.exp(-m_t)))

            m_new = m_t[L - 1:L, :]
            b_last = b_c[L - 1:L, :]
            w_s = jnp.exp(b_last - b_c + ig_c - m_new)
            decay = jnp.exp(b_last + m_prev - m_new)
            kw = k * w_s
            new_state.append((decay * c_prev + _dot_tn(kw.astype(BF16), vb),
                              decay * n_prev + jnp.sum(kw, axis=0, keepdims=True), m_new))
        for h, (c_new, n_new, m_new) in enumerate(new_state):
            c_ref[bi, h] = c_new
            n_ref[bi, h:h + 1, :] = n_new
            m_ref[bi, :, h:h + 1] = m_new
        return carry

    if bt == 1:
        one(0, 0)
    else:
        lax.fori_loop(0, bt, one, 0, unroll=2)


def _mlstm(qk_pre, conv_init, conv_w, conv_b, mv, gates, b_ig, b_fg, c0, n0, m0, chunk, bt):
    b, s, _ = qk_pre.shape
    assert s % chunk == 0 and b % bt == 0 and chunk % SUBLANES == 0
    cinit = jnp.pad(conv_init, ((0, 0), (SUBLANES - (CONV_W - 1), 0), (0, 0)))
    cw = jnp.pad(conv_w, ((0, SUBLANES - CONV_W), (0, 0)))
    g_rows = jnp.swapaxes(gates[:, :, :2 * N_HEADS], 1, 2)
    mi_row, mf_row = g_rows[:, :N_HEADS], g_rows[:, N_HEADS:]
    bcol = jnp.concatenate([b_ig, b_fg, jnp.zeros((GATE_PAD - 2 * N_HEADS,), F32)]).reshape(1, GATE_PAD)
    nc = s // chunk
    seq3 = lambda w: pl.BlockSpec((bt, chunk, w), lambda i, c: (i, c, 0))
    fix3 = lambda r, w: pl.BlockSpec((bt, r, w), lambda i, c: (i, 0, 0))
    row3 = lambda: pl.BlockSpec((bt, N_HEADS, chunk), lambda i, c: (i, 0, c))
    const = lambda r, w: pl.BlockSpec((r, w), lambda i, c: (0, 0))
    cspec = lambda: pl.BlockSpec((bt, N_HEADS, HEAD_DIM, HEAD_DIM), lambda i, c: (i, 0, 0, 0))
    est = (2 * bt * chunk * (2 * MIX_W + MIX_W + GATE_PAD + MIX_W) * 4 + bt * (chunk + SUBLANES) * 2 * MIX_W * 4
           + 6 * bt * N_HEADS * HEAD_DIM * HEAD_DIM * 4 + 8 * chunk * chunk * 4 + 6 * chunk * 2 * MIX_W * 4)
    outs = pl.pallas_call(
        functools.partial(_mlstm_kernel, chunk=chunk, bt=bt),
        grid=(b // bt, nc),
        in_specs=[seq3(2 * MIX_W), fix3(SUBLANES, 2 * MIX_W), const(SUBLANES, 2 * MIX_W), const(1, 2 * MIX_W),
                  seq3(MIX_W), seq3(GATE_PAD), row3(), row3(),
                  const(1, GATE_PAD), const(N_HEADS, 1), const(N_HEADS, 1),
                  cspec(), fix3(N_HEADS, HEAD_DIM), fix3(1, N_HEADS)],
        out_specs=[seq3(MIX_W), fix3(SUBLANES, 2 * MIX_W), cspec(), fix3(N_HEADS, HEAD_DIM), fix3(1, N_HEADS)],
        out_shape=[jax.ShapeDtypeStruct((b, s, MIX_W), F32),
                   jax.ShapeDtypeStruct((b, SUBLANES, 2 * MIX_W), F32),
                   jax.ShapeDtypeStruct((b, N_HEADS, HEAD_DIM, HEAD_DIM), F32),
                   jax.ShapeDtypeStruct((b, N_HEADS, HEAD_DIM), F32),
                   jax.ShapeDtypeStruct((b, 1, N_HEADS), F32)],
        scratch_shapes=[pltpu.VMEM((bt, chunk + SUBLANES, 2 * MIX_W), F32)],
        compiler_params=_params(("arbitrary", "arbitrary"), est),
        name="mlstm",
    )(qk_pre, cinit, cw, conv_b.reshape(1, -1), mv, gates, mi_row, mf_row,
      bcol, b_ig.reshape(-1, 1), b_fg.reshape(-1, 1), c0, n0, m0.reshape(b, 1, N_HEADS))
    h, cst, c, n, m = outs
    return h, cst[:, SUBLANES - (CONV_W - 1):], c, n, m.reshape(b, N_HEADS)


IN_WIDTHS = (MIX_W, MIX_W, MIX_W, 2 * MIX_W, MIX_W, MIX_W, GATE_PAD)


def _mixers_to_output(x2, att, mlh, mo, mem_k, mem_v, wts, bt_x, tq_x, tm):
    b = mem_k.shape[0]
    mk, mv_ = _mem_tile_view(mem_k), _mem_tile_view(mem_v)
    h1 = _merge(att, mlh, mo, wts["b_og"], wts["g_attn_out"], wts["g_mlstm_out"], wts["w_out"], x2, tm)
    if tq_x >= 2 * LANES:
        h2 = _xattn_block(h1.reshape(b, -1, D_MODEL), wts["g_cross"], wts["w_cq"], mk, mv_, wts["w_co"],
                          tq_x).reshape(-1, D_MODEL)
    else:
        (xq,) = _norm_matmul(h1, wts["g_cross"], wts["w_cq"], (D_MODEL,), tm)
        xo = _xattn(xq.reshape(b, -1, D_MODEL), mk, mv_, bt_x, tq_x)
        h2 = _matmul_res(xo.reshape(-1, D_MODEL), wts["w_co"], h1, tm)
    return _ffn(h2, wts["g_ffn"], wts["w_gate"], wts["w_up"], wts["w_down"], wts["g_final"], tm)


def kernel(x_prompt, x_sample, mem_prompt, cache_win_k, cache_win_v, cache_mem_k, cache_mem_v, state_conv, state_mlstm_C, state_mlstm_n, state_mlstm_m, g_mix, w_in, conv_w, conv_b, b_ig, b_fg, b_og, g_attn_out, g_mlstm_out, w_out, g_cross, g_mem, w_cq, w_ck, w_cv, w_co, g_ffn, w_gate, w_up, w_down, g_final):
    bp, sp, _ = x_prompt.shape
    bs, ts, _ = x_sample.shape
    wts = dict(b_og=b_og, g_attn_out=g_attn_out, g_mlstm_out=g_mlstm_out, g_cross=g_cross, g_ffn=g_ffn,
               g_final=g_final, w_out=w_out.astype(BF16), w_cq=w_cq.astype(BF16), w_co=w_co.astype(BF16),
               w_gate=w_gate.astype(BF16), w_up=w_up.astype(BF16), w_down=w_down.astype(BF16))
    w_in_p = jnp.pad(w_in, ((0, 0), (0, sum(IN_WIDTHS) - N_IN))).astype(BF16)
    w_ckv = jnp.concatenate([w_ck, w_cv], axis=1).astype(BF16)
    tm = 512

    xp2 = x_prompt.reshape(bp * sp, D_MODEL)
    qk_pre, mv, mo, gates, aq_c, ak_c, av_c, ak_t, av_t = _norm_matmul(
        xp2, g_mix, w_in_p, IN_WIDTHS, tm, transposed=(1, 2), class_major=(0, 1, 2), seq=sp)
    r3 = lambda a: a.reshape(bp, sp, -1)
    att = _attn_prompt(aq_c, ak_c, av_c)
    keep = min(DIL_GROUPS[-1][0], sp)
    from_t = lambda a: jnp.transpose(a.reshape(a.shape[0], N_HEADS, HEAD_DIM, a.shape[2]), (0, 3, 1, 2))
    win_k_p = from_t(ak_t)[:, sp - keep:]
    win_v_p = from_t(av_t)[:, sp - keep:]
    mlh, conv_p, c_p, n_p, m_p = _mlstm(
        r3(qk_pre), jnp.zeros((bp, CONV_W - 1, 2 * MIX_W), F32), conv_w, conv_b, r3(mv), r3(gates), b_ig, b_fg,
        jnp.zeros((bp, N_HEADS, HEAD_DIM, HEAD_DIM), F32), jnp.zeros((bp, N_HEADS, HEAD_DIM), F32),
        jnp.full((bp, N_HEADS), NEG, F32), chunk=256, bt=1)
    mem_k_p, mem_v_p = _norm_matmul(mem_prompt.reshape(bp * N_MEM, D_MODEL), g_mem, w_ckv, (D_MODEL, D_MODEL), tm)
    mem_k_p = mem_k_p.reshape(bp, N_MEM, N_X_HEADS, X_HEAD_DIM)
    mem_v_p = mem_v_p.reshape(bp, N_MEM, N_X_HEADS, X_HEAD_DIM)
    y_p = _mixers_to_output(xp2, att.reshape(-1, MIX_W), mlh.reshape(-1, MIX_W), mo, mem_k_p, mem_v_p, wts,
                            bt_x=1, tq_x=512, tm=tm)

    xs2 = x_sample.reshape(bs * ts, D_MODEL)
    aq, ak, av, qk_pre, mv, mo, gates = _norm_matmul(xs2, g_mix, w_in_p, IN_WIDTHS, tm)
    r3 = lambda a: a.reshape(bs, ts, -1)
    wb = cache_win_k.shape[1]
    to_t = lambda a: jnp.transpose(a, (0, 2, 3, 1)).reshape(a.shape[0], MIX_W, a.shape[1])
    att, win_k_s, win_v_s = _attn_sample(r3(aq), r3(ak), r3(av), to_t(cache_win_k), to_t(cache_win_v))
    mlh, conv_s, c_s, n_s, m_s = _mlstm(r3(qk_pre), state_conv, conv_w, conv_b, r3(mv), r3(gates), b_ig, b_fg,
                                        state_mlstm_C, state_mlstm_n, state_mlstm_m, chunk=ts, bt=8)
    y_s = _mixers_to_output(xs2, att.reshape(-1, MIX_W), mlh.reshape(-1, MIX_W), mo, cache_mem_k, cache_mem_v, wts,
                            bt_x=4, tq_x=ts, tm=tm)

    return (y_p.reshape(bp, sp, D_MODEL), y_s.reshape(bs, ts, D_MODEL),
            win_k_p, win_v_p, conv_p, c_p, n_p, m_p, mem_k_p, mem_v_p,
            from_t(win_k_s), from_t(win_v_s), conv_s, c_s, n_s, m_s)
```

```python
import functools

import numpy as np
import jax
import jax.numpy as jnp
from jax import lax
from jax.experimental import pallas as pl
from jax.experimental.pallas import tpu as pltpu

F32 = jnp.float32
BF16 = jnp.bfloat16

D_MODEL = 1024
HEAD_DIM = 64
N_HEADS = 8
MIX_W = N_HEADS * HEAD_DIM
DIL_GROUPS = ((128, 1), (512, 4), (2048, 16))
BAND = 128
CONV_W = 4
N_MEM = 256
N_X_HEADS = 4
X_HEAD_DIM = D_MODEL // N_X_HEADS
D_FF = 2816
RMS_EPS = 1e-6
NEG = -1e30
N_IN = 3 * MIX_W + 2 * MIX_W + MIX_W + MIX_W + 2 * N_HEADS
GATE_PAD = 128

LANES = 128
SUBLANES = 8
VMEM_BYTES_V7X = 64 * 1024 * 1024


def _vmem_limit(nbytes):
    return int(min(max(2 * nbytes, 16 * 1024 * 1024), VMEM_BYTES_V7X - 8 * 1024 * 1024))


def _params(semantics, vmem_estimate):
    return pltpu.CompilerParams(dimension_semantics=semantics,
                                vmem_limit_bytes=_vmem_limit(vmem_estimate))


def _rms(x, g):
    return x * lax.rsqrt(jnp.mean(x * x, axis=-1, keepdims=True) + RMS_EPS) * g


def _sigmoid(x):
    return 1.0 / (1.0 + jnp.exp(-x))


def _log_sigmoid(x):
    return jnp.minimum(x, 0.0) - jnp.log(1.0 + jnp.exp(-jnp.abs(x)))


def _dot(a, b):
    return jnp.dot(a, b, preferred_element_type=F32)


def _dot_nt(a, b):
    return lax.dot_general(a, b, (((1,), (1,)), ((), ())), preferred_element_type=F32)


def _dot_tn(a, b):
    return lax.dot_general(a, b, (((0,), (0,)), ((), ())), preferred_element_type=F32)


N_CLASSES = max(d for _, d in DIL_GROUPS)
IN_WIDTHS = (MIX_W, MIX_W, MIX_W, 2 * MIX_W, MIX_W, MIX_W, GATE_PAD)
IN_OFFS = tuple(int(o) for o in np.cumsum((0,) + IN_WIDTHS))


def _norm_matmul_kernel(x_ref, g_ref, w_ref, *o_refs, splits):
    xn = _rms(x_ref[...], g_ref[...]).astype(BF16)
    for o_ref, (a, b) in zip(o_refs, splits):
        o_ref[...] = _dot(xn, w_ref[:, a:b])


def _norm_matmul(x, g, w, widths, tm):
    m, k = x.shape
    n = w.shape[1]
    assert sum(widths) == n and m % tm == 0
    offs = np.cumsum([0] + list(widths))
    splits = tuple((int(offs[i]), int(offs[i + 1])) for i in range(len(widths)))
    est = 2 * tm * k * 4 + 2 * k * n * 2 + 2 * tm * n * 4
    return pl.pallas_call(
        functools.partial(_norm_matmul_kernel, splits=splits),
        grid=(m // tm,),
        in_specs=[pl.BlockSpec((tm, k), lambda i: (i, 0)),
                  pl.BlockSpec((1, k), lambda i: (0, 0)),
                  pl.BlockSpec((k, n), lambda i: (0, 0))],
        out_specs=[pl.BlockSpec((tm, wd), lambda i: (i, 0)) for wd in widths],
        out_shape=[jax.ShapeDtypeStruct((m, wd), F32) for wd in widths],
        compiler_params=_params(("parallel",), est),
        name="norm_matmul",
    )(x, g.reshape(1, k), w)


def _inproj_prompt_kernel(x_ref, g_ref, w_ref, wvt_ref, cw_ref, cb_ref,
                          qc_ref, kc_ref, vc_ref, kt_ref, vt_ref, mq_ref, mk_ref, mvt_ref, mo_ref, gate_ref, tail_ref,
                          cls_scr, xp_scr, *, per_seq):
    tm = x_ref.shape[0]

    @pl.when(pl.program_id(0) % per_seq == 0)
    def _():
        xp_scr[0:SUBLANES, :] = jnp.zeros((SUBLANES, 2 * MIX_W), F32)

    xn = _rms(x_ref[...], g_ref[...]).astype(BF16)
    proj = lambda idx: _dot(xn, w_ref[:, IN_OFFS[idx]:IN_OFFS[idx + 1]])

    xp_scr[SUBLANES:SUBLANES + tm, :] = proj(3)
    conv = cb_ref[...]
    for j in range(CONV_W):
        off = SUBLANES - (CONV_W - 1) + j
        conv = conv + xp_scr[off:off + tm, :] * cw_ref[j:j + 1, :]
    tail = xp_scr[tm:tm + SUBLANES, :]
    xp_scr[0:SUBLANES, :] = tail
    tail_ref[0] = tail
    act = conv * _sigmoid(conv)
    mq_ref[0] = act[:, :MIX_W].T
    mk_ref[...] = act[:, MIX_W:] * (HEAD_DIM ** -0.5)

    for idx, c_ref, t_ref in ((0, qc_ref, None), (1, kc_ref, kt_ref), (2, vc_ref, vt_ref)):
        y = proj(idx)
        if t_ref is not None:
            t_ref[0] = y.T
        scr = cls_scr.at[idx]
        for lt in range(MIX_W // LANES):
            scr[lt] = y[:, lt * LANES:(lt + 1) * LANES]
        for c in range(N_CLASSES):
            for lt in range(MIX_W // LANES):
                c_ref[0, c, :, lt * LANES:(lt + 1) * LANES] = scr[lt, pl.ds(c, tm // N_CLASSES, stride=N_CLASSES), :]

    mvt_ref[0] = _dot_nt(wvt_ref[...], xn)
    mo_ref[...] = proj(5)
    gate_ref[...] = proj(6)


def _inproj_prompt(x, g, w, conv_w, conv_b, tm, seq):
    m, k = x.shape
    n = w.shape[1]
    assert n == IN_OFFS[-1] and m % seq == 0 and seq % tm == 0 and tm % (N_CLASSES * SUBLANES) == 0
    b, per_seq = m // seq, seq // tm
    est = (2 * tm * k * 4 + 2 * k * n * 2 + 2 * tm * (n + 4 * MIX_W) * 4 + 3 * tm * MIX_W * 4
           + 3 * (tm + SUBLANES) * 2 * MIX_W * 4)
    cls_spec = lambda: pl.BlockSpec((1, N_CLASSES, tm // N_CLASSES, MIX_W), lambda i: (i // per_seq, 0, i % per_seq, 0))
    cls_shape = jax.ShapeDtypeStruct((b, N_CLASSES, seq // N_CLASSES, MIX_W), F32)
    t_spec = lambda: pl.BlockSpec((1, MIX_W, tm), lambda i: (i // per_seq, 0, i % per_seq))
    t_shape = jax.ShapeDtypeStruct((b, MIX_W, seq), F32)
    nat_spec = lambda wd: pl.BlockSpec((tm, wd), lambda i: (i, 0))
    nat_shape = lambda wd: jax.ShapeDtypeStruct((m, wd), F32)
    cw = jnp.pad(conv_w, ((0, SUBLANES - CONV_W), (0, 0)))
    return pl.pallas_call(
        functools.partial(_inproj_prompt_kernel, per_seq=per_seq),
        grid=(m // tm,),
        in_specs=[pl.BlockSpec((tm, k), lambda i: (i, 0)),
                  pl.BlockSpec((1, k), lambda i: (0, 0)),
                  pl.BlockSpec((k, n), lambda i: (0, 0)),
                  pl.BlockSpec((MIX_W, k), lambda i: (0, 0)),
                  pl.BlockSpec((SUBLANES, 2 * MIX_W), lambda i: (0, 0)),
                  pl.BlockSpec((1, 2 * MIX_W), lambda i: (0, 0))],
        out_specs=[cls_spec(), cls_spec(), cls_spec(), t_spec(), t_spec(), t_spec(), nat_spec(MIX_W), t_spec(),
                   nat_spec(MIX_W), nat_spec(GATE_PAD),
                   pl.BlockSpec((1, SUBLANES, 2 * MIX_W), lambda i: (i // per_seq, 0, 0))],
        out_shape=[cls_shape, cls_shape, cls_shape, t_shape, t_shape, t_shape, nat_shape(MIX_W), t_shape,
                   nat_shape(MIX_W), nat_shape(GATE_PAD),
                   jax.ShapeDtypeStruct((b, SUBLANES, 2 * MIX_W), F32)],
        scratch_shapes=[pltpu.VMEM((3, MIX_W // LANES, tm, LANES), F32),
                        pltpu.VMEM((tm + SUBLANES, 2 * MIX_W), F32)],
        compiler_params=_params(("arbitrary",), est),
        name="inproj_prompt",
    )(x, g.reshape(1, k), w, w[:, IN_OFFS[4]:IN_OFFS[5]].T, cw, conv_b.reshape(1, -1))


def _matmul_res_kernel(x_ref, w_ref, r_ref, o_ref):
    o_ref[...] = r_ref[...] + _dot(x_ref[...].astype(BF16), w_ref[...])


def _matmul_res(x, w, res, tm):
    m, k = x.shape
    n = w.shape[1]
    est = 2 * tm * k * 4 + 2 * k * n * 2 + 4 * tm * n * 4
    return pl.pallas_call(
        _matmul_res_kernel,
        grid=(m // tm,),
        in_specs=[pl.BlockSpec((tm, k), lambda i: (i, 0)),
                  pl.BlockSpec((k, n), lambda i: (0, 0)),
                  pl.BlockSpec((tm, n), lambda i: (i, 0))],
        out_specs=pl.BlockSpec((tm, n), lambda i: (i, 0)),
        out_shape=jax.ShapeDtypeStruct((m, n), F32),
        compiler_params=_params(("parallel",), est),
        name="matmul_res",
    )(x, w, res)


def _merge_kernel(att_ref, mlh_ref, mo_ref, bog_ref, ga_ref, gm_ref, w_ref, x_ref, o_ref):
    a_n = _rms(att_ref[...], ga_ref[...]).astype(BF16)
    ml = _sigmoid(mo_ref[...] + bog_ref[...]) * mlh_ref[...]
    m_n = _rms(ml, gm_ref[...]).astype(BF16)
    o_ref[...] = x_ref[...] + _dot(a_n, w_ref[0:MIX_W, :]) + _dot(m_n, w_ref[MIX_W:2 * MIX_W, :])


def _merge(att, mlh, mo, b_og, g_a, g_m, w_out, x, tm):
    m = x.shape[0]
    half = lambda: pl.BlockSpec((tm, MIX_W), lambda i: (i, 0))
    vec = lambda: pl.BlockSpec((1, MIX_W), lambda i: (0, 0))
    est = 6 * tm * MIX_W * 4 + 2 * 2 * MIX_W * D_MODEL * 2 + 4 * tm * D_MODEL * 4
    return pl.pallas_call(
        _merge_kernel,
        grid=(m // tm,),
        in_specs=[half(), half(), half(), vec(), vec(), vec(),
                  pl.BlockSpec((2 * MIX_W, D_MODEL), lambda i: (0, 0)),
                  pl.BlockSpec((tm, D_MODEL), lambda i: (i, 0))],
        out_specs=pl.BlockSpec((tm, D_MODEL), lambda i: (i, 0)),
        out_shape=jax.ShapeDtypeStruct((m, D_MODEL), F32),
        compiler_params=_params(("parallel",), est),
        name="merge",
    )(att, mlh, mo, b_og.reshape(1, -1), g_a.reshape(1, -1), g_m.reshape(1, -1), w_out, x)


FF_CHUNK = D_FF // 2


def _ffn_kernel(h_ref, gf_ref, wg_ref, wu_ref, wd_ref, gl_ref, o_ref):
    h = h_ref[...]
    hn = _rms(h, gf_ref[...]).astype(BF16)
    acc = h
    for c in range(D_FF // FF_CHUNK):
        sl = slice(c * FF_CHUNK, (c + 1) * FF_CHUNK)
        gate = _dot(hn, wg_ref[:, sl])
        up = _dot(hn, wu_ref[:, sl])
        act = (gate * _sigmoid(gate) * up).astype(BF16)
        acc = acc + _dot(act, wd_ref[sl, :])
    o_ref[...] = _rms(acc, gl_ref[...])


def _ffn(h, g_ffn, w_gate, w_up, w_down, g_final, tm):
    m = h.shape[0]
    est = 4 * tm * D_MODEL * 4 + 2 * 3 * D_MODEL * D_FF * 2 + 3 * tm * FF_CHUNK * 4
    return pl.pallas_call(
        _ffn_kernel,
        grid=(m // tm,),
        in_specs=[pl.BlockSpec((tm, D_MODEL), lambda i: (i, 0)),
                  pl.BlockSpec((1, D_MODEL), lambda i: (0, 0)),
                  pl.BlockSpec((D_MODEL, D_FF), lambda i: (0, 0)),
                  pl.BlockSpec((D_MODEL, D_FF), lambda i: (0, 0)),
                  pl.BlockSpec((D_FF, D_MODEL), lambda i: (0, 0)),
                  pl.BlockSpec((1, D_MODEL), lambda i: (0, 0))],
        out_specs=pl.BlockSpec((tm, D_MODEL), lambda i: (i, 0)),
        out_shape=jax.ShapeDtypeStruct((m, D_MODEL), F32),
        compiler_params=_params(("parallel",), est),
        name="ffn",
    )(h, g_ffn.reshape(1, -1), w_gate, w_up, w_down, g_final.reshape(1, -1))


X_HALVES = X_HEAD_DIM // LANES
X_ROWS = N_X_HEADS * X_HALVES


def _mem_tile_view(a):
    b = a.shape[0]
    a = a.reshape(b, N_MEM, N_X_HEADS, X_HALVES, LANES)
    return jnp.transpose(a, (0, 1, 3, 2, 4)).reshape(b, N_MEM * X_ROWS, LANES)


def _mem_head(ref, bi, h):
    halves = [ref[bi, pl.ds(j * N_X_HEADS + h, N_MEM, stride=X_ROWS), :] for j in range(X_HALVES)]
    return jnp.concatenate(halves, axis=1).astype(BF16)


def _xattn_kernel(q_ref, k_ref, v_ref, o_ref, *, bt):
    for bi in range(bt):
        for h in range(N_X_HEADS):
            sl = slice(h * X_HEAD_DIM, (h + 1) * X_HEAD_DIM)
            q = (q_ref[bi, :, sl] * (X_HEAD_DIM ** -0.5)).astype(BF16)
            s = _dot_nt(q, _mem_head(k_ref, bi, h))
            p = jnp.exp(s - jnp.max(s, axis=-1, keepdims=True))
            inv_l = 1.0 / jnp.sum(p, axis=-1, keepdims=True)
            o_ref[bi, :, sl] = _dot(p.astype(BF16), _mem_head(v_ref, bi, h)) * inv_l


def _xattn(q, mem_k, mem_v, bt, tq):
    b, t, _ = q.shape
    est = 4 * bt * tq * D_MODEL * 4 + 4 * bt * N_MEM * D_MODEL * 4
    kv = lambda: pl.BlockSpec((bt, N_MEM * X_ROWS, LANES), lambda i, j: (i, 0, 0))
    return pl.pallas_call(
        functools.partial(_xattn_kernel, bt=bt),
        grid=(b // bt, t // tq),
        in_specs=[pl.BlockSpec((bt, tq, D_MODEL), lambda i, j: (i, j, 0)), kv(), kv()],
        out_specs=pl.BlockSpec((bt, tq, D_MODEL), lambda i, j: (i, j, 0)),
        out_shape=jax.ShapeDtypeStruct((b, t, D_MODEL), F32),
        compiler_params=_params(("parallel", "parallel"), est),
        name="xattn",
    )(q, mem_k, mem_v)


def _xattn_block_kernel(h_ref, g_ref, wq_ref, k_ref, v_ref, wo_ref, o_ref):
    h1 = h_ref[0]
    q_all = _dot(_rms(h1, g_ref[...]).astype(BF16), wq_ref[...]) * (X_HEAD_DIM ** -0.5)
    o_parts = []
    for h in range(N_X_HEADS):
        sl = slice(h * X_HEAD_DIM, (h + 1) * X_HEAD_DIM)
        s = _dot_nt(q_all[:, sl].astype(BF16), _mem_head(k_ref, 0, h))
        p = jnp.exp(s - jnp.max(s, axis=-1, keepdims=True))
        inv_l = 1.0 / jnp.sum(p, axis=-1, keepdims=True)
        o_parts.append((_dot(p.astype(BF16), _mem_head(v_ref, 0, h)) * inv_l).astype(BF16))
    o_ref[0] = h1 + _dot(jnp.concatenate(o_parts, axis=1), wo_ref[...])


def _xattn_block(h1, g, w_cq, mem_k, mem_v, w_co, tq):
    b, t, _ = h1.shape
    est = 4 * tq * D_MODEL * 4 + 4 * N_MEM * D_MODEL * 4 + 2 * 2 * D_MODEL * D_MODEL * 2 + 4 * tq * D_MODEL * 4
    kv = lambda: pl.BlockSpec((1, N_MEM * X_ROWS, LANES), lambda i, j: (i, 0, 0))
    wspec = lambda: pl.BlockSpec((D_MODEL, D_MODEL), lambda i, j: (0, 0))
    return pl.pallas_call(
        _xattn_block_kernel,
        grid=(b, t // tq),
        in_specs=[pl.BlockSpec((1, tq, D_MODEL), lambda i, j: (i, j, 0)),
                  pl.BlockSpec((1, D_MODEL), lambda i, j: (0, 0)), wspec(), kv(), kv(), wspec()],
        out_specs=pl.BlockSpec((1, tq, D_MODEL), lambda i, j: (i, j, 0)),
        out_shape=jax.ShapeDtypeStruct((b, t, D_MODEL), F32),
        compiler_params=_params(("parallel", "parallel"), est),
        name="xattn_block",
    )(h1, g.reshape(1, -1), w_cq, mem_k, mem_v, w_co)


def _alibi_slopes():
    return np.exp2(-8.0 * (np.arange(N_HEADS, dtype=np.float64) + 1.0) / N_HEADS)


def _prompt_bias_tables():
    full, first = [], []
    for _, dil in DIL_GROUPS:
        jn = N_CLASSES // dil
        rn = BAND // jn
        qj, qm = np.divmod(np.arange(BAND), rn)
        kj, km = np.divmod(np.arange(2 * BAND), 2 * rn)
        for tabs, row_offset in ((full, rn), (first, 0)):
            delta = jn * (row_offset + qm[:, None] - km[None, :]) + (qj[:, None] - kj[None, :])
            valid = (delta >= 0) & (delta <= BAND)
            bias = -_alibi_slopes()[:, None, None] * (delta * dil)[None].astype(np.float64)
            tabs.append(np.where(valid[None], bias, NEG))
    return np.stack(full).astype(np.float32), np.stack(first).astype(np.float32)


UNITS_PER_TRIP = 5


def _largest_divisor(n, cap):
    return max(u for u in range(1, cap + 1) if n % u == 0)


def _attn_prompt_kernel(q_ref, k_ref, v_ref, bfull_ref, bfirst_ref, o_ref, og_ref, lg_ref, *, rows_per_class):
    lane = lax.broadcasted_iota(jnp.int32, (1, LANES), 1)
    first_head = lane < HEAD_DIM

    def unit(g, dil, r_d, blk, full):
        jn = N_CLASSES // dil
        rn = BAND // jn
        classes = [r_d + dil * j for j in range(jn)]
        if full:
            m0 = pl.multiple_of(blk * rn, SUBLANES)
            q_rows = pl.ds(m0, rn)
            k_rows = pl.ds(pl.multiple_of(m0 - rn, SUBLANES), 2 * rn)
        else:
            assert blk == 0
            q_rows = pl.ds(0, rn)
            k_rows = pl.ds(0, min(2 * rn, rows_per_class))
        gather = lambda ref, rows: jnp.concatenate([ref[0, c, rows, :] for c in classes], axis=0)
        q = gather(q_ref, q_rows) * (HEAD_DIM ** -0.5)
        k = gather(k_ref, k_rows).astype(BF16)
        v = gather(v_ref, k_rows).astype(BF16)
        if k.shape[0] < 2 * BAND:
            assert jn == 1
            k = jnp.concatenate([k, k], axis=0)
            v = jnp.concatenate([v, v], axis=0)
        outs, lses = [], []
        for h in range(2):
            qh = jnp.where(first_head if h == 0 else jnp.logical_not(first_head), q, 0.0).astype(BF16)
            s = _dot_nt(qh, k) + (bfull_ref[g, h] if full else bfirst_ref[g, h])
            m = jnp.max(s, axis=-1, keepdims=True)
            p = jnp.exp(s - m)
            l = jnp.sum(p, axis=-1, keepdims=True)
            outs.append(_dot(p.astype(BF16), v) * (1.0 / l))
            lses.append(m + jnp.log(l))
        o = jnp.where(first_head, outs[0], outs[1])
        lse = jnp.broadcast_to(jnp.where(first_head, lses[0], lses[1]), o.shape)
        for j, c in enumerate(classes):
            og_ref[g, c, q_rows, :] = o[j * rn:(j + 1) * rn]
            lg_ref[g, c, q_rows, :] = lse[j * rn:(j + 1) * rn]

    for g, (window, dil) in enumerate(DIL_GROUPS):
        assert window // dil == BAND and N_CLASSES % dil == 0
        nblk = rows_per_class * N_CLASSES // (dil * BAND)
        if dil <= UNITS_PER_TRIP:
            for r in range(dil):
                unit(g, dil, r, 0, False)
        else:
            def body1(r, carry, g=g, dil=dil):
                unit(g, dil, r, 0, False)
                return carry
            lax.fori_loop(0, dil, body1, 0, unroll=_largest_divisor(dil, UNITS_PER_TRIP))
        rest = dil * (nblk - 1)
        if rest:
            def body(j, carry, g=g, dil=dil, nblk=nblk):
                unit(g, dil, j // (nblk - 1), j % (nblk - 1) + 1, True)
                return carry
            lax.fori_loop(0, rest, body, 0, unroll=_largest_divisor(rest, UNITS_PER_TRIP))

    def merge(c, carry):
        l0, l1, l2 = lg_ref[0, c], lg_ref[1, c], lg_ref[2, c]
        mx = jnp.maximum(jnp.maximum(l0, l1), l2)
        e0, e1, e2 = jnp.exp(l0 - mx), jnp.exp(l1 - mx), jnp.exp(l2 - mx)
        merged = (e0 * og_ref[0, c] + e1 * og_ref[1, c] + e2 * og_ref[2, c]) * (1.0 / (e0 + e1 + e2))
        o_ref[0, pl.ds(c, rows_per_class, stride=N_CLASSES), :] = merged
        return carry
    lax.fori_loop(0, N_CLASSES, merge, 0, unroll=2)


def _attn_prompt(q, k, v):
    b, _, rpc, _ = q.shape
    s = rpc * N_CLASSES
    bfull, bfirst = (jnp.asarray(t) for t in _prompt_bias_tables())
    ng = len(DIL_GROUPS)
    blk = lambda: pl.BlockSpec((1, N_CLASSES, rpc, LANES), lambda i, p: (i, 0, 0, p))
    est = 8 * s * LANES * 4 + 6 * s * LANES * 4 + 2 * ng * 2 * BAND * 3 * BAND * 4
    return pl.pallas_call(
        functools.partial(_attn_prompt_kernel, rows_per_class=rpc),
        grid=(b, MIX_W // LANES),
        in_specs=[blk(), blk(), blk(),
                  pl.BlockSpec((ng, 2, BAND, 2 * BAND), lambda i, p: (0, p, 0, 0)),
                  pl.BlockSpec((ng, 2, BAND, 2 * BAND), lambda i, p: (0, p, 0, 0))],
        out_specs=pl.BlockSpec((1, s, LANES), lambda i, p: (i, 0, p)),
        out_shape=jax.ShapeDtypeStruct((b, s, MIX_W), F32),
        scratch_shapes=[pltpu.VMEM((ng, N_CLASSES, rpc, LANES), F32) for _ in range(2)],
        compiler_params=_params(("parallel", "parallel"), est),
        name="attn_prompt",
    )(q, k, v, bfull, bfirst)


def _sample_bias_tables(wb, t_new):
    slopes = _alibi_slopes()

    def table(dist):
        mult = np.zeros(dist.shape)
        for window, dil in DIL_GROUPS:
            mult += (dist >= 0) & (dist <= window) & (dist % dil == 0)
        logm = np.where(mult > 0, np.log(np.maximum(mult, 1.0)), NEG)
        tab = -slopes[:, None, None] * dist[None].astype(np.float64) + logm[None]
        tab = np.where((mult > 0)[None], tab, NEG)
        return tab.reshape(N_HEADS * t_new, dist.shape[1]).astype(np.float32)

    t = np.arange(t_new)[:, None]
    return table(wb + t - np.arange(wb)[None, :]), table(t - np.arange(t_new)[None, :])


def _attn_sample_kernel(q_ref, kn_ref, vn_ref, wk_ref, wv_ref, bo_ref, bn_ref,
                        att_ref, ok_ref, ov_ref, *, wb, t_new):
    rows_q = N_HEADS * t_new
    q = q_ref[0] * (HEAD_DIM ** -0.5)
    qm = jnp.concatenate([q] * N_HEADS, axis=0)
    row_head = lax.broadcasted_iota(jnp.int32, (rows_q, MIX_W), 0) // t_new
    lane_head = lax.broadcasted_iota(jnp.int32, (rows_q, MIX_W), 1) // HEAD_DIM
    own = row_head == lane_head
    qm = jnp.where(own, qm, 0.0).astype(BF16)
    k_new, v_new = kn_ref[0], vn_ref[0]
    wk, wv = wk_ref[0], wv_ref[0]
    s_old = _dot(qm, wk.astype(BF16)) + bo_ref[...]
    s_new = _dot_nt(qm, k_new.astype(BF16)) + bn_ref[...]
    m = jnp.maximum(jnp.max(s_old, axis=-1, keepdims=True), jnp.max(s_new, axis=-1, keepdims=True))
    p_old = jnp.exp(s_old - m)
    p_new = jnp.exp(s_new - m)
    inv_l = 1.0 / (jnp.sum(p_old, axis=-1, keepdims=True) + jnp.sum(p_new, axis=-1, keepdims=True))
    o = _dot_nt(p_old.astype(BF16), wv.astype(BF16)) + _dot(p_new.astype(BF16), v_new.astype(BF16))
    o = jnp.where(own, o * inv_l, 0.0)
    att = o[0:t_new]
    for h in range(1, N_HEADS):
        att = att + o[h * t_new:(h + 1) * t_new]
    att_ref[0] = att
    lane = lax.broadcasted_iota(jnp.int32, (MIX_W, LANES), 1)
    is_new = lane >= LANES - t_new
    pad = jnp.zeros((LANES - t_new, MIX_W), F32)
    for w, new, o_ref in ((wk, k_new, ok_ref), (wv, v_new, ov_ref)):
        rolled = pltpu.roll(w, wb - t_new, axis=1)
        o_ref[0] = rolled
        new_t = jnp.concatenate([pad, new], axis=0).T
        o_ref[0, :, wb - LANES:wb] = jnp.where(is_new, new_t, rolled[:, wb - LANES:wb])


def _attn_sample(q, k_new, v_new, win_k_t, win_v_t):
    b, t_new, _ = q.shape
    wb = win_k_t.shape[2]
    assert wb >= DIL_GROUPS[-1][0] and t_new % SUBLANES == 0 and wb % LANES == 0
    b_old, b_new = _sample_bias_tables(wb, t_new)
    small = lambda: pl.BlockSpec((1, t_new, MIX_W), lambda i: (i, 0, 0))
    big = lambda: pl.BlockSpec((1, MIX_W, wb), lambda i: (i, 0, 0))
    est = 8 * wb * MIX_W * 4 + 2 * wb * MIX_W * 2 + 4 * N_HEADS * t_new * wb * 4
    return pl.pallas_call(
        functools.partial(_attn_sample_kernel, wb=wb, t_new=t_new),
        grid=(b,),
        in_specs=[small(), small(), small(), big(), big(),
                  pl.BlockSpec((N_HEADS * t_new, wb), lambda i: (0, 0)),
                  pl.BlockSpec((N_HEADS * t_new, t_new), lambda i: (0, 0))],
        out_specs=[small(), big(), big()],
        out_shape=[jax.ShapeDtypeStruct((b, t_new, MIX_W), F32),
                   jax.ShapeDtypeStruct((b, MIX_W, wb), F32),
                   jax.ShapeDtypeStruct((b, MIX_W, wb), F32)],
        compiler_params=_params(("parallel",), est),
        name="attn_sample",
    )(q, k_new, v_new, win_k_t, win_v_t, jnp.asarray(b_old), jnp.asarray(b_new))


ST_ROWS = HEAD_DIM + SUBLANES


def _mlstm_seq_kernel(qt_ref, k_ref, vt_ref, gcol_ref, mirow_ref, mfrow_ref, bcol_ref, bigrow_ref, bfgrow_ref,
                      h_ref, st_ref, m_ref, *, chunk):
    L = chunk
    hp = lax.Precision.HIGHEST

    @pl.when(pl.program_id(1) == 0)
    def _():
        st_ref[...] = jnp.zeros(st_ref.shape, F32)
        m_ref[...] = jnp.full(m_ref.shape, NEG, F32)

    ss = lax.broadcasted_iota(jnp.int32, (L, L), 0)
    tt = lax.broadcasted_iota(jnp.int32, (L, L), 1)
    causal = ss <= tt
    ig_r = mirow_ref[0] + bigrow_ref[...]
    lf_r = _log_sigmoid(mfrow_ref[0] + bfgrow_ref[...])
    b_r = jnp.dot(lf_r, causal.astype(F32), precision=hp, preferred_element_type=F32)
    lane = lax.broadcasted_iota(jnp.int32, (N_HEADS, L), 1)
    cm = ig_r - b_r
    shift = 1
    while shift < L:
        cm = jnp.maximum(cm, jnp.where(lane >= shift, pltpu.roll(cm, shift, axis=1), NEG))
        shift *= 2
    m_prev = m_ref[0][:, 0:1]
    m_t = b_r + jnp.maximum(m_prev, cm)
    row_all = b_r - m_t
    inter_w = jnp.exp(b_r + m_prev - m_t)
    floor = jnp.exp(-m_t)
    m_new = m_t[:, L - 1:L]
    b_last = b_r[:, L - 1:L]
    w_s = jnp.exp(b_last - b_r + ig_r - m_new)
    decay = jnp.exp(b_last + m_prev - m_new)
    pre_c = gcol_ref[0] + bcol_ref[...]
    bcum_c = jnp.dot((ss >= tt).astype(F32), _log_sigmoid(pre_c), precision=hp, preferred_element_type=F32)
    col_all = pre_c - pltpu.roll(bcum_c, GATE_PAD - N_HEADS, axis=1)

    ones_row = (lax.broadcasted_iota(jnp.int32, (SUBLANES, L), 0) == 0).astype(F32)
    zeros_q = jnp.zeros((HEAD_DIM, L), BF16)
    h_parts, new_state = [], []
    for h in range(N_HEADS):
        pair, half = divmod(h, 2)
        rows = slice(h * HEAD_DIM, (h + 1) * HEAD_DIM)
        q_own = qt_ref[0, rows, :].astype(BF16)
        q_pad = jnp.concatenate([q_own, zeros_q] if half == 0 else [zeros_q, q_own], axis=0)
        k_pair = k_ref[0, :, pair * LANES:(pair + 1) * LANES].astype(BF16)
        v_ext = jnp.concatenate([vt_ref[0, rows, :], ones_row], axis=0)
        st = st_ref[0, h]
        d = jnp.exp(jnp.where(causal, col_all[:, h:h + 1] + row_all[h:h + 1, :], NEG))
        a = (_dot(k_pair, q_pad) * d).astype(BF16)
        tot = _dot(v_ext.astype(BF16), a) + inter_w[h:h + 1, :] * _dot(st.astype(BF16), q_pad)
        den = tot[HEAD_DIM:HEAD_DIM + 1, :]
        h_parts.append(tot[0:HEAD_DIM, :] * (1.0 / jnp.maximum(jnp.abs(den), floor[h:h + 1, :])))
        new_state.append(decay[h:h + 1, :] * st + _dot((v_ext * w_s[h:h + 1, :]).astype(BF16), k_pair))
    h_ref[0] = jnp.concatenate(h_parts, axis=0).T
    for h, st_new in enumerate(new_state):
        st_ref[0, h] = st_new
    m_ref[0] = jnp.broadcast_to(m_new, (N_HEADS, LANES))


def _mlstm_seq(q_t, k, v_t, gates, b_ig, b_fg, chunk):
    b, s, _ = k.shape
    assert s % chunk == 0 and chunk % LANES == 0
    g_rows = jnp.swapaxes(gates[:, :, :2 * N_HEADS], 1, 2)
    bcol = jnp.concatenate([b_ig, b_fg, jnp.zeros((GATE_PAD - 2 * N_HEADS,), F32)]).reshape(1, GATE_PAD)
    ft = lambda: pl.BlockSpec((1, MIX_W, chunk), lambda i, c: (i, 0, c))
    row = lambda: pl.BlockSpec((1, N_HEADS, chunk), lambda i, c: (i, 0, c))
    const = lambda r, w: pl.BlockSpec((r, w), lambda i, c: (0, 0))
    est = 8 * chunk * MIX_W * 4 + 12 * chunk * chunk * 4 + 4 * N_HEADS * ST_ROWS * LANES * 4
    h, st, m = pl.pallas_call(
        functools.partial(_mlstm_seq_kernel, chunk=chunk),
        grid=(b, s // chunk),
        in_specs=[ft(), pl.BlockSpec((1, chunk, MIX_W), lambda i, c: (i, c, 0)), ft(),
                  pl.BlockSpec((1, chunk, GATE_PAD), lambda i, c: (i, c, 0)), row(), row(),
                  const(1, GATE_PAD), const(N_HEADS, 1), const(N_HEADS, 1)],
        out_specs=[pl.BlockSpec((1, chunk, MIX_W), lambda i, c: (i, c, 0)),
                   pl.BlockSpec((1, N_HEADS, ST_ROWS, LANES), lambda i, c: (i, 0, 0, 0)),
                   pl.BlockSpec((1, N_HEADS, LANES), lambda i, c: (i, 0, 0))],
        out_shape=[jax.ShapeDtypeStruct((b, s, MIX_W), F32),
                   jax.ShapeDtypeStruct((b, N_HEADS, ST_ROWS, LANES), F32),
                   jax.ShapeDtypeStruct((b, N_HEADS, LANES), F32)],
        compiler_params=_params(("arbitrary", "arbitrary"), est),
        name="mlstm_seq",
    )(q_t, k, v_t, gates, g_rows[:, :N_HEADS], g_rows[:, N_HEADS:], bcol, b_ig.reshape(-1, 1), b_fg.reshape(-1, 1))
    st = st.reshape(b, N_HEADS // 2, 2, ST_ROWS, 2, HEAD_DIM)
    st = jnp.stack([st[:, :, 0, :, 0, :], st[:, :, 1, :, 1, :]], axis=2).reshape(b, N_HEADS, ST_ROWS, HEAD_DIM)
    c = jnp.swapaxes(st[:, :, :HEAD_DIM, :], 2, 3)
    return h, c, st[:, :, HEAD_DIM, :], m[:, :, 0]


def _mlstm_kernel(qk_ref, cinit_ref, cw_ref, cb_ref, mv_ref, gcol_ref, mirow_ref, mfrow_ref,
                  bcol_ref, bigrow_ref, bfgrow_ref, c0_ref, n0_ref, m0_ref,
                  h_ref, cst_ref, c_ref, n_ref, m_ref, xp_scr, *, chunk, bt):
    L = chunk

    @pl.when(pl.program_id(1) == 0)
    def _():
        c_ref[...] = c0_ref[...]
        n_ref[...] = n0_ref[...]
        m_ref[...] = m0_ref[...]
        xp_scr[:, 0:SUBLANES, :] = cinit_ref[...]

    rr = lax.broadcasted_iota(jnp.int32, (L, L), 0)
    cc = lax.broadcasted_iota(jnp.int32, (L, L), 1)
    causal = rr >= cc
    tril = causal.astype(F32)
    triu = (rr <= cc).astype(F32)
    hp = lax.Precision.HIGHEST

    def one(bi, carry):
        xp_scr[bi, SUBLANES:SUBLANES + L, :] = qk_ref[bi]
        conv = cb_ref[...]
        for j in range(CONV_W):
            off = SUBLANES - (CONV_W - 1) + j
            conv = conv + xp_scr[bi, off:off + L, :] * cw_ref[j:j + 1, :]
        tail = xp_scr[bi, L:L + SUBLANES, :]
        xp_scr[bi, 0:SUBLANES, :] = tail
        cst_ref[bi] = tail
        qk = conv * _sigmoid(conv)
        v_all = mv_ref[bi]

        pre_c = gcol_ref[bi] + bcol_ref[...]
        bcum_c = jnp.dot(tril, _log_sigmoid(pre_c), precision=hp, preferred_element_type=F32)
        ig_r_all = mirow_ref[bi] + bigrow_ref[...]
        bcum_r = jnp.dot(_log_sigmoid(mfrow_ref[bi] + bfgrow_ref[...]), triu, precision=hp,
                         preferred_element_type=F32)

        c_all, n_all, m_all = c_ref[bi], n_ref[bi], m_ref[bi]
        new_state = []
        for h in range(N_HEADS):
            sl = slice(h * HEAD_DIM, (h + 1) * HEAD_DIM)
            q = qk[:, sl]
            k = qk[:, MIX_W + h * HEAD_DIM:MIX_W + (h + 1) * HEAD_DIM] * (HEAD_DIM ** -0.5)
            v = v_all[:, sl]
            ig_c = pre_c[:, h:h + 1]
            b_c = bcum_c[:, N_HEADS + h:N_HEADS + h + 1]
            b_r = bcum_r[h:h + 1, :]
            ig_r = ig_r_all[h:h + 1, :]
            c_prev = c_all[h]
            n_prev = n_all[h:h + 1, :]
            m_prev = m_all[:, h:h + 1]

            log_d = jnp.where(causal, b_c - b_r + ig_r, NEG)
            log_inter = b_c + m_prev
            m_t = jnp.maximum(log_inter, jnp.max(log_d, axis=-1, keepdims=True))
            d_w = jnp.exp(log_d - m_t)
            inter_w = jnp.exp(log_inter - m_t)
            qb, kb, vb = q.astype(BF16), k.astype(BF16), v.astype(BF16)
            a = _dot_nt(qb, kb) * d_w
            num = _dot(a.astype(BF16), vb) + inter_w * _dot(qb, c_prev.astype(BF16))
            den = jnp.sum(a, axis=-1, keepdims=True) + inter_w * jnp.sum(q * n_prev, axis=-1, keepdims=True)
            h_ref[bi, :, sl] = num * (1.0 / jnp.maximum(jnp.abs(den), jnp.exp(-m_t)))

            m_new = m_t[L - 1:L, :]
            b_last = b_c[L - 1:L, :]
            w_s = jnp.exp(b_last - b_c + ig_c - m_new)
            decay = jnp.exp(b_last + m_prev - m_new)
            kw = k * w_s
            new_state.append((decay * c_prev + _dot_tn(kw.astype(BF16), vb),
                              decay * n_prev + jnp.sum(kw, axis=0, keepdims=True), m_new))
        for h, (c_new, n_new, m_new) in enumerate(new_state):
            c_ref[bi, h] = c_new
            n_ref[bi, h:h + 1, :] = n_new
            m_ref[bi, :, h:h + 1] = m_new
        return carry

    if bt == 1:
        one(0, 0)
    else:
        lax.fori_loop(0, bt, one, 0, unroll=2)


def _mlstm(qk_pre, conv_init, conv_w, conv_b, mv, gates, b_ig, b_fg, c0, n0, m0, chunk, bt):
    b, s, _ = qk_pre.shape
    assert s % chunk == 0 and b % bt == 0 and chunk % SUBLANES == 0
    cinit = jnp.pad(conv_init, ((0, 0), (SUBLANES - (CONV_W - 1), 0), (0, 0)))
    cw = jnp.pad(conv_w, ((0, SUBLANES - CONV_W), (0, 0)))
    g_rows = jnp.swapaxes(gates[:, :, :2 * N_HEADS], 1, 2)
    mi_row, mf_row = g_rows[:, :N_HEADS], g_rows[:, N_HEADS:]
    bcol = jnp.concatenate([b_ig, b_fg, jnp.zeros((GATE_PAD - 2 * N_HEADS,), F32)]).reshape(1, GATE_PAD)
    nc = s // chunk
    seq3 = lambda w: pl.BlockSpec((bt, chunk, w), lambda i, c: (i, c, 0))
    fix3 = lambda r, w: pl.BlockSpec((bt, r, w), lambda i, c: (i, 0, 0))
    row3 = lambda: pl.BlockSpec((bt, N_HEADS, chunk), lambda i, c: (i, 0, c))
    const = lambda r, w: pl.BlockSpec((r, w), lambda i, c: (0, 0))
    cspec = lambda: pl.BlockSpec((bt, N_HEADS, HEAD_DIM, HEAD_DIM), lambda i, c: (i, 0, 0, 0))
    est = (2 * bt * chunk * (2 * MIX_W + MIX_W + GATE_PAD + MIX_W) * 4 + bt * (chunk + SUBLANES) * 2 * MIX_W * 4
           + 6 * bt * N_HEADS * HEAD_DIM * HEAD_DIM * 4 + 8 * chunk * chunk * 4 + 6 * chunk * 2 * MIX_W * 4)
    outs = pl.pallas_call(
        functools.partial(_mlstm_kernel, chunk=chunk, bt=bt),
        grid=(b // bt, nc),
        in_specs=[seq3(2 * MIX_W), fix3(SUBLANES, 2 * MIX_W), const(SUBLANES, 2 * MIX_W), const(1, 2 * MIX_W),
                  seq3(MIX_W), seq3(GATE_PAD), row3(), row3(),
                  const(1, GATE_PAD), const(N_HEADS, 1), const(N_HEADS, 1),
                  cspec(), fix3(N_HEADS, HEAD_DIM), fix3(1, N_HEADS)],
        out_specs=[seq3(MIX_W), fix3(SUBLANES, 2 * MIX_W), cspec(), fix3(N_HEADS, HEAD_DIM), fix3(1, N_HEADS)],
        out_shape=[jax.ShapeDtypeStruct((b, s, MIX_W), F32),
                   jax.ShapeDtypeStruct((b, SUBLANES, 2 * MIX_W), F32),
                   jax.ShapeDtypeStruct((b, N_HEADS, HEAD_DIM, HEAD_DIM), F32),
                   jax.ShapeDtypeStruct((b, N_HEADS, HEAD_DIM), F32),
                   jax.ShapeDtypeStruct((b, 1, N_HEADS), F32)],
        scratch_shapes=[pltpu.VMEM((bt, chunk + SUBLANES, 2 * MIX_W), F32)],
        compiler_params=_params(("arbitrary", "arbitrary"), est),
        name="mlstm",
    )(qk_pre, cinit, cw, conv_b.reshape(1, -1), mv, gates, mi_row, mf_row,
      bcol, b_ig.reshape(-1, 1), b_fg.reshape(-1, 1), c0, n0, m0.reshape(b, 1, N_HEADS))
    h, cst, c, n, m = outs
    return h, cst[:, SUBLANES - (CONV_W - 1):], c, n, m.reshape(b, N_HEADS)


def _mixers_to_output(x2, att, mlh, mo, mem_k, mem_v, wts, bt_x, tq_x, tm):
    b = mem_k.shape[0]
    mk, mv_ = _mem_tile_view(mem_k), _mem_tile_view(mem_v)
    h1 = _merge(att, mlh, mo, wts["b_og"], wts["g_attn_out"], wts["g_mlstm_out"], wts["w_out"], x2, tm)
    if tq_x >= 2 * LANES:
        h2 = _xattn_block(h1.reshape(b, -1, D_MODEL), wts["g_cross"], wts["w_cq"], mk, mv_, wts["w_co"],
                          tq_x).reshape(-1, D_MODEL)
    else:
        (xq,) = _norm_matmul(h1, wts["g_cross"], wts["w_cq"], (D_MODEL,), tm)
        xo = _xattn(xq.reshape(b, -1, D_MODEL), mk, mv_, bt_x, tq_x)
        h2 = _matmul_res(xo.reshape(-1, D_MODEL), wts["w_co"], h1, tm)
    return _ffn(h2, wts["g_ffn"], wts["w_gate"], wts["w_up"], wts["w_down"], wts["g_final"], tm)


def kernel(x_prompt, x_sample, mem_prompt, cache_win_k, cache_win_v, cache_mem_k, cache_mem_v, state_conv, state_mlstm_C, state_mlstm_n, state_mlstm_m, g_mix, w_in, conv_w, conv_b, b_ig, b_fg, b_og, g_attn_out, g_mlstm_out, w_out, g_cross, g_mem, w_cq, w_ck, w_cv, w_co, g_ffn, w_gate, w_up, w_down, g_final):
    bp, sp, _ = x_prompt.shape
    bs, ts, _ = x_sample.shape
    wts = dict(b_og=b_og, g_attn_out=g_attn_out, g_mlstm_out=g_mlstm_out, g_cross=g_cross, g_ffn=g_ffn,
               g_final=g_final, w_out=w_out.astype(BF16), w_cq=w_cq.astype(BF16), w_co=w_co.astype(BF16),
               w_gate=w_gate.astype(BF16), w_up=w_up.astype(BF16), w_down=w_down.astype(BF16))
    w_in_p = jnp.pad(w_in, ((0, 0), (0, sum(IN_WIDTHS) - N_IN))).astype(BF16)
    w_ckv = jnp.concatenate([w_ck, w_cv], axis=1).astype(BF16)
    tm = 512

    xp2 = x_prompt.reshape(bp * sp, D_MODEL)
    aq_c, ak_c, av_c, ak_t, av_t, mq_t, mk, mv_t, mo, gates, conv_tail = _inproj_prompt(
        xp2, g_mix, w_in_p, conv_w, conv_b, tm, sp)
    r3 = lambda a: a.reshape(bp, sp, -1)
    att = _attn_prompt(aq_c, ak_c, av_c)
    keep = min(DIL_GROUPS[-1][0], sp)
    from_t = lambda a: jnp.transpose(a.reshape(a.shape[0], N_HEADS, HEAD_DIM, a.shape[2]), (0, 3, 1, 2))
    win_k_p = from_t(ak_t)[:, sp - keep:]
    win_v_p = from_t(av_t)[:, sp - keep:]
    mlh, c_p, n_p, m_p = _mlstm_seq(mq_t, r3(mk), mv_t, r3(gates), b_ig, b_fg, chunk=256)
    conv_p = conv_tail[:, SUBLANES - (CONV_W - 1):]
    mem_k_p, mem_v_p = _norm_matmul(mem_prompt.reshape(bp * N_MEM, D_MODEL), g_mem, w_ckv, (D_MODEL, D_MODEL), tm)
    mem_k_p = mem_k_p.reshape(bp, N_MEM, N_X_HEADS, X_HEAD_DIM)
    mem_v_p = mem_v_p.reshape(bp, N_MEM, N_X_HEADS, X_HEAD_DIM)
    y_p = _mixers_to_output(xp2, att.reshape(-1, MIX_W), mlh.reshape(-1, MIX_W), mo, mem_k_p, mem_v_p, wts,
                            bt_x=1, tq_x=512, tm=tm)

    xs2 = x_sample.reshape(bs * ts, D_MODEL)
    aq, ak, av, qk_pre, mv, mo, gates = _norm_matmul(xs2, g_mix, w_in_p, IN_WIDTHS, tm)
    r3 = lambda a: a.reshape(bs, ts, -1)
    wb = cache_win_k.shape[1]
    to_t = lambda a: jnp.transpose(a, (0, 2, 3, 1)).reshape(a.shape[0], MIX_W, a.shape[1])
    att, win_k_s, win_v_s = _attn_sample(r3(aq), r3(ak), r3(av), to_t(cache_win_k), to_t(cache_win_v))
    mlh, conv_s, c_s, n_s, m_s = _mlstm(r3(qk_pre), state_conv, conv_w, conv_b, r3(mv), r3(gates), b_ig, b_fg,
                                        state_mlstm_C, state_mlstm_n, state_mlstm_m, chunk=ts, bt=8)
    y_s = _mixers_to_output(xs2, att.reshape(-1, MIX_W), mlh.reshape(-1, MIX_W), mo, cache_mem_k, cache_mem_v, wts,
                            bt_x=4, tq_x=ts, tm=tm)

    return (y_p.reshape(bp, sp, D_MODEL), y_s.reshape(bs, ts, D_MODEL),
            win_k_p, win_v_p, conv_p, c_p, n_p, m_p, mem_k_p, mem_v_p,
            from_t(win_k_s), from_t(win_v_s), conv_s, c_s, n_s, m_s)
```

```python
import functools

import numpy as np
import jax
import jax.numpy as jnp
from jax import lax
from jax.experimental import pallas as pl
from jax.experimental.pallas import tpu as pltpu

F32 = jnp.float32
BF16 = jnp.bfloat16

D_MODEL = 1024
HEAD_DIM = 64
N_HEADS = 8
MIX_W = N_HEADS * HEAD_DIM
DIL_GROUPS = ((128, 1), (512, 4), (2048, 16))
BAND = 128
CONV_W = 4
N_MEM = 256
N_X_HEADS = 4
X_HEAD_DIM = D_MODEL // N_X_HEADS
D_FF = 2816
RMS_EPS = 1e-6
NEG = -1e30
N_IN = 3 * MIX_W + 2 * MIX_W + MIX_W + MIX_W + 2 * N_HEADS
GATE_PAD = 128

LANES = 128
SUBLANES = 8
VMEM_BYTES_V7X = 64 * 1024 * 1024


def _vmem_limit(nbytes):
    return int(min(max(2 * nbytes, 16 * 1024 * 1024), VMEM_BYTES_V7X - 8 * 1024 * 1024))


def _params(semantics, vmem_estimate):
    return pltpu.CompilerParams(dimension_semantics=semantics,
                                vmem_limit_bytes=_vmem_limit(vmem_estimate))


def _rms(x, g):
    return x * lax.rsqrt(jnp.mean(x * x, axis=-1, keepdims=True) + RMS_EPS) * g


def _sigmoid(x):
    return 1.0 / (1.0 + jnp.exp(-x))


def _log_sigmoid(x):
    return jnp.minimum(x, 0.0) - jnp.log(1.0 + jnp.exp(-jnp.abs(x)))


def _dot(a, b):
    return jnp.dot(a, b, preferred_element_type=F32)


def _dot_nt(a, b):
    return lax.dot_general(a, b, (((1,), (1,)), ((), ())), preferred_element_type=F32)


N_CLASSES = max(d for _, d in DIL_GROUPS)
IN_WIDTHS = (MIX_W, MIX_W, MIX_W, 2 * MIX_W, MIX_W, MIX_W, GATE_PAD)
IN_OFFS = tuple(int(o) for o in np.cumsum((0,) + IN_WIDTHS))


def _norm_matmul_kernel(x_ref, g_ref, w_ref, *o_refs, splits):
    xn = _rms(x_ref[...], g_ref[...]).astype(BF16)
    for o_ref, (a, b) in zip(o_refs, splits):
        o_ref[...] = _dot(xn, w_ref[:, a:b])


def _norm_matmul(x, g, w, widths, tm):
    m, k = x.shape
    n = w.shape[1]
    assert sum(widths) == n and m % tm == 0
    offs = np.cumsum([0] + list(widths))
    splits = tuple((int(offs[i]), int(offs[i + 1])) for i in range(len(widths)))
    est = 2 * tm * k * 4 + 2 * k * n * 2 + 2 * tm * n * 4
    return pl.pallas_call(
        functools.partial(_norm_matmul_kernel, splits=splits),
        grid=(m // tm,),
        in_specs=[pl.BlockSpec((tm, k), lambda i: (i, 0)),
                  pl.BlockSpec((1, k), lambda i: (0, 0)),
                  pl.BlockSpec((k, n), lambda i: (0, 0))],
        out_specs=[pl.BlockSpec((tm, wd), lambda i: (i, 0)) for wd in widths],
        out_shape=[jax.ShapeDtypeStruct((m, wd), F32) for wd in widths],
        compiler_params=_params(("parallel",), est),
        name="norm_matmul",
    )(x, g.reshape(1, k), w)


def _inproj_prompt_kernel(x_ref, g_ref, w_ref, wvt_ref, cw_ref, cb_ref,
                          qc_ref, kc_ref, vc_ref, kt_ref, vt_ref, mq_ref, mk_ref, mvt_ref, mo_ref, gate_ref, tail_ref,
                          cls_scr, xp_scr, *, per_seq):
    tm = x_ref.shape[0]

    @pl.when(pl.program_id(0) % per_seq == 0)
    def _():
        xp_scr[0:SUBLANES, :] = jnp.zeros((SUBLANES, 2 * MIX_W), F32)

    xn = _rms(x_ref[...], g_ref[...]).astype(BF16)
    proj = lambda idx: _dot(xn, w_ref[:, IN_OFFS[idx]:IN_OFFS[idx + 1]])

    xp_scr[SUBLANES:SUBLANES + tm, :] = proj(3)
    conv = cb_ref[...]
    for j in range(CONV_W):
        off = SUBLANES - (CONV_W - 1) + j
        conv = conv + xp_scr[off:off + tm, :] * cw_ref[j:j + 1, :]
    tail = xp_scr[tm:tm + SUBLANES, :]
    xp_scr[0:SUBLANES, :] = tail
    tail_ref[0] = tail
    act = conv * _sigmoid(conv)
    mq_ref[0] = act[:, :MIX_W].T
    mk_ref[...] = act[:, MIX_W:] * (HEAD_DIM ** -0.5)

    for idx, c_ref, t_ref in ((0, qc_ref, None), (1, kc_ref, kt_ref), (2, vc_ref, vt_ref)):
        y = proj(idx)
        if t_ref is not None:
            t_ref[0] = y.T
        scr = cls_scr.at[idx]
        for lt in range(MIX_W // LANES):
            scr[lt] = y[:, lt * LANES:(lt + 1) * LANES]
        for c in range(N_CLASSES):
            for lt in range(MIX_W // LANES):
                c_ref[0, c, :, lt * LANES:(lt + 1) * LANES] = scr[lt, pl.ds(c, tm // N_CLASSES, stride=N_CLASSES), :]

    mvt_ref[0] = _dot_nt(wvt_ref[...], xn)
    mo_ref[...] = proj(5)
    gate_ref[...] = proj(6)


def _inproj_prompt(x, g, w, conv_w, conv_b, tm, seq):
    m, k = x.shape
    n = w.shape[1]
    assert n == IN_OFFS[-1] and m % seq == 0 and seq % tm == 0 and tm % (N_CLASSES * SUBLANES) == 0
    b, per_seq = m // seq, seq // tm
    est = (2 * tm * k * 4 + 2 * k * n * 2 + 2 * tm * (n + 4 * MIX_W) * 4 + 3 * tm * MIX_W * 4
           + 3 * (tm + SUBLANES) * 2 * MIX_W * 4)
    cls_spec = lambda: pl.BlockSpec((1, N_CLASSES, tm // N_CLASSES, MIX_W), lambda i: (i // per_seq, 0, i % per_seq, 0))
    cls_shape = jax.ShapeDtypeStruct((b, N_CLASSES, seq // N_CLASSES, MIX_W), F32)
    t_spec = lambda: pl.BlockSpec((1, MIX_W, tm), lambda i: (i // per_seq, 0, i % per_seq))
    t_shape = jax.ShapeDtypeStruct((b, MIX_W, seq), F32)
    nat_spec = lambda wd: pl.BlockSpec((tm, wd), lambda i: (i, 0))
    nat_shape = lambda wd: jax.ShapeDtypeStruct((m, wd), F32)
    cw = jnp.pad(conv_w, ((0, SUBLANES - CONV_W), (0, 0)))
    return pl.pallas_call(
        functools.partial(_inproj_prompt_kernel, per_seq=per_seq),
        grid=(m // tm,),
        in_specs=[pl.BlockSpec((tm, k), lambda i: (i, 0)),
                  pl.BlockSpec((1, k), lambda i: (0, 0)),
                  pl.BlockSpec((k, n), lambda i: (0, 0)),
                  pl.BlockSpec((MIX_W, k), lambda i: (0, 0)),
                  pl.BlockSpec((SUBLANES, 2 * MIX_W), lambda i: (0, 0)),
                  pl.BlockSpec((1, 2 * MIX_W), lambda i: (0, 0))],
        out_specs=[cls_spec(), cls_spec(), cls_spec(), t_spec(), t_spec(), t_spec(), nat_spec(MIX_W), t_spec(),
                   nat_spec(MIX_W), nat_spec(GATE_PAD),
                   pl.BlockSpec((1, SUBLANES, 2 * MIX_W), lambda i: (i // per_seq, 0, 0))],
        out_shape=[cls_shape, cls_shape, cls_shape, t_shape, t_shape, t_shape, nat_shape(MIX_W), t_shape,
                   nat_shape(MIX_W), nat_shape(GATE_PAD),
                   jax.ShapeDtypeStruct((b, SUBLANES, 2 * MIX_W), F32)],
        scratch_shapes=[pltpu.VMEM((3, MIX_W // LANES, tm, LANES), F32),
                        pltpu.VMEM((tm + SUBLANES, 2 * MIX_W), F32)],
        compiler_params=_params(("arbitrary",), est),
        name="inproj_prompt",
    )(x, g.reshape(1, k), w, w[:, IN_OFFS[4]:IN_OFFS[5]].T, cw, conv_b.reshape(1, -1))


def _matmul_res_kernel(x_ref, w_ref, r_ref, o_ref):
    o_ref[...] = r_ref[...] + _dot(x_ref[...].astype(BF16), w_ref[...])


def _matmul_res(x, w, res, tm):
    m, k = x.shape
    n = w.shape[1]
    est = 2 * tm * k * 4 + 2 * k * n * 2 + 4 * tm * n * 4
    return pl.pallas_call(
        _matmul_res_kernel,
        grid=(m // tm,),
        in_specs=[pl.BlockSpec((tm, k), lambda i: (i, 0)),
                  pl.BlockSpec((k, n), lambda i: (0, 0)),
                  pl.BlockSpec((tm, n), lambda i: (i, 0))],
        out_specs=pl.BlockSpec((tm, n), lambda i: (i, 0)),
        out_shape=jax.ShapeDtypeStruct((m, n), F32),
        compiler_params=_params(("parallel",), est),
        name="matmul_res",
    )(x, w, res)


def _merge_kernel(att_ref, mlh_ref, mo_ref, bog_ref, ga_ref, gm_ref, w_ref, x_ref, o_ref):
    a_n = _rms(att_ref[...], ga_ref[...]).astype(BF16)
    ml = _sigmoid(mo_ref[...] + bog_ref[...]) * mlh_ref[...]
    m_n = _rms(ml, gm_ref[...]).astype(BF16)
    o_ref[...] = x_ref[...] + _dot(a_n, w_ref[0:MIX_W, :]) + _dot(m_n, w_ref[MIX_W:2 * MIX_W, :])


def _merge(att, mlh, mo, b_og, g_a, g_m, w_out, x, tm):
    m = x.shape[0]
    half = lambda: pl.BlockSpec((tm, MIX_W), lambda i: (i, 0))
    vec = lambda: pl.BlockSpec((1, MIX_W), lambda i: (0, 0))
    est = 6 * tm * MIX_W * 4 + 2 * 2 * MIX_W * D_MODEL * 2 + 4 * tm * D_MODEL * 4
    return pl.pallas_call(
        _merge_kernel,
        grid=(m // tm,),
        in_specs=[half(), half(), half(), vec(), vec(), vec(),
                  pl.BlockSpec((2 * MIX_W, D_MODEL), lambda i: (0, 0)),
                  pl.BlockSpec((tm, D_MODEL), lambda i: (i, 0))],
        out_specs=pl.BlockSpec((tm, D_MODEL), lambda i: (i, 0)),
        out_shape=jax.ShapeDtypeStruct((m, D_MODEL), F32),
        compiler_params=_params(("parallel",), est),
        name="merge",
    )(att, mlh, mo, b_og.reshape(1, -1), g_a.reshape(1, -1), g_m.reshape(1, -1), w_out, x)


FF_CHUNK = D_FF // 2


def _ffn_kernel(h_ref, gf_ref, wg_ref, wu_ref, wd_ref, gl_ref, o_ref):
    h = h_ref[...]
    hn = _rms(h, gf_ref[...]).astype(BF16)
    acc = h
    for c in range(D_FF // FF_CHUNK):
        sl = slice(c * FF_CHUNK, (c + 1) * FF_CHUNK)
        gate = _dot(hn, wg_ref[:, sl])
        up = _dot(hn, wu_ref[:, sl])
        act = (gate * _sigmoid(gate) * up).astype(BF16)
        acc = acc + _dot(act, wd_ref[sl, :])
    o_ref[...] = _rms(acc, gl_ref[...])


def _ffn(h, g_ffn, w_gate, w_up, w_down, g_final, tm):
    m = h.shape[0]
    est = 4 * tm * D_MODEL * 4 + 2 * 3 * D_MODEL * D_FF * 2 + 3 * tm * FF_CHUNK * 4
    return pl.pallas_call(
        _ffn_kernel,
        grid=(m // tm,),
        in_specs=[pl.BlockSpec((tm, D_MODEL), lambda i: (i, 0)),
                  pl.BlockSpec((1, D_MODEL), lambda i: (0, 0)),
                  pl.BlockSpec((D_MODEL, D_FF), lambda i: (0, 0)),
                  pl.BlockSpec((D_MODEL, D_FF), lambda i: (0, 0)),
                  pl.BlockSpec((D_FF, D_MODEL), lambda i: (0, 0)),
                  pl.BlockSpec((1, D_MODEL), lambda i: (0, 0))],
        out_specs=pl.BlockSpec((tm, D_MODEL), lambda i: (i, 0)),
        out_shape=jax.ShapeDtypeStruct((m, D_MODEL), F32),
        compiler_params=_params(("parallel",), est),
        name="ffn",
    )(h, g_ffn.reshape(1, -1), w_gate, w_up, w_down, g_final.reshape(1, -1))


X_HALVES = X_HEAD_DIM // LANES
X_ROWS = N_X_HEADS * X_HALVES


def _mem_tile_view(a):
    b = a.shape[0]
    a = a.reshape(b, N_MEM, N_X_HEADS, X_HALVES, LANES)
    return jnp.transpose(a, (0, 1, 3, 2, 4)).reshape(b, N_MEM * X_ROWS, LANES)


def _mem_head(ref, bi, h):
    halves = [ref[bi, pl.ds(j * N_X_HEADS + h, N_MEM, stride=X_ROWS), :] for j in range(X_HALVES)]
    return jnp.concatenate(halves, axis=1).astype(BF16)


def _xattn_kernel(q_ref, k_ref, v_ref, o_ref, *, bt):
    for bi in range(bt):
        for h in range(N_X_HEADS):
            sl = slice(h * X_HEAD_DIM, (h + 1) * X_HEAD_DIM)
            q = (q_ref[bi, :, sl] * (X_HEAD_DIM ** -0.5)).astype(BF16)
            s = _dot_nt(q, _mem_head(k_ref, bi, h))
            p = jnp.exp(s - jnp.max(s, axis=-1, keepdims=True))
            inv_l = 1.0 / jnp.sum(p, axis=-1, keepdims=True)
            o_ref[bi, :, sl] = _dot(p.astype(BF16), _mem_head(v_ref, bi, h)) * inv_l


def _xattn(q, mem_k, mem_v, bt, tq):
    b, t, _ = q.shape
    est = 4 * bt * tq * D_MODEL * 4 + 4 * bt * N_MEM * D_MODEL * 4
    kv = lambda: pl.BlockSpec((bt, N_MEM * X_ROWS, LANES), lambda i, j: (i, 0, 0))
    return pl.pallas_call(
        functools.partial(_xattn_kernel, bt=bt),
        grid=(b // bt, t // tq),
        in_specs=[pl.BlockSpec((bt, tq, D_MODEL), lambda i, j: (i, j, 0)), kv(), kv()],
        out_specs=pl.BlockSpec((bt, tq, D_MODEL), lambda i, j: (i, j, 0)),
        out_shape=jax.ShapeDtypeStruct((b, t, D_MODEL), F32),
        compiler_params=_params(("parallel", "parallel"), est),
        name="xattn",
    )(q, mem_k, mem_v)


def _xattn_block_kernel(h_ref, g_ref, wq_ref, k_ref, v_ref, wo_ref, o_ref):
    h1 = h_ref[0]
    q_all = _dot(_rms(h1, g_ref[...]).astype(BF16), wq_ref[...]) * (X_HEAD_DIM ** -0.5)
    o_parts = []
    for h in range(N_X_HEADS):
        sl = slice(h * X_HEAD_DIM, (h + 1) * X_HEAD_DIM)
        s = _dot_nt(q_all[:, sl].astype(BF16), _mem_head(k_ref, 0, h))
        p = jnp.exp(s - jnp.max(s, axis=-1, keepdims=True))
        inv_l = 1.0 / jnp.sum(p, axis=-1, keepdims=True)
        o_parts.append((_dot(p.astype(BF16), _mem_head(v_ref, 0, h)) * inv_l).astype(BF16))
    o_ref[0] = h1 + _dot(jnp.concatenate(o_parts, axis=1), wo_ref[...])


def _xattn_block(h1, g, w_cq, mem_k, mem_v, w_co, tq):
    b, t, _ = h1.shape
    est = 4 * tq * D_MODEL * 4 + 4 * N_MEM * D_MODEL * 4 + 2 * 2 * D_MODEL * D_MODEL * 2 + 4 * tq * D_MODEL * 4
    kv = lambda: pl.BlockSpec((1, N_MEM * X_ROWS, LANES), lambda i, j: (i, 0, 0))
    wspec = lambda: pl.BlockSpec((D_MODEL, D_MODEL), lambda i, j: (0, 0))
    return pl.pallas_call(
        _xattn_block_kernel,
        grid=(b, t // tq),
        in_specs=[pl.BlockSpec((1, tq, D_MODEL), lambda i, j: (i, j, 0)),
                  pl.BlockSpec((1, D_MODEL), lambda i, j: (0, 0)), wspec(), kv(), kv(), wspec()],
        out_specs=pl.BlockSpec((1, tq, D_MODEL), lambda i, j: (i, j, 0)),
        out_shape=jax.ShapeDtypeStruct((b, t, D_MODEL), F32),
        compiler_params=_params(("parallel", "parallel"), est),
        name="xattn_block",
    )(h1, g.reshape(1, -1), w_cq, mem_k, mem_v, w_co)


def _alibi_slopes():
    return np.exp2(-8.0 * (np.arange(N_HEADS, dtype=np.float64) + 1.0) / N_HEADS)


def _prompt_bias_tables():
    full, first = [], []
    for _, dil in DIL_GROUPS:
        jn = N_CLASSES // dil
        rn = BAND // jn
        qj, qm = np.divmod(np.arange(BAND), rn)
        kj, km = np.divmod(np.arange(2 * BAND), 2 * rn)
        for tabs, row_offset in ((full, rn), (first, 0)):
            delta = jn * (row_offset + qm[:, None] - km[None, :]) + (qj[:, None] - kj[None, :])
            valid = (delta >= 0) & (delta <= BAND)
            bias = -_alibi_slopes()[:, None, None] * (delta * dil)[None].astype(np.float64)
            tabs.append(np.where(valid[None], bias, NEG))
    return np.stack(full).astype(np.float32), np.stack(first).astype(np.float32)


UNITS_PER_TRIP = 16


def _largest_divisor(n, cap):
    return max(u for u in range(1, cap + 1) if n % u == 0)


def _attn_prompt_kernel(q_ref, k_ref, v_ref, bfull_ref, bfirst_ref, o_ref, og_ref, lg_ref, *, rows_per_class):
    lane = lax.broadcasted_iota(jnp.int32, (1, LANES), 1)
    first_head = lane < HEAD_DIM

    def unit(g, dil, r_d, blk, full):
        jn = N_CLASSES // dil
        rn = BAND // jn
        classes = [r_d + dil * j for j in range(jn)]
        if full:
            m0 = pl.multiple_of(blk * rn, SUBLANES)
            q_rows = pl.ds(m0, rn)
            k_rows = pl.ds(pl.multiple_of(m0 - rn, SUBLANES), 2 * rn)
        else:
            assert blk == 0
            q_rows = pl.ds(0, rn)
            k_rows = pl.ds(0, min(2 * rn, rows_per_class))
        gather = lambda ref, rows: jnp.concatenate([ref[0, c, rows, :] for c in classes], axis=0)
        q = gather(q_ref, q_rows) * (HEAD_DIM ** -0.5)
        k = gather(k_ref, k_rows).astype(BF16)
        v = gather(v_ref, k_rows).astype(BF16)
        if k.shape[0] < 2 * BAND:
            assert jn == 1
            k = jnp.concatenate([k, k], axis=0)
            v = jnp.concatenate([v, v], axis=0)
        outs, lses = [], []
        for h in range(2):
            qh = jnp.where(first_head if h == 0 else jnp.logical_not(first_head), q, 0.0).astype(BF16)
            s = _dot_nt(qh, k) + (bfull_ref[g, h] if full else bfirst_ref[g, h])
            m = jnp.max(s, axis=-1, keepdims=True)
            p = jnp.exp(s - m)
            l = jnp.sum(p, axis=-1, keepdims=True)
            outs.append(_dot(p.astype(BF16), v) * (1.0 / l))
            lses.append(m + jnp.log(l))
        o = jnp.where(first_head, outs[0], outs[1])
        lse = jnp.broadcast_to(jnp.where(first_head, lses[0], lses[1]), o.shape)
        for j, c in enumerate(classes):
            og_ref[g, c, q_rows, :] = o[j * rn:(j + 1) * rn]
            lg_ref[g, c, q_rows, :] = lse[j * rn:(j + 1) * rn]

    for g, (window, dil) in enumerate(DIL_GROUPS):
        assert window // dil == BAND and N_CLASSES % dil == 0
        nblk = rows_per_class * N_CLASSES // (dil * BAND)
        if dil <= UNITS_PER_TRIP:
            for r in range(dil):
                unit(g, dil, r, 0, False)
        else:
            def body1(r, carry, g=g, dil=dil):
                unit(g, dil, r, 0, False)
                return carry
            lax.fori_loop(0, dil, body1, 0, unroll=_largest_divisor(dil, UNITS_PER_TRIP))
        rest = dil * (nblk - 1)
        if rest:
            def body(j, carry, g=g, dil=dil, nblk=nblk):
                unit(g, dil, j // (nblk - 1), j % (nblk - 1) + 1, True)
                return carry
            lax.fori_loop(0, rest, body, 0, unroll=_largest_divisor(rest, UNITS_PER_TRIP))

    def merge(c, carry):
        l0, l1, l2 = lg_ref[0, c], lg_ref[1, c], lg_ref[2, c]
        mx = jnp.maximum(jnp.maximum(l0, l1), l2)
        e0, e1, e2 = jnp.exp(l0 - mx), jnp.exp(l1 - mx), jnp.exp(l2 - mx)
        merged = (e0 * og_ref[0, c] + e1 * og_ref[1, c] + e2 * og_ref[2, c]) * (1.0 / (e0 + e1 + e2))
        o_ref[0, pl.ds(c, rows_per_class, stride=N_CLASSES), :] = merged
        return carry
    lax.fori_loop(0, N_CLASSES, merge, 0, unroll=2)


def _attn_prompt(q, k, v):
    b, _, rpc, _ = q.shape
    s = rpc * N_CLASSES
    bfull, bfirst = (jnp.asarray(t) for t in _prompt_bias_tables())
    ng = len(DIL_GROUPS)
    blk = lambda: pl.BlockSpec((1, N_CLASSES, rpc, LANES), lambda i, p: (i, 0, 0, p))
    est = 8 * s * LANES * 4 + 6 * s * LANES * 4 + 2 * ng * 2 * BAND * 3 * BAND * 4
    return pl.pallas_call(
        functools.partial(_attn_prompt_kernel, rows_per_class=rpc),
        grid=(b, MIX_W // LANES),
        in_specs=[blk(), blk(), blk(),
                  pl.BlockSpec((ng, 2, BAND, 2 * BAND), lambda i, p: (0, p, 0, 0)),
                  pl.BlockSpec((ng, 2, BAND, 2 * BAND), lambda i, p: (0, p, 0, 0))],
        out_specs=pl.BlockSpec((1, s, LANES), lambda i, p: (i, 0, p)),
        out_shape=jax.ShapeDtypeStruct((b, s, MIX_W), F32),
        scratch_shapes=[pltpu.VMEM((ng, N_CLASSES, rpc, LANES), F32) for _ in range(2)],
        compiler_params=_params(("parallel", "parallel"), est),
        name="attn_prompt",
    )(q, k, v, bfull, bfirst)


def _sample_bias_tables(wb, t_new):
    slopes = _alibi_slopes()

    def table(dist):
        mult = np.zeros(dist.shape)
        for window, dil in DIL_GROUPS:
            mult += (dist >= 0) & (dist <= window) & (dist % dil == 0)
        logm = np.where(mult > 0, np.log(np.maximum(mult, 1.0)), NEG)
        tab = -slopes[:, None, None] * dist[None].astype(np.float64) + logm[None]
        tab = np.where((mult > 0)[None], tab, NEG)
        return tab.reshape(N_HEADS * t_new, dist.shape[1]).astype(np.float32)

    t = np.arange(t_new)[:, None]
    return table(wb + t - np.arange(wb)[None, :]), table(t - np.arange(t_new)[None, :])


def _attn_sample_kernel(q_ref, kn_ref, vn_ref, wk_ref, wv_ref, bo_ref, bn_ref,
                        att_ref, ok_ref, ov_ref, *, wb, t_new):
    rows_q = N_HEADS * t_new
    q = q_ref[0] * (HEAD_DIM ** -0.5)
    qm = jnp.concatenate([q] * N_HEADS, axis=0)
    row_head = lax.broadcasted_iota(jnp.int32, (rows_q, MIX_W), 0) // t_new
    lane_head = lax.broadcasted_iota(jnp.int32, (rows_q, MIX_W), 1) // HEAD_DIM
    own = row_head == lane_head
    qm = jnp.where(own, qm, 0.0).astype(BF16)
    k_new, v_new = kn_ref[0], vn_ref[0]
    wk, wv = wk_ref[0], wv_ref[0]
    s_old = _dot(qm, wk.astype(BF16)) + bo_ref[...]
    s_new = _dot_nt(qm, k_new.astype(BF16)) + bn_ref[...]
    m = jnp.maximum(jnp.max(s_old, axis=-1, keepdims=True), jnp.max(s_new, axis=-1, keepdims=True))
    p_old = jnp.exp(s_old - m)
    p_new = jnp.exp(s_new - m)
    inv_l = 1.0 / (jnp.sum(p_old, axis=-1, keepdims=True) + jnp.sum(p_new, axis=-1, keepdims=True))
    o = _dot_nt(p_old.astype(BF16), wv.astype(BF16)) + _dot(p_new.astype(BF16), v_new.astype(BF16))
    o = jnp.where(own, o * inv_l, 0.0)
    att = o[0:t_new]
    for h in range(1, N_HEADS):
        att = att + o[h * t_new:(h + 1) * t_new]
    att_ref[0] = att
    lane = lax.broadcasted_iota(jnp.int32, (MIX_W, LANES), 1)
    is_new = lane >= LANES - t_new
    pad = jnp.zeros((LANES - t_new, MIX_W), F32)
    for w, new, o_ref in ((wk, k_new, ok_ref), (wv, v_new, ov_ref)):
        rolled = pltpu.roll(w, wb - t_new, axis=1)
        o_ref[0] = rolled
        new_t = jnp.concatenate([pad, new], axis=0).T
        o_ref[0, :, wb - LANES:wb] = jnp.where(is_new, new_t, rolled[:, wb - LANES:wb])


def _attn_sample(q, k_new, v_new, win_k_t, win_v_t):
    b, t_new, _ = q.shape
    wb = win_k_t.shape[2]
    assert wb >= DIL_GROUPS[-1][0] and t_new % SUBLANES == 0 and wb % LANES == 0
    b_old, b_new = _sample_bias_tables(wb, t_new)
    small = lambda: pl.BlockSpec((1, t_new, MIX_W), lambda i: (i, 0, 0))
    big = lambda: pl.BlockSpec((1, MIX_W, wb), lambda i: (i, 0, 0))
    est = 8 * wb * MIX_W * 4 + 2 * wb * MIX_W * 2 + 4 * N_HEADS * t_new * wb * 4
    return pl.pallas_call(
        functools.partial(_attn_sample_kernel, wb=wb, t_new=t_new),
        grid=(b,),
        in_specs=[small(), small(), small(), big(), big(),
                  pl.BlockSpec((N_HEADS * t_new, wb), lambda i: (0, 0)),
                  pl.BlockSpec((N_HEADS * t_new, t_new), lambda i: (0, 0))],
        out_specs=[small(), big(), big()],
        out_shape=[jax.ShapeDtypeStruct((b, t_new, MIX_W), F32),
                   jax.ShapeDtypeStruct((b, MIX_W, wb), F32),
                   jax.ShapeDtypeStruct((b, MIX_W, wb), F32)],
        compiler_params=_params(("parallel",), est),
        name="attn_sample",
    )(q, k_new, v_new, win_k_t, win_v_t, jnp.asarray(b_old), jnp.asarray(b_new))


ST_ROWS = HEAD_DIM + SUBLANES


def _mlstm_seq_kernel(qt_ref, k_ref, vt_ref, gcol_ref, mirow_ref, mfrow_ref, bcol_ref, bigrow_ref, bfgrow_ref,
                      h_ref, st_ref, m_ref, *, chunk):
    L = chunk
    hp = lax.Precision.HIGHEST

    @pl.when(pl.program_id(1) == 0)
    def _():
        st_ref[...] = jnp.zeros(st_ref.shape, F32)
        m_ref[...] = jnp.full(m_ref.shape, NEG, F32)

    ss = lax.broadcasted_iota(jnp.int32, (L, L), 0)
    tt = lax.broadcasted_iota(jnp.int32, (L, L), 1)
    causal = ss <= tt
    ig_r = mirow_ref[0] + bigrow_ref[...]
    lf_r = _log_sigmoid(mfrow_ref[0] + bfgrow_ref[...])
    b_r = jnp.dot(lf_r, causal.astype(F32), precision=hp, preferred_element_type=F32)
    lane = lax.broadcasted_iota(jnp.int32, (N_HEADS, L), 1)
    cm = ig_r - b_r
    shift = 1
    while shift < L:
        cm = jnp.maximum(cm, jnp.where(lane >= shift, pltpu.roll(cm, shift, axis=1), NEG))
        shift *= 2
    m_prev = m_ref[0][:, 0:1]
    m_t = b_r + jnp.maximum(m_prev, cm)
    row_all = b_r - m_t
    inter_w = jnp.exp(b_r + m_prev - m_t)
    floor = jnp.exp(-m_t)
    m_new = m_t[:, L - 1:L]
    b_last = b_r[:, L - 1:L]
    w_s = jnp.exp(b_last - b_r + ig_r - m_new)
    decay = jnp.exp(b_last + m_prev - m_new)
    pre_c = gcol_ref[0] + bcol_ref[...]
    bcum_c = jnp.dot((ss >= tt).astype(F32), _log_sigmoid(pre_c), precision=hp, preferred_element_type=F32)
    col_all = pre_c - pltpu.roll(bcum_c, GATE_PAD - N_HEADS, axis=1)

    ones_row = (lax.broadcasted_iota(jnp.int32, (SUBLANES, L), 0) == 0).astype(F32)
    zeros_q = jnp.zeros((HEAD_DIM, L), BF16)
    h_parts, new_state = [], []
    for h in range(N_HEADS):
        pair, half = divmod(h, 2)
        rows = slice(h * HEAD_DIM, (h + 1) * HEAD_DIM)
        q_own = qt_ref[0, rows, :].astype(BF16)
        q_pad = jnp.concatenate([q_own, zeros_q] if half == 0 else [zeros_q, q_own], axis=0)
        k_pair = k_ref[0, :, pair * LANES:(pair + 1) * LANES].astype(BF16)
        v_ext = jnp.concatenate([vt_ref[0, rows, :], ones_row], axis=0)
        st = st_ref[0, h]
        d = jnp.exp(jnp.where(causal, col_all[:, h:h + 1] + row_all[h:h + 1, :], NEG))
        a = (_dot(k_pair, q_pad) * d).astype(BF16)
        tot = _dot(v_ext.astype(BF16), a) + inter_w[h:h + 1, :] * _dot(st.astype(BF16), q_pad)
        den = tot[HEAD_DIM:HEAD_DIM + 1, :]
        h_parts.append(tot[0:HEAD_DIM, :] * (1.0 / jnp.maximum(jnp.abs(den), floor[h:h + 1, :])))
        new_state.append(decay[h:h + 1, :] * st + _dot((v_ext * w_s[h:h + 1, :]).astype(BF16), k_pair))
    h_ref[0] = jnp.concatenate(h_parts, axis=0).T
    for h, st_new in enumerate(new_state):
        st_ref[0, h] = st_new
    m_ref[0] = jnp.broadcast_to(m_new, (N_HEADS, LANES))


def _mlstm_seq(q_t, k, v_t, gates, b_ig, b_fg, chunk):
    b, s, _ = k.shape
    assert s % chunk == 0 and chunk % LANES == 0
    g_rows = jnp.swapaxes(gates[:, :, :2 * N_HEADS], 1, 2)
    bcol = jnp.concatenate([b_ig, b_fg, jnp.zeros((GATE_PAD - 2 * N_HEADS,), F32)]).reshape(1, GATE_PAD)
    ft = lambda: pl.BlockSpec((1, MIX_W, chunk), lambda i, c: (i, 0, c))
    row = lambda: pl.BlockSpec((1, N_HEADS, chunk), lambda i, c: (i, 0, c))
    const = lambda r, w: pl.BlockSpec((r, w), lambda i, c: (0, 0))
    est = 8 * chunk * MIX_W * 4 + 12 * chunk * chunk * 4 + 4 * N_HEADS * ST_ROWS * LANES * 4
    h, st, m = pl.pallas_call(
        functools.partial(_mlstm_seq_kernel, chunk=chunk),
        grid=(b, s // chunk),
        in_specs=[ft(), pl.BlockSpec((1, chunk, MIX_W), lambda i, c: (i, c, 0)), ft(),
                  pl.BlockSpec((1, chunk, GATE_PAD), lambda i, c: (i, c, 0)), row(), row(),
                  const(1, GATE_PAD), const(N_HEADS, 1), const(N_HEADS, 1)],
        out_specs=[pl.BlockSpec((1, chunk, MIX_W), lambda i, c: (i, c, 0)),
                   pl.BlockSpec((1, N_HEADS, ST_ROWS, LANES), lambda i, c: (i, 0, 0, 0)),
                   pl.BlockSpec((1, N_HEADS, LANES), lambda i, c: (i, 0, 0))],
        out_shape=[jax.ShapeDtypeStruct((b, s, MIX_W), F32),
                   jax.ShapeDtypeStruct((b, N_HEADS, ST_ROWS, LANES), F32),
                   jax.ShapeDtypeStruct((b, N_HEADS, LANES), F32)],
        compiler_params=_params(("arbitrary", "arbitrary"), est),
        name="mlstm_seq",
    )(q_t, k, v_t, gates, g_rows[:, :N_HEADS], g_rows[:, N_HEADS:], bcol, b_ig.reshape(-1, 1), b_fg.reshape(-1, 1))
    st = st.reshape(b, N_HEADS // 2, 2, ST_ROWS, 2, HEAD_DIM)
    st = jnp.stack([st[:, :, 0, :, 0, :], st[:, :, 1, :, 1, :]], axis=2).reshape(b, N_HEADS, ST_ROWS, HEAD_DIM)
    c = jnp.swapaxes(st[:, :, :HEAD_DIM, :], 2, 3)
    return h, c, st[:, :, HEAD_DIM, :], m[:, :, 0]


def _mlstm_step_kernel(xq_ref, xk_ref, v_ref, g_ref, sq_ref, sk_ref, cwq_ref, cwk_ref, cbq_ref, cbk_ref, gb_ref,
                       c0_ref, n0_ref, m0_ref, h_ref, c_ref, n_ref, m_ref, q_scr, ik_scr, *, t_new, batch):
    hist = CONV_W - 1
    tile = lambda ref, t: ref[pl.ds(t, batch, stride=t_new), :].T
    xq = [sq_ref[t].T for t in range(hist)] + [tile(xq_ref, t) for t in range(t_new)]
    xk = [sk_ref[t].T for t in range(hist)] + [tile(xk_ref, t) for t in range(t_new)]
    c_ref[...] = c0_ref[...]
    n = [n0_ref[e] for e in range(2)]
    m = [m0_ref[e] for e in range(2)]
    for t in range(t_new):
        cq, ck = cbq_ref[...], cbk_ref[...]
        for j in range(CONV_W):
            cq = cq + xq[t + j] * cwq_ref[j]
            ck = ck + xk[t + j] * cwk_ref[j]
        q_t = cq * _sigmoid(cq)
        k_t = ck * _sigmoid(ck) * (HEAD_DIM ** -0.5)
        v_t = tile(v_ref, t)
        gates = tile(g_ref, t)[0:2 * N_HEADS, :] + gb_ref[...]
        h_pair = []
        for e in range(2):
            head = 2 * pl.program_id(0) + e
            rows = slice(e * HEAD_DIM, (e + 1) * HEAD_DIM)
            ig = _gates_row(gates, head)
            lf = _log_sigmoid(_gates_row(gates, N_HEADS + head))
            m_new = jnp.maximum(lf + m[e], ig)
            f = jnp.exp(lf + m[e] - m_new)
            i = jnp.exp(ig - m_new)
            q_e, k_e, v_e = q_t[rows], k_t[rows], v_t[rows]
            q_scr[...] = q_e
            ik_scr[...] = k_e * i

            def body(kk, num, e=e, f=f, v_e=v_e):
                c_new = f * c_ref[e, kk] + ik_scr[pl.ds(kk, 1), :] * v_e
                c_ref[e, kk] = c_new
                return num + c_new * q_scr[pl.ds(kk, 1), :]
            num = lax.fori_loop(0, HEAD_DIM, body, jnp.zeros((HEAD_DIM, batch), F32), unroll=8)
            n[e] = f * n[e] + k_e * i
            den = jnp.sum(n[e] * q_e, axis=0, keepdims=True)
            h_pair.append(num * (1.0 / jnp.maximum(jnp.abs(den), jnp.exp(-m_new))))
            m[e] = m_new
        h_ref[pl.ds(t, batch, stride=t_new), :] = jnp.concatenate(h_pair, axis=0).T
    for e in range(2):
        n_ref[e] = n[e]
        m_ref[e] = m[e]


def _gates_row(gates, idx):
    sel = lax.broadcasted_iota(jnp.int32, gates.shape, 0) == idx
    return jnp.sum(jnp.where(sel, gates, 0.0), axis=0, keepdims=True)


def _mlstm_step(qk_pre, conv_state, conv_w, conv_b, mv, gates, b_ig, b_fg, c0, n0, m0, t_new):
    rows = qk_pre.shape[0]
    batch = rows // t_new
    assert batch == LANES and t_new % SUBLANES == 0
    hist = CONV_W - 1
    lanes_b = lambda a: jnp.broadcast_to(a[..., None], a.shape + (batch,))
    sc = jnp.swapaxes(conv_state, 0, 1)
    cw, cb = lanes_b(conv_w), lanes_b(conv_b)
    gb = lanes_b(jnp.concatenate([b_ig, b_fg]))
    c_t = jnp.transpose(c0, (1, 2, 3, 0))
    n_t = jnp.transpose(n0, (1, 2, 0))
    m_t = jnp.transpose(m0, (1, 0)).reshape(N_HEADS, 1, batch)
    npair = N_HEADS // 2
    colq = lambda: pl.BlockSpec((rows, LANES), lambda p: (0, p))
    colk = lambda: pl.BlockSpec((rows, LANES), lambda p: (0, npair + p))
    est = (8 * rows * LANES * 4 + 4 * 2 * HEAD_DIM * HEAD_DIM * batch * 4 + 2 * (hist + 2 * CONV_W + 2) * LANES * batch * 4
           + 4 * (hist + t_new) * LANES * batch * 4)
    h, c, n, m = pl.pallas_call(
        functools.partial(_mlstm_step_kernel, t_new=t_new, batch=batch),
        grid=(npair,),
        in_specs=[colq(), colk(), colq(), pl.BlockSpec((rows, GATE_PAD), lambda p: (0, 0)),
                  pl.BlockSpec((hist, batch, LANES), lambda p: (0, 0, p)),
                  pl.BlockSpec((hist, batch, LANES), lambda p: (0, 0, npair + p)),
                  pl.BlockSpec((CONV_W, LANES, batch), lambda p: (0, p, 0)),
                  pl.BlockSpec((CONV_W, LANES, batch), lambda p: (0, npair + p, 0)),
                  pl.BlockSpec((LANES, batch), lambda p: (p, 0)),
                  pl.BlockSpec((LANES, batch), lambda p: (npair + p, 0)),
                  pl.BlockSpec((2 * N_HEADS, batch), lambda p: (0, 0)),
                  pl.BlockSpec((2, HEAD_DIM, HEAD_DIM, batch), lambda p: (p, 0, 0, 0)),
                  pl.BlockSpec((2, HEAD_DIM, batch), lambda p: (p, 0, 0)),
                  pl.BlockSpec((2, 1, batch), lambda p: (p, 0, 0))],
        out_specs=[colq(),
                   pl.BlockSpec((2, HEAD_DIM, HEAD_DIM, batch), lambda p: (p, 0, 0, 0)),
                   pl.BlockSpec((2, HEAD_DIM, batch), lambda p: (p, 0, 0)),
                   pl.BlockSpec((2, 1, batch), lambda p: (p, 0, 0))],
        out_shape=[jax.ShapeDtypeStruct((rows, MIX_W), F32),
                   jax.ShapeDtypeStruct(c_t.shape, F32), jax.ShapeDtypeStruct(n_t.shape, F32),
                   jax.ShapeDtypeStruct(m_t.shape, F32)],
        scratch_shapes=[pltpu.VMEM((HEAD_DIM, batch), F32), pltpu.VMEM((HEAD_DIM, batch), F32)],
        compiler_params=_params(("arbitrary",), est),
        name="mlstm_step",
    )(qk_pre, qk_pre, mv, gates, sc, sc, cw, cw, cb, cb, gb, c_t, n_t, m_t)
    return (h, jnp.transpose(c, (3, 0, 1, 2)), jnp.transpose(n, (2, 0, 1)),
            jnp.transpose(m.reshape(N_HEADS, batch), (1, 0)))


def _mixers_to_output(x2, att, mlh, mo, mem_k, mem_v, wts, bt_x, tq_x, tm):
    b = mem_k.shape[0]
    mk, mv_ = _mem_tile_view(mem_k), _mem_tile_view(mem_v)
    h1 = _merge(att, mlh, mo, wts["b_og"], wts["g_attn_out"], wts["g_mlstm_out"], wts["w_out"], x2, tm)
    if tq_x >= 2 * LANES:
        h2 = _xattn_block(h1.reshape(b, -1, D_MODEL), wts["g_cross"], wts["w_cq"], mk, mv_, wts["w_co"],
                          tq_x).reshape(-1, D_MODEL)
    else:
        (xq,) = _norm_matmul(h1, wts["g_cross"], wts["w_cq"], (D_MODEL,), tm)
        xo = _xattn(xq.reshape(b, -1, D_MODEL), mk, mv_, bt_x, tq_x)
        h2 = _matmul_res(xo.reshape(-1, D_MODEL), wts["w_co"], h1, tm)
    return _ffn(h2, wts["g_ffn"], wts["w_gate"], wts["w_up"], wts["w_down"], wts["g_final"], tm)


def kernel(x_prompt, x_sample, mem_prompt, cache_win_k, cache_win_v, cache_mem_k, cache_mem_v, state_conv, state_mlstm_C, state_mlstm_n, state_mlstm_m, g_mix, w_in, conv_w, conv_b, b_ig, b_fg, b_og, g_attn_out, g_mlstm_out, w_out, g_cross, g_mem, w_cq, w_ck, w_cv, w_co, g_ffn, w_gate, w_up, w_down, g_final):
    bp, sp, _ = x_prompt.shape
    bs, ts, _ = x_sample.shape
    wts = dict(b_og=b_og, g_attn_out=g_attn_out, g_mlstm_out=g_mlstm_out, g_cross=g_cross, g_ffn=g_ffn,
               g_final=g_final, w_out=w_out.astype(BF16), w_cq=w_cq.astype(BF16), w_co=w_co.astype(BF16),
               w_gate=w_gate.astype(BF16), w_up=w_up.astype(BF16), w_down=w_down.astype(BF16))
    w_in_p = jnp.pad(w_in, ((0, 0), (0, sum(IN_WIDTHS) - N_IN))).astype(BF16)
    w_ckv = jnp.concatenate([w_ck, w_cv], axis=1).astype(BF16)
    tm = 512

    xp2 = x_prompt.reshape(bp * sp, D_MODEL)
    aq_c, ak_c, av_c, ak_t, av_t, mq_t, mk, mv_t, mo, gates, conv_tail = _inproj_prompt(
        xp2, g_mix, w_in_p, conv_w, conv_b, tm, sp)
    r3 = lambda a: a.reshape(bp, sp, -1)
    att = _attn_prompt(aq_c, ak_c, av_c)
    keep = min(DIL_GROUPS[-1][0], sp)
    from_t = lambda a: jnp.transpose(a.reshape(a.shape[0], N_HEADS, HEAD_DIM, a.shape[2]), (0, 3, 1, 2))
    win_k_p = from_t(ak_t)[:, sp - keep:]
    win_v_p = from_t(av_t)[:, sp - keep:]
    mlh, c_p, n_p, m_p = _mlstm_seq(mq_t, r3(mk), mv_t, r3(gates), b_ig, b_fg, chunk=256)
    conv_p = conv_tail[:, SUBLANES - (CONV_W - 1):]
    mem_k_p, mem_v_p = _norm_matmul(mem_prompt.reshape(bp * N_MEM, D_MODEL), g_mem, w_ckv, (D_MODEL, D_MODEL), tm)
    mem_k_p = mem_k_p.reshape(bp, N_MEM, N_X_HEADS, X_HEAD_DIM)
    mem_v_p = mem_v_p.reshape(bp, N_MEM, N_X_HEADS, X_HEAD_DIM)
    y_p = _mixers_to_output(xp2, att.reshape(-1, MIX_W), mlh.reshape(-1, MIX_W), mo, mem_k_p, mem_v_p, wts,
                            bt_x=1, tq_x=512, tm=tm)

    xs2 = x_sample.reshape(bs * ts, D_MODEL)
    aq, ak, av, qk_pre, mv, mo, gates = _norm_matmul(xs2, g_mix, w_in_p, IN_WIDTHS, tm)
    r3 = lambda a: a.reshape(bs, ts, -1)
    wb = cache_win_k.shape[1]
    to_t = lambda a: jnp.transpose(a, (0, 2, 3, 1)).reshape(a.shape[0], MIX_W, a.shape[1])
    att, win_k_s, win_v_s = _attn_sample(r3(aq), r3(ak), r3(av), to_t(cache_win_k), to_t(cache_win_v))
    mlh, c_s, n_s, m_s = _mlstm_step(qk_pre, state_conv, conv_w, conv_b, mv, gates, b_ig, b_fg,
                                     state_mlstm_C, state_mlstm_n, state_mlstm_m, ts)
    conv_s = jnp.concatenate([state_conv, r3(qk_pre)], axis=1)[:, -(CONV_W - 1):]
    y_s = _mixers_to_output(xs2, att.reshape(-1, MIX_W), mlh, mo, cache_mem_k, cache_mem_v, wts,
                            bt_x=4, tq_x=ts, tm=tm)

    return (y_p.reshape(bp, sp, D_MODEL), y_s.reshape(bs, ts, D_MODEL),
            win_k_p, win_v_p, conv_p, c_p, n_p, m_p, mem_k_p, mem_v_p,
            from_t(win_k_s), from_t(win_v_s), conv_s, c_s, n_s, m_s)
```

```python
import functools

import numpy as np
import jax
import jax.numpy as jnp
from jax import lax
from jax.experimental import pallas as pl
from jax.experimental.pallas import tpu as pltpu

F32 = jnp.float32
BF16 = jnp.bfloat16

D_MODEL = 1024
HEAD_DIM = 64
N_HEADS = 8
MIX_W = N_HEADS * HEAD_DIM
DIL_GROUPS = ((128, 1), (512, 4), (2048, 16))
BAND = 128
CONV_W = 4
N_MEM = 256
N_X_HEADS = 4
X_HEAD_DIM = D_MODEL // N_X_HEADS
D_FF = 2816
RMS_EPS = 1e-6
NEG = -1e30
N_IN = 3 * MIX_W + 2 * MIX_W + MIX_W + MIX_W + 2 * N_HEADS
GATE_PAD = 128

LANES = 128
SUBLANES = 8
VMEM_BYTES_V7X = 64 * 1024 * 1024


def _vmem_limit(nbytes):
    return int(min(max(2 * nbytes, 16 * 1024 * 1024), VMEM_BYTES_V7X - 8 * 1024 * 1024))


def _params(semantics, vmem_estimate):
    return pltpu.CompilerParams(dimension_semantics=semantics,
                                vmem_limit_bytes=_vmem_limit(vmem_estimate))


def _rms(x, g):
    return x * lax.rsqrt(jnp.mean(x * x, axis=-1, keepdims=True) + RMS_EPS) * g


def _sigmoid(x):
    return 1.0 / (1.0 + jnp.exp(-x))


def _log_sigmoid(x):
    return jnp.minimum(x, 0.0) - jnp.log(1.0 + jnp.exp(-jnp.abs(x)))


def _dot(a, b):
    return jnp.dot(a, b, preferred_element_type=F32)


def _dot_nt(a, b):
    return lax.dot_general(a, b, (((1,), (1,)), ((), ())), preferred_element_type=F32)


N_CLASSES = max(d for _, d in DIL_GROUPS)
IN_WIDTHS = (MIX_W, MIX_W, MIX_W, 2 * MIX_W, MIX_W, MIX_W, GATE_PAD)
IN_OFFS = tuple(int(o) for o in np.cumsum((0,) + IN_WIDTHS))


def _norm_matmul_kernel(x_ref, g_ref, w_ref, *o_refs, splits):
    xn = _rms(x_ref[...], g_ref[...]).astype(BF16)
    for o_ref, (a, b) in zip(o_refs, splits):
        o_ref[...] = _dot(xn, w_ref[:, a:b])


def _norm_matmul(x, g, w, widths, tm):
    m, k = x.shape
    n = w.shape[1]
    assert sum(widths) == n and m % tm == 0
    offs = np.cumsum([0] + list(widths))
    splits = tuple((int(offs[i]), int(offs[i + 1])) for i in range(len(widths)))
    est = 2 * tm * k * 4 + 2 * k * n * 2 + 2 * tm * n * 4
    return pl.pallas_call(
        functools.partial(_norm_matmul_kernel, splits=splits),
        grid=(m // tm,),
        in_specs=[pl.BlockSpec((tm, k), lambda i: (i, 0)),
                  pl.BlockSpec((1, k), lambda i: (0, 0)),
                  pl.BlockSpec((k, n), lambda i: (0, 0))],
        out_specs=[pl.BlockSpec((tm, wd), lambda i: (i, 0)) for wd in widths],
        out_shape=[jax.ShapeDtypeStruct((m, wd), F32) for wd in widths],
        compiler_params=_params(("parallel",), est),
        name="norm_matmul",
    )(x, g.reshape(1, k), w)


def _inproj_prompt_kernel(x_ref, g_ref, w_ref, wvt_ref, cw_ref, cb_ref,
                          qc_ref, kc_ref, vc_ref, kt_ref, vt_ref, mq_ref, mk_ref, mvt_ref, mo_ref, gate_ref, tail_ref,
                          cls_scr, xp_scr, *, per_seq):
    tm = x_ref.shape[0]

    @pl.when(pl.program_id(0) % per_seq == 0)
    def _():
        xp_scr[0:SUBLANES, :] = jnp.zeros((SUBLANES, 2 * MIX_W), F32)

    xn = _rms(x_ref[...], g_ref[...]).astype(BF16)
    proj = lambda idx: _dot(xn, w_ref[:, IN_OFFS[idx]:IN_OFFS[idx + 1]])

    xp_scr[SUBLANES:SUBLANES + tm, :] = proj(3)
    conv = cb_ref[...]
    for j in range(CONV_W):
        off = SUBLANES - (CONV_W - 1) + j
        conv = conv + xp_scr[off:off + tm, :] * cw_ref[j:j + 1, :]
    tail = xp_scr[tm:tm + SUBLANES, :]
    xp_scr[0:SUBLANES, :] = tail
    tail_ref[0] = tail
    act = conv * _sigmoid(conv)
    mq_ref[0] = act[:, :MIX_W].T
    mk_ref[...] = act[:, MIX_W:] * (HEAD_DIM ** -0.5)

    for idx, c_ref, t_ref in ((0, qc_ref, None), (1, kc_ref, kt_ref), (2, vc_ref, vt_ref)):
        y = proj(idx)
        if t_ref is not None:
            t_ref[0] = y.T
        scr = cls_scr.at[idx]
        for lt in range(MIX_W // LANES):
            scr[lt] = y[:, lt * LANES:(lt + 1) * LANES]
        for c in range(N_CLASSES):
            for lt in range(MIX_W // LANES):
                c_ref[0, c, :, lt * LANES:(lt + 1) * LANES] = scr[lt, pl.ds(c, tm // N_CLASSES, stride=N_CLASSES), :]

    mvt_ref[0] = _dot_nt(wvt_ref[...], xn)
    mo_ref[...] = proj(5)
    gate_ref[...] = proj(6)


def _inproj_prompt(x, g, w, conv_w, conv_b, tm, seq):
    m, k = x.shape
    n = w.shape[1]
    assert n == IN_OFFS[-1] and m % seq == 0 and seq % tm == 0 and tm % (N_CLASSES * SUBLANES) == 0
    b, per_seq = m // seq, seq // tm
    est = (2 * tm * k * 4 + 2 * k * n * 2 + 2 * tm * (n + 4 * MIX_W) * 4 + 3 * tm * MIX_W * 4
           + 3 * (tm + SUBLANES) * 2 * MIX_W * 4)
    cls_spec = lambda: pl.BlockSpec((1, N_CLASSES, tm // N_CLASSES, MIX_W), lambda i: (i // per_seq, 0, i % per_seq, 0))
    cls_shape = jax.ShapeDtypeStruct((b, N_CLASSES, seq // N_CLASSES, MIX_W), F32)
    t_spec = lambda: pl.BlockSpec((1, MIX_W, tm), lambda i: (i // per_seq, 0, i % per_seq))
    t_shape = jax.ShapeDtypeStruct((b, MIX_W, seq), F32)
    nat_spec = lambda wd: pl.BlockSpec((tm, wd), lambda i: (i, 0))
    nat_shape = lambda wd: jax.ShapeDtypeStruct((m, wd), F32)
    cw = jnp.pad(conv_w, ((0, SUBLANES - CONV_W), (0, 0)))
    return pl.pallas_call(
        functools.partial(_inproj_prompt_kernel, per_seq=per_seq),
        grid=(m // tm,),
        in_specs=[pl.BlockSpec((tm, k), lambda i: (i, 0)),
                  pl.BlockSpec((1, k), lambda i: (0, 0)),
                  pl.BlockSpec((k, n), lambda i: (0, 0)),
                  pl.BlockSpec((MIX_W, k), lambda i: (0, 0)),
                  pl.BlockSpec((SUBLANES, 2 * MIX_W), lambda i: (0, 0)),
                  pl.BlockSpec((1, 2 * MIX_W), lambda i: (0, 0))],
        out_specs=[cls_spec(), cls_spec(), cls_spec(), t_spec(), t_spec(), t_spec(), nat_spec(MIX_W), t_spec(),
                   nat_spec(MIX_W), nat_spec(GATE_PAD),
                   pl.BlockSpec((1, SUBLANES, 2 * MIX_W), lambda i: (i // per_seq, 0, 0))],
        out_shape=[cls_shape, cls_shape, cls_shape, t_shape, t_shape, t_shape, nat_shape(MIX_W), t_shape,
                   nat_shape(MIX_W), nat_shape(GATE_PAD),
                   jax.ShapeDtypeStruct((b, SUBLANES, 2 * MIX_W), F32)],
        scratch_shapes=[pltpu.VMEM((3, MIX_W // LANES, tm, LANES), F32),
                        pltpu.VMEM((tm + SUBLANES, 2 * MIX_W), F32)],
        compiler_params=_params(("arbitrary",), est),
        name="inproj_prompt",
    )(x, g.reshape(1, k), w, w[:, IN_OFFS[4]:IN_OFFS[5]].T, cw, conv_b.reshape(1, -1))


def _matmul_res_kernel(x_ref, w_ref, r_ref, o_ref):
    o_ref[...] = r_ref[...] + _dot(x_ref[...].astype(BF16), w_ref[...])


def _matmul_res(x, w, res, tm):
    m, k = x.shape
    n = w.shape[1]
    est = 2 * tm * k * 4 + 2 * k * n * 2 + 4 * tm * n * 4
    return pl.pallas_call(
        _matmul_res_kernel,
        grid=(m // tm,),
        in_specs=[pl.BlockSpec((tm, k), lambda i: (i, 0)),
                  pl.BlockSpec((k, n), lambda i: (0, 0)),
                  pl.BlockSpec((tm, n), lambda i: (i, 0))],
        out_specs=pl.BlockSpec((tm, n), lambda i: (i, 0)),
        out_shape=jax.ShapeDtypeStruct((m, n), F32),
        compiler_params=_params(("parallel",), est),
        name="matmul_res",
    )(x, w, res)


def _merge_kernel(att_ref, mlh_ref, mo_ref, bog_ref, ga_ref, gm_ref, w_ref, x_ref, o_ref):
    a_n = _rms(att_ref[...], ga_ref[...]).astype(BF16)
    ml = _sigmoid(mo_ref[...] + bog_ref[...]) * mlh_ref[...]
    m_n = _rms(ml, gm_ref[...]).astype(BF16)
    o_ref[...] = x_ref[...] + _dot(a_n, w_ref[0:MIX_W, :]) + _dot(m_n, w_ref[MIX_W:2 * MIX_W, :])


def _merge(att, mlh, mo, b_og, g_a, g_m, w_out, x, tm):
    m = x.shape[0]
    half = lambda: pl.BlockSpec((tm, MIX_W), lambda i: (i, 0))
    vec = lambda: pl.BlockSpec((1, MIX_W), lambda i: (0, 0))
    est = 6 * tm * MIX_W * 4 + 2 * 2 * MIX_W * D_MODEL * 2 + 4 * tm * D_MODEL * 4
    return pl.pallas_call(
        _merge_kernel,
        grid=(m // tm,),
        in_specs=[half(), half(), half(), vec(), vec(), vec(),
                  pl.BlockSpec((2 * MIX_W, D_MODEL), lambda i: (0, 0)),
                  pl.BlockSpec((tm, D_MODEL), lambda i: (i, 0))],
        out_specs=pl.BlockSpec((tm, D_MODEL), lambda i: (i, 0)),
        out_shape=jax.ShapeDtypeStruct((m, D_MODEL), F32),
        compiler_params=_params(("parallel",), est),
        name="merge",
    )(att, mlh, mo, b_og.reshape(1, -1), g_a.reshape(1, -1), g_m.reshape(1, -1), w_out, x)


FF_CHUNK = D_FF // 2


def _ffn_kernel(h_ref, gf_ref, wg_ref, wu_ref, wd_ref, gl_ref, o_ref):
    h = h_ref[...]
    hn = _rms(h, gf_ref[...]).astype(BF16)
    acc = h
    for c in range(D_FF // FF_CHUNK):
        sl = slice(c * FF_CHUNK, (c + 1) * FF_CHUNK)
        gate = _dot(hn, wg_ref[:, sl])
        up = _dot(hn, wu_ref[:, sl])
        act = (gate * _sigmoid(gate) * up).astype(BF16)
        acc = acc + _dot(act, wd_ref[sl, :])
    o_ref[...] = _rms(acc, gl_ref[...])


def _ffn(h, g_ffn, w_gate, w_up, w_down, g_final, tm):
    m = h.shape[0]
    est = 4 * tm * D_MODEL * 4 + 2 * 3 * D_MODEL * D_FF * 2 + 3 * tm * FF_CHUNK * 4
    return pl.pallas_call(
        _ffn_kernel,
        grid=(m // tm,),
        in_specs=[pl.BlockSpec((tm, D_MODEL), lambda i: (i, 0)),
                  pl.BlockSpec((1, D_MODEL), lambda i: (0, 0)),
                  pl.BlockSpec((D_MODEL, D_FF), lambda i: (0, 0)),
                  pl.BlockSpec((D_MODEL, D_FF), lambda i: (0, 0)),
                  pl.BlockSpec((D_FF, D_MODEL), lambda i: (0, 0)),
                  pl.BlockSpec((1, D_MODEL), lambda i: (0, 0))],
        out_specs=pl.BlockSpec((tm, D_MODEL), lambda i: (i, 0)),
        out_shape=jax.ShapeDtypeStruct((m, D_MODEL), F32),
        compiler_params=_params(("parallel",), est),
        name="ffn",
    )(h, g_ffn.reshape(1, -1), w_gate, w_up, w_down, g_final.reshape(1, -1))


X_HALVES = X_HEAD_DIM // LANES
X_ROWS = N_X_HEADS * X_HALVES


def _mem_tile_view(a):
    b = a.shape[0]
    a = a.reshape(b, N_MEM, N_X_HEADS, X_HALVES, LANES)
    return jnp.transpose(a, (0, 1, 3, 2, 4)).reshape(b, N_MEM * X_ROWS, LANES)


def _mem_tile_unview(a):
    b = a.shape[0]
    a = a.reshape(b, N_MEM, X_HALVES, N_X_HEADS, LANES)
    return jnp.transpose(a, (0, 1, 3, 2, 4)).reshape(b, N_MEM, N_X_HEADS, X_HEAD_DIM)


def _mem_proj_kernel(x_ref, g_ref, w_ref, k_ref, v_ref):
    xn = _rms(x_ref[0], g_ref[...]).astype(BF16)
    for idx, o_ref in enumerate((k_ref, v_ref)):
        y = _dot(xn, w_ref[:, idx * D_MODEL:(idx + 1) * D_MODEL])
        for h in range(N_X_HEADS):
            for j in range(X_HALVES):
                lanes = slice(h * X_HEAD_DIM + j * LANES, h * X_HEAD_DIM + (j + 1) * LANES)
                o_ref[0, pl.ds(j * N_X_HEADS + h, N_MEM, stride=X_ROWS), :] = y[:, lanes]


def _mem_proj(mem, g, w_ckv):
    b = mem.shape[0]
    view = jax.ShapeDtypeStruct((b, N_MEM * X_ROWS, LANES), F32)
    est = 2 * N_MEM * D_MODEL * 4 + 2 * D_MODEL * 2 * D_MODEL * 2 + 6 * N_MEM * D_MODEL * 4
    return pl.pallas_call(
        _mem_proj_kernel,
        grid=(b,),
        in_specs=[pl.BlockSpec((1, N_MEM, D_MODEL), lambda i: (i, 0, 0)),
                  pl.BlockSpec((1, D_MODEL), lambda i: (0, 0)),
                  pl.BlockSpec((D_MODEL, 2 * D_MODEL), lambda i: (0, 0))],
        out_specs=[pl.BlockSpec((1, N_MEM * X_ROWS, LANES), lambda i: (i, 0, 0)) for _ in range(2)],
        out_shape=[view, view],
        compiler_params=_params(("parallel",), est),
        name="mem_proj",
    )(mem, g.reshape(1, -1), w_ckv)


def _mem_head(ref, bi, h):
    halves = [ref[bi, pl.ds(j * N_X_HEADS + h, N_MEM, stride=X_ROWS), :] for j in range(X_HALVES)]
    return jnp.concatenate(halves, axis=1).astype(BF16)


def _xattn_kernel(q_ref, k_ref, v_ref, o_ref, *, bt):
    for bi in range(bt):
        for h in range(N_X_HEADS):
            sl = slice(h * X_HEAD_DIM, (h + 1) * X_HEAD_DIM)
            q = (q_ref[bi, :, sl] * (X_HEAD_DIM ** -0.5)).astype(BF16)
            s = _dot_nt(q, _mem_head(k_ref, bi, h))
            p = jnp.exp(s - jnp.max(s, axis=-1, keepdims=True))
            inv_l = 1.0 / jnp.sum(p, axis=-1, keepdims=True)
            o_ref[bi, :, sl] = _dot(p.astype(BF16), _mem_head(v_ref, bi, h)) * inv_l


def _xattn(q, mem_k, mem_v, bt, tq):
    b, t, _ = q.shape
    est = 4 * bt * tq * D_MODEL * 4 + 4 * bt * N_MEM * D_MODEL * 4
    kv = lambda: pl.BlockSpec((bt, N_MEM * X_ROWS, LANES), lambda i, j: (i, 0, 0))
    return pl.pallas_call(
        functools.partial(_xattn_kernel, bt=bt),
        grid=(b // bt, t // tq),
        in_specs=[pl.BlockSpec((bt, tq, D_MODEL), lambda i, j: (i, j, 0)), kv(), kv()],
        out_specs=pl.BlockSpec((bt, tq, D_MODEL), lambda i, j: (i, j, 0)),
        out_shape=jax.ShapeDtypeStruct((b, t, D_MODEL), F32),
        compiler_params=_params(("parallel", "parallel"), est),
        name="xattn",
    )(q, mem_k, mem_v)


def _xattn_block_kernel(att_ref, mlh_ref, mo_ref, x_ref, bog_ref, ga_ref, gm_ref, wout_ref,
                        g_ref, wq_ref, k_ref, v_ref, wo_ref, o_ref):
    a_n = _rms(att_ref[0], ga_ref[...]).astype(BF16)
    m_n = _rms(_sigmoid(mo_ref[0] + bog_ref[...]) * mlh_ref[0], gm_ref[...]).astype(BF16)
    h1 = x_ref[0] + _dot(a_n, wout_ref[0:MIX_W, :]) + _dot(m_n, wout_ref[MIX_W:2 * MIX_W, :])
    q_all = _dot(_rms(h1, g_ref[...]).astype(BF16), wq_ref[...]) * (X_HEAD_DIM ** -0.5)
    o_parts = []
    for h in range(N_X_HEADS):
        sl = slice(h * X_HEAD_DIM, (h + 1) * X_HEAD_DIM)
        s = _dot_nt(q_all[:, sl].astype(BF16), _mem_head(k_ref, 0, h))
        p = jnp.exp(s - jnp.max(s, axis=-1, keepdims=True))
        inv_l = 1.0 / jnp.sum(p, axis=-1, keepdims=True)
        o_parts.append((_dot(p.astype(BF16), _mem_head(v_ref, 0, h)) * inv_l).astype(BF16))
    o_ref[0] = h1 + _dot(jnp.concatenate(o_parts, axis=1), wo_ref[...])


def _xattn_block(att, mlh, mo, x, wts, mem_k, mem_v, tq):
    b, t, _ = x.shape
    est = (6 * tq * MIX_W * 4 + 8 * tq * D_MODEL * 4 + 4 * N_MEM * D_MODEL * 4 + 2 * 3 * D_MODEL * D_MODEL * 2)
    kv = lambda: pl.BlockSpec((1, N_MEM * X_ROWS, LANES), lambda i, j: (i, 0, 0))
    wspec = lambda: pl.BlockSpec((D_MODEL, D_MODEL), lambda i, j: (0, 0))
    half = lambda: pl.BlockSpec((1, tq, MIX_W), lambda i, j: (i, j, 0))
    full = lambda: pl.BlockSpec((1, tq, D_MODEL), lambda i, j: (i, j, 0))
    vec = lambda w: pl.BlockSpec((1, w), lambda i, j: (0, 0))
    row = lambda a: a.reshape(1, -1)
    return pl.pallas_call(
        _xattn_block_kernel,
        grid=(b, t // tq),
        in_specs=[half(), half(), half(), full(), vec(MIX_W), vec(MIX_W), vec(MIX_W), wspec(),
                  vec(D_MODEL), wspec(), kv(), kv(), wspec()],
        out_specs=full(),
        out_shape=jax.ShapeDtypeStruct((b, t, D_MODEL), F32),
        compiler_params=_params(("parallel", "parallel"), est),
        name="xattn_block",
    )(att, mlh, mo, x, row(wts["b_og"]), row(wts["g_attn_out"]), row(wts["g_mlstm_out"]), wts["w_out"],
      row(wts["g_cross"]), wts["w_cq"], mem_k, mem_v, wts["w_co"])


def _alibi_slopes():
    return np.exp2(-8.0 * (np.arange(N_HEADS, dtype=np.float64) + 1.0) / N_HEADS)


def _prompt_bias_tables():
    full, first = [], []
    for _, dil in DIL_GROUPS:
        jn = N_CLASSES // dil
        rn = BAND // jn
        qj, qm = np.divmod(np.arange(BAND), rn)
        kj, km = np.divmod(np.arange(2 * BAND), 2 * rn)
        for tabs, row_offset in ((full, rn), (first, 0)):
            delta = jn * (row_offset + qm[:, None] - km[None, :]) + (qj[:, None] - kj[None, :])
            valid = (delta >= 0) & (delta <= BAND)
            bias = -_alibi_slopes()[:, None, None] * (delta * dil)[None].astype(np.float64)
            tabs.append(np.where(valid[None], bias, NEG))
    return np.stack(full).astype(np.float32), np.stack(first).astype(np.float32)


UNITS_PER_TRIP = 16


def _largest_divisor(n, cap):
    return max(u for u in range(1, cap + 1) if n % u == 0)


def _attn_prompt_kernel(q_ref, k_ref, v_ref, bfull_ref, bfirst_ref, o_ref, og_ref, lg_ref, *, rows_per_class):
    lane = lax.broadcasted_iota(jnp.int32, (1, LANES), 1)
    first_head = lane < HEAD_DIM

    def unit(g, dil, r_d, blk, full):
        jn = N_CLASSES // dil
        rn = BAND // jn
        classes = [r_d + dil * j for j in range(jn)]
        if full:
            m0 = pl.multiple_of(blk * rn, SUBLANES)
            q_rows = pl.ds(m0, rn)
            k_rows = pl.ds(pl.multiple_of(m0 - rn, SUBLANES), 2 * rn)
        else:
            assert blk == 0
            q_rows = pl.ds(0, rn)
            k_rows = pl.ds(0, min(2 * rn, rows_per_class))
        gather = lambda ref, rows: jnp.concatenate([ref[0, c, rows, :] for c in classes], axis=0)
        q = gather(q_ref, q_rows) * (HEAD_DIM ** -0.5)
        k = gather(k_ref, k_rows).astype(BF16)
        v = gather(v_ref, k_rows).astype(BF16)
        if k.shape[0] < 2 * BAND:
            assert jn == 1
            k = jnp.concatenate([k, k], axis=0)
            v = jnp.concatenate([v, v], axis=0)
        outs, lses = [], []
        for h in range(2):
            qh = jnp.where(first_head if h == 0 else jnp.logical_not(first_head), q, 0.0).astype(BF16)
            s = _dot_nt(qh, k) + (bfull_ref[g, h] if full else bfirst_ref[g, h])
            m = jnp.max(s, axis=-1, keepdims=True)
            p = jnp.exp(s - m)
            l = jnp.sum(p, axis=-1, keepdims=True)
            outs.append(_dot(p.astype(BF16), v) * (1.0 / l))
            lses.append(m + jnp.log(l))
        o = jnp.where(first_head, outs[0], outs[1])
        lse = jnp.broadcast_to(jnp.where(first_head, lses[0], lses[1]), o.shape)
        for j, c in enumerate(classes):
            og_ref[g, c, q_rows, :] = o[j * rn:(j + 1) * rn]
            lg_ref[g, c, q_rows, :] = lse[j * rn:(j + 1) * rn]

    for g, (window, dil) in enumerate(DIL_GROUPS):
        assert window // dil == BAND and N_CLASSES % dil == 0
        nblk = rows_per_class * N_CLASSES // (dil * BAND)
        if dil <= UNITS_PER_TRIP:
            for r in range(dil):
                unit(g, dil, r, 0, False)
        else:
            def body1(r, carry, g=g, dil=dil):
                unit(g, dil, r, 0, False)
                return carry
            lax.fori_loop(0, dil, body1, 0, unroll=_largest_divisor(dil, UNITS_PER_TRIP))
        rest = dil * (nblk - 1)
        if rest:
            def body(j, carry, g=g, dil=dil, nblk=nblk):
                unit(g, dil, j // (nblk - 1), j % (nblk - 1) + 1, True)
                return carry
            lax.fori_loop(0, rest, body, 0, unroll=_largest_divisor(rest, UNITS_PER_TRIP))

    def merge(c, carry):
        l0, l1, l2 = lg_ref[0, c], lg_ref[1, c], lg_ref[2, c]
        mx = jnp.maximum(jnp.maximum(l0, l1), l2)
        e0, e1, e2 = jnp.exp(l0 - mx), jnp.exp(l1 - mx), jnp.exp(l2 - mx)
        merged = (e0 * og_ref[0, c] + e1 * og_ref[1, c] + e2 * og_ref[2, c]) * (1.0 / (e0 + e1 + e2))
        o_ref[0, pl.ds(c, rows_per_class, stride=N_CLASSES), :] = merged
        return carry
    lax.fori_loop(0, N_CLASSES, merge, 0, unroll=2)


def _attn_prompt(q, k, v):
    b, _, rpc, _ = q.shape
    s = rpc * N_CLASSES
    bfull, bfirst = (jnp.asarray(t) for t in _prompt_bias_tables())
    ng = len(DIL_GROUPS)
    blk = lambda: pl.BlockSpec((1, N_CLASSES, rpc, LANES), lambda i, p: (i, 0, 0, p))
    est = 8 * s * LANES * 4 + 6 * s * LANES * 4 + 2 * ng * 2 * BAND * 3 * BAND * 4
    return pl.pallas_call(
        functools.partial(_attn_prompt_kernel, rows_per_class=rpc),
        grid=(b, MIX_W // LANES),
        in_specs=[blk(), blk(), blk(),
                  pl.BlockSpec((ng, 2, BAND, 2 * BAND), lambda i, p: (0, p, 0, 0)),
                  pl.BlockSpec((ng, 2, BAND, 2 * BAND), lambda i, p: (0, p, 0, 0))],
        out_specs=pl.BlockSpec((1, s, LANES), lambda i, p: (i, 0, p)),
        out_shape=jax.ShapeDtypeStruct((b, s, MIX_W), F32),
        scratch_shapes=[pltpu.VMEM((ng, N_CLASSES, rpc, LANES), F32) for _ in range(2)],
        compiler_params=_params(("parallel", "parallel"), est),
        name="attn_prompt",
    )(q, k, v, bfull, bfirst)


def _sample_bias_tables(wb, t_new):
    slopes = _alibi_slopes()

    def table(dist):
        mult = np.zeros(dist.shape)
        for window, dil in DIL_GROUPS:
            mult += (dist >= 0) & (dist <= window) & (dist % dil == 0)
        logm = np.where(mult > 0, np.log(np.maximum(mult, 1.0)), NEG)
        tab = -slopes[:, None, None] * dist[None].astype(np.float64) + logm[None]
        tab = np.where((mult > 0)[None], tab, NEG)
        return tab.reshape(N_HEADS * t_new, dist.shape[1]).astype(np.float32)

    t = np.arange(t_new)[:, None]
    return table(wb + t - np.arange(wb)[None, :]), table(t - np.arange(t_new)[None, :])


def _attn_sample_kernel(q_ref, kn_ref, vn_ref, wk_ref, wv_ref, bo_ref, bn_ref,
                        att_ref, ok_ref, ov_ref, *, wb, t_new):
    rows_q = N_HEADS * t_new
    q = q_ref[0] * (HEAD_DIM ** -0.5)
    qm = jnp.concatenate([q] * N_HEADS, axis=0)
    row_head = lax.broadcasted_iota(jnp.int32, (rows_q, MIX_W), 0) // t_new
    lane_head = lax.broadcasted_iota(jnp.int32, (rows_q, MIX_W), 1) // HEAD_DIM
    own = row_head == lane_head
    qm = jnp.where(own, qm, 0.0).astype(BF16)
    k_new, v_new = kn_ref[0], vn_ref[0]
    wk, wv = wk_ref[0], wv_ref[0]
    s_old = _dot(qm, wk.astype(BF16)) + bo_ref[...]
    s_new = _dot_nt(qm, k_new.astype(BF16)) + bn_ref[...]
    m = jnp.maximum(jnp.max(s_old, axis=-1, keepdims=True), jnp.max(s_new, axis=-1, keepdims=True))
    p_old = jnp.exp(s_old - m)
    p_new = jnp.exp(s_new - m)
    inv_l = 1.0 / (jnp.sum(p_old, axis=-1, keepdims=True) + jnp.sum(p_new, axis=-1, keepdims=True))
    o = _dot_nt(p_old.astype(BF16), wv.astype(BF16)) + _dot(p_new.astype(BF16), v_new.astype(BF16))
    o = jnp.where(own, o * inv_l, 0.0)
    att = o[0:t_new]
    for h in range(1, N_HEADS):
        att = att + o[h * t_new:(h + 1) * t_new]
    att_ref[0] = att
    lane = lax.broadcasted_iota(jnp.int32, (MIX_W, LANES), 1)
    is_new = lane >= LANES - t_new
    pad = jnp.zeros((LANES - t_new, MIX_W), F32)
    for w, new, o_ref in ((wk, k_new, ok_ref), (wv, v_new, ov_ref)):
        rolled = pltpu.roll(w, wb - t_new, axis=1)
        o_ref[0] = rolled
        new_t = jnp.concatenate([pad, new], axis=0).T
        o_ref[0, :, wb - LANES:wb] = jnp.where(is_new, new_t, rolled[:, wb - LANES:wb])


def _attn_sample(q, k_new, v_new, win_k_t, win_v_t):
    b, t_new, _ = q.shape
    wb = win_k_t.shape[2]
    assert wb >= DIL_GROUPS[-1][0] and t_new % SUBLANES == 0 and wb % LANES == 0
    b_old, b_new = _sample_bias_tables(wb, t_new)
    small = lambda: pl.BlockSpec((1, t_new, MIX_W), lambda i: (i, 0, 0))
    big = lambda: pl.BlockSpec((1, MIX_W, wb), lambda i: (i, 0, 0))
    est = 8 * wb * MIX_W * 4 + 2 * wb * MIX_W * 2 + 4 * N_HEADS * t_new * wb * 4
    return pl.pallas_call(
        functools.partial(_attn_sample_kernel, wb=wb, t_new=t_new),
        grid=(b,),
        in_specs=[small(), small(), small(), big(), big(),
                  pl.BlockSpec((N_HEADS * t_new, wb), lambda i: (0, 0)),
                  pl.BlockSpec((N_HEADS * t_new, t_new), lambda i: (0, 0))],
        out_specs=[small(), big(), big()],
        out_shape=[jax.ShapeDtypeStruct((b, t_new, MIX_W), F32),
                   jax.ShapeDtypeStruct((b, MIX_W, wb), F32),
                   jax.ShapeDtypeStruct((b, MIX_W, wb), F32)],
        compiler_params=_params(("parallel",), est),
        name="attn_sample",
    )(q, k_new, v_new, win_k_t, win_v_t, jnp.asarray(b_old), jnp.asarray(b_new))


ST_ROWS = HEAD_DIM + SUBLANES


def _mlstm_seq_kernel(qt_ref, k_ref, vt_ref, gcol_ref, mirow_ref, mfrow_ref, bcol_ref, bigrow_ref, bfgrow_ref,
                      h_ref, st_ref, m_ref, *, chunk):
    L = chunk
    hp = lax.Precision.HIGHEST

    @pl.when(pl.program_id(1) == 0)
    def _():
        st_ref[...] = jnp.zeros(st_ref.shape, F32)
        m_ref[...] = jnp.full(m_ref.shape, NEG, F32)

    ss = lax.broadcasted_iota(jnp.int32, (L, L), 0)
    tt = lax.broadcasted_iota(jnp.int32, (L, L), 1)
    causal = ss <= tt
    ig_r = mirow_ref[0] + bigrow_ref[...]
    lf_r = _log_sigmoid(mfrow_ref[0] + bfgrow_ref[...])
    b_r = jnp.dot(lf_r, causal.astype(F32), precision=hp, preferred_element_type=F32)
    lane = lax.broadcasted_iota(jnp.int32, (N_HEADS, L), 1)
    cm = ig_r - b_r
    shift = 1
    while shift < L:
        cm = jnp.maximum(cm, jnp.where(lane >= shift, pltpu.roll(cm, shift, axis=1), NEG))
        shift *= 2
    m_prev = m_ref[0][:, 0:1]
    m_t = b_r + jnp.maximum(m_prev, cm)
    row_all = b_r - m_t
    inter_w = jnp.exp(b_r + m_prev - m_t)
    floor = jnp.exp(-m_t)
    m_new = m_t[:, L - 1:L]
    b_last = b_r[:, L - 1:L]
    w_s = jnp.exp(b_last - b_r + ig_r - m_new)
    decay = jnp.exp(b_last + m_prev - m_new)
    pre_c = gcol_ref[0] + bcol_ref[...]
    bcum_c = jnp.dot((ss >= tt).astype(F32), _log_sigmoid(pre_c), precision=hp, preferred_element_type=F32)
    col_all = pre_c - pltpu.roll(bcum_c, GATE_PAD - N_HEADS, axis=1)

    ones_row = (lax.broadcasted_iota(jnp.int32, (SUBLANES, L), 0) == 0).astype(F32)
    zeros_q = jnp.zeros((HEAD_DIM, L), BF16)
    h_parts, new_state = [], []
    for h in range(N_HEADS):
        pair, half = divmod(h, 2)
        rows = slice(h * HEAD_DIM, (h + 1) * HEAD_DIM)
        q_own = qt_ref[0, rows, :].astype(BF16)
        q_pad = jnp.concatenate([q_own, zeros_q] if half == 0 else [zeros_q, q_own], axis=0)
        k_pair = k_ref[0, :, pair * LANES:(pair + 1) * LANES].astype(BF16)
        v_ext = jnp.concatenate([vt_ref[0, rows, :], ones_row], axis=0)
        st = st_ref[0, h]
        d = jnp.exp(jnp.where(causal, col_all[:, h:h + 1] + row_all[h:h + 1, :], NEG))
        a = (_dot(k_pair, q_pad) * d).astype(BF16)
        tot = _dot(v_ext.astype(BF16), a) + inter_w[h:h + 1, :] * _dot(st.astype(BF16), q_pad)
        den = tot[HEAD_DIM:HEAD_DIM + 1, :]
        h_parts.append(tot[0:HEAD_DIM, :] * (1.0 / jnp.maximum(jnp.abs(den), floor[h:h + 1, :])))
        new_state.append(decay[h:h + 1, :] * st + _dot((v_ext * w_s[h:h + 1, :]).astype(BF16), k_pair))
    h_ref[0] = jnp.concatenate(h_parts, axis=0).T
    for h, st_new in enumerate(new_state):
        st_ref[0, h] = st_new
    m_ref[0] = jnp.broadcast_to(m_new, (N_HEADS, LANES))


def _mlstm_seq(q_t, k, v_t, gates, b_ig, b_fg, chunk):
    b, s, _ = k.shape
    assert s % chunk == 0 and chunk % LANES == 0
    g_rows = jnp.swapaxes(gates[:, :, :2 * N_HEADS], 1, 2)
    bcol = jnp.concatenate([b_ig, b_fg, jnp.zeros((GATE_PAD - 2 * N_HEADS,), F32)]).reshape(1, GATE_PAD)
    ft = lambda: pl.BlockSpec((1, MIX_W, chunk), lambda i, c: (i, 0, c))
    row = lambda: pl.BlockSpec((1, N_HEADS, chunk), lambda i, c: (i, 0, c))
    const = lambda r, w: pl.BlockSpec((r, w), lambda i, c: (0, 0))
    est = 8 * chunk * MIX_W * 4 + 12 * chunk * chunk * 4 + 4 * N_HEADS * ST_ROWS * LANES * 4
    h, st, m = pl.pallas_call(
        functools.partial(_mlstm_seq_kernel, chunk=chunk),
        grid=(b, s // chunk),
        in_specs=[ft(), pl.BlockSpec((1, chunk, MIX_W), lambda i, c: (i, c, 0)), ft(),
                  pl.BlockSpec((1, chunk, GATE_PAD), lambda i, c: (i, c, 0)), row(), row(),
                  const(1, GATE_PAD), const(N_HEADS, 1), const(N_HEADS, 1)],
        out_specs=[pl.BlockSpec((1, chunk, MIX_W), lambda i, c: (i, c, 0)),
                   pl.BlockSpec((1, N_HEADS, ST_ROWS, LANES), lambda i, c: (i, 0, 0, 0)),
                   pl.BlockSpec((1, N_HEADS, LANES), lambda i, c: (i, 0, 0))],
        out_shape=[jax.ShapeDtypeStruct((b, s, MIX_W), F32),
                   jax.ShapeDtypeStruct((b, N_HEADS, ST_ROWS, LANES), F32),
                   jax.ShapeDtypeStruct((b, N_HEADS, LANES), F32)],
        compiler_params=_params(("arbitrary", "arbitrary"), est),
        name="mlstm_seq",
    )(q_t, k, v_t, gates, g_rows[:, :N_HEADS], g_rows[:, N_HEADS:], bcol, b_ig.reshape(-1, 1), b_fg.reshape(-1, 1))
    st = st.reshape(b, N_HEADS // 2, 2, ST_ROWS, 2, HEAD_DIM)
    st = jnp.stack([st[:, :, 0, :, 0, :], st[:, :, 1, :, 1, :]], axis=2).reshape(b, N_HEADS, ST_ROWS, HEAD_DIM)
    c = jnp.swapaxes(st[:, :, :HEAD_DIM, :], 2, 3)
    return h, c, st[:, :, HEAD_DIM, :], m[:, :, 0]


def _mlstm_step_kernel(xq_ref, xk_ref, v_ref, g_ref, sq_ref, sk_ref, cwq_ref, cwk_ref, cbq_ref, cbk_ref, gb_ref,
                       c0_ref, n0_ref, m0_ref, h_ref, c_ref, n_ref, m_ref, q_scr, ik_scr, *, t_new, batch):
    hist = CONV_W - 1
    tile = lambda ref, t: ref[pl.ds(t, batch, stride=t_new), :].T
    xq = [sq_ref[t].T for t in range(hist)] + [tile(xq_ref, t) for t in range(t_new)]
    xk = [sk_ref[t].T for t in range(hist)] + [tile(xk_ref, t) for t in range(t_new)]
    c_ref[...] = c0_ref[...]
    n = [n0_ref[e] for e in range(2)]
    m = [m0_ref[e] for e in range(2)]
    for t in range(t_new):
        cq, ck = cbq_ref[...], cbk_ref[...]
        for j in range(CONV_W):
            cq = cq + xq[t + j] * cwq_ref[j]
            ck = ck + xk[t + j] * cwk_ref[j]
        q_t = cq * _sigmoid(cq)
        k_t = ck * _sigmoid(ck) * (HEAD_DIM ** -0.5)
        v_t = tile(v_ref, t)
        gates = tile(g_ref, t)[0:2 * N_HEADS, :] + gb_ref[...]
        h_pair = []
        for e in range(2):
            head = 2 * pl.program_id(0) + e
            rows = slice(e * HEAD_DIM, (e + 1) * HEAD_DIM)
            ig = _gates_row(gates, head)
            lf = _log_sigmoid(_gates_row(gates, N_HEADS + head))
            m_new = jnp.maximum(lf + m[e], ig)
            f = jnp.exp(lf + m[e] - m_new)
            i = jnp.exp(ig - m_new)
            q_e, k_e, v_e = q_t[rows], k_t[rows], v_t[rows]
            q_scr[...] = q_e
            ik_scr[...] = k_e * i

            def body(kk, num, e=e, f=f, v_e=v_e):
                c_new = f * c_ref[e, kk] + ik_scr[pl.ds(kk, 1), :] * v_e
                c_ref[e, kk] = c_new
                return num + c_new * q_scr[pl.ds(kk, 1), :]
            num = lax.fori_loop(0, HEAD_DIM, body, jnp.zeros((HEAD_DIM, batch), F32), unroll=8)
            n[e] = f * n[e] + k_e * i
            den = jnp.sum(n[e] * q_e, axis=0, keepdims=True)
            h_pair.append(num * (1.0 / jnp.maximum(jnp.abs(den), jnp.exp(-m_new))))
            m[e] = m_new
        h_ref[pl.ds(t, batch, stride=t_new), :] = jnp.concatenate(h_pair, axis=0).T
    for e in range(2):
        n_ref[e] = n[e]
        m_ref[e] = m[e]


def _gates_row(gates, idx):
    sel = lax.broadcasted_iota(jnp.int32, gates.shape, 0) == idx
    return jnp.sum(jnp.where(sel, gates, 0.0), axis=0, keepdims=True)


def _mlstm_step(qk_pre, conv_state, conv_w, conv_b, mv, gates, b_ig, b_fg, c0, n0, m0, t_new):
    rows = qk_pre.shape[0]
    batch = rows // t_new
    assert batch == LANES and t_new % SUBLANES == 0
    hist = CONV_W - 1
    lanes_b = lambda a: jnp.broadcast_to(a[..., None], a.shape + (batch,))
    sc = jnp.swapaxes(conv_state, 0, 1)
    cw, cb = lanes_b(conv_w), lanes_b(conv_b)
    gb = lanes_b(jnp.concatenate([b_ig, b_fg]))
    c_t = jnp.transpose(c0, (1, 2, 3, 0))
    n_t = jnp.transpose(n0, (1, 2, 0))
    m_t = jnp.transpose(m0, (1, 0)).reshape(N_HEADS, 1, batch)
    npair = N_HEADS // 2
    colq = lambda: pl.BlockSpec((rows, LANES), lambda p: (0, p))
    colk = lambda: pl.BlockSpec((rows, LANES), lambda p: (0, npair + p))
    est = (8 * rows * LANES * 4 + 4 * 2 * HEAD_DIM * HEAD_DIM * batch * 4 + 2 * (hist + 2 * CONV_W + 2) * LANES * batch * 4
           + 4 * (hist + t_new) * LANES * batch * 4)
    h, c, n, m = pl.pallas_call(
        functools.partial(_mlstm_step_kernel, t_new=t_new, batch=batch),
        grid=(npair,),
        in_specs=[colq(), colk(), colq(), pl.BlockSpec((rows, GATE_PAD), lambda p: (0, 0)),
                  pl.BlockSpec((hist, batch, LANES), lambda p: (0, 0, p)),
                  pl.BlockSpec((hist, batch, LANES), lambda p: (0, 0, npair + p)),
                  pl.BlockSpec((CONV_W, LANES, batch), lambda p: (0, p, 0)),
                  pl.BlockSpec((CONV_W, LANES, batch), lambda p: (0, npair + p, 0)),
                  pl.BlockSpec((LANES, batch), lambda p: (p, 0)),
                  pl.BlockSpec((LANES, batch), lambda p: (npair + p, 0)),
                  pl.BlockSpec((2 * N_HEADS, batch), lambda p: (0, 0)),
                  pl.BlockSpec((2, HEAD_DIM, HEAD_DIM, batch), lambda p: (p, 0, 0, 0)),
                  pl.BlockSpec((2, HEAD_DIM, batch), lambda p: (p, 0, 0)),
                  pl.BlockSpec((2, 1, batch), lambda p: (p, 0, 0))],
        out_specs=[colq(),
                   pl.BlockSpec((2, HEAD_DIM, HEAD_DIM, batch), lambda p: (p, 0, 0, 0)),
                   pl.BlockSpec((2, HEAD_DIM, batch), lambda p: (p, 0, 0)),
                   pl.BlockSpec((2, 1, batch), lambda p: (p, 0, 0))],
        out_shape=[jax.ShapeDtypeStruct((rows, MIX_W), F32),
                   jax.ShapeDtypeStruct(c_t.shape, F32), jax.ShapeDtypeStruct(n_t.shape, F32),
                   jax.ShapeDtypeStruct(m_t.shape, F32)],
        scratch_shapes=[pltpu.VMEM((HEAD_DIM, batch), F32), pltpu.VMEM((HEAD_DIM, batch), F32)],
        compiler_params=_params(("arbitrary",), est),
        name="mlstm_step",
    )(qk_pre, qk_pre, mv, gates, sc, sc, cw, cw, cb, cb, gb, c_t, n_t, m_t)
    return (h, jnp.transpose(c, (3, 0, 1, 2)), jnp.transpose(n, (2, 0, 1)),
            jnp.transpose(m.reshape(N_HEADS, batch), (1, 0)))


def _mixers_to_output(x2, att, mlh, mo, mem_k, mem_v, wts, bt_x, tq_x, tm):
    b = mem_k.shape[0]
    if tq_x >= 2 * LANES:
        r3 = lambda a: a.reshape(b, -1, a.shape[-1])
        h2 = _xattn_block(r3(att), r3(mlh), r3(mo), r3(x2), wts, mem_k, mem_v, tq_x).reshape(-1, D_MODEL)
    else:
        h1 = _merge(att, mlh, mo, wts["b_og"], wts["g_attn_out"], wts["g_mlstm_out"], wts["w_out"], x2, tm)
        (xq,) = _norm_matmul(h1, wts["g_cross"], wts["w_cq"], (D_MODEL,), tm)
        xo = _xattn(xq.reshape(b, -1, D_MODEL), mem_k, mem_v, bt_x, tq_x)
        h2 = _matmul_res(xo.reshape(-1, D_MODEL), wts["w_co"], h1, tm)
    return _ffn(h2, wts["g_ffn"], wts["w_gate"], wts["w_up"], wts["w_down"], wts["g_final"], tm)


def kernel(x_prompt, x_sample, mem_prompt, cache_win_k, cache_win_v, cache_mem_k, cache_mem_v, state_conv, state_mlstm_C, state_mlstm_n, state_mlstm_m, g_mix, w_in, conv_w, conv_b, b_ig, b_fg, b_og, g_attn_out, g_mlstm_out, w_out, g_cross, g_mem, w_cq, w_ck, w_cv, w_co, g_ffn, w_gate, w_up, w_down, g_final):
    bp, sp, _ = x_prompt.shape
    bs, ts, _ = x_sample.shape
    wts = dict(b_og=b_og, g_attn_out=g_attn_out, g_mlstm_out=g_mlstm_out, g_cross=g_cross, g_ffn=g_ffn,
               g_final=g_final, w_out=w_out.astype(BF16), w_cq=w_cq.astype(BF16), w_co=w_co.astype(BF16),
               w_gate=w_gate.astype(BF16), w_up=w_up.astype(BF16), w_down=w_down.astype(BF16))
    w_in_p = jnp.pad(w_in, ((0, 0), (0, sum(IN_WIDTHS) - N_IN))).astype(BF16)
    w_ckv = jnp.concatenate([w_ck, w_cv], axis=1).astype(BF16)
    tm = 512

    xp2 = x_prompt.reshape(bp * sp, D_MODEL)
    aq_c, ak_c, av_c, ak_t, av_t, mq_t, mk, mv_t, mo, gates, conv_tail = _inproj_prompt(
        xp2, g_mix, w_in_p, conv_w, conv_b, tm, sp)
    r3 = lambda a: a.reshape(bp, sp, -1)
    att = _attn_prompt(aq_c, ak_c, av_c)
    keep = min(DIL_GROUPS[-1][0], sp)
    from_t = lambda a: jnp.transpose(a.reshape(a.shape[0], N_HEADS, HEAD_DIM, a.shape[2]), (0, 3, 1, 2))
    win_k_p = from_t(ak_t)[:, sp - keep:]
    win_v_p = from_t(av_t)[:, sp - keep:]
    mlh, c_p, n_p, m_p = _mlstm_seq(mq_t, r3(mk), mv_t, r3(gates), b_ig, b_fg, chunk=256)
    conv_p = conv_tail[:, SUBLANES - (CONV_W - 1):]
    mem_k_view, mem_v_view = _mem_proj(mem_prompt, g_mem, w_ckv)
    mem_k_p, mem_v_p = _mem_tile_unview(mem_k_view), _mem_tile_unview(mem_v_view)
    y_p = _mixers_to_output(xp2, att.reshape(-1, MIX_W), mlh.reshape(-1, MIX_W), mo, mem_k_view, mem_v_view, wts,
                            bt_x=1, tq_x=512, tm=tm)

    xs2 = x_sample.reshape(bs * ts, D_MODEL)
    aq, ak, av, qk_pre, mv, mo, gates = _norm_matmul(xs2, g_mix, w_in_p, IN_WIDTHS, tm)
    r3 = lambda a: a.reshape(bs, ts, -1)
    wb = cache_win_k.shape[1]
    to_t = lambda a: jnp.transpose(a, (0, 2, 3, 1)).reshape(a.shape[0], MIX_W, a.shape[1])
    att, win_k_s, win_v_s = _attn_sample(r3(aq), r3(ak), r3(av), to_t(cache_win_k), to_t(cache_win_v))
    mlh, c_s, n_s, m_s = _mlstm_step(qk_pre, state_conv, conv_w, conv_b, mv, gates, b_ig, b_fg,
                                     state_mlstm_C, state_mlstm_n, state_mlstm_m, ts)
    conv_s = jnp.concatenate([state_conv, r3(qk_pre)], axis=1)[:, -(CONV_W - 1):]
    y_s = _mixers_to_output(xs2, att.reshape(-1, MIX_W), mlh, mo, _mem_tile_view(cache_mem_k),
                            _mem_tile_view(cache_mem_v), wts,
                            bt_x=4, tq_x=ts, tm=tm)

    return (y_p.reshape(bp, sp, D_MODEL), y_s.reshape(bs, ts, D_MODEL),
            win_k_p, win_v_p, conv_p, c_p, n_p, m_p, mem_k_p, mem_v_p,
            from_t(win_k_s), from_t(win_v_s), conv_s, c_s, n_s, m_s)
```

```python
import functools

import numpy as np
import jax
import jax.numpy as jnp
from jax import lax
from jax.experimental import pallas as pl
from jax.experimental.pallas import tpu as pltpu

F32 = jnp.float32
BF16 = jnp.bfloat16

D_MODEL = 1024
HEAD_DIM = 64
N_HEADS = 8
MIX_W = N_HEADS * HEAD_DIM
DIL_GROUPS = ((128, 1), (512, 4), (2048, 16))
BAND = 128
CONV_W = 4
N_MEM = 256
N_X_HEADS = 4
X_HEAD_DIM = D_MODEL // N_X_HEADS
D_FF = 2816
RMS_EPS = 1e-6
NEG = -1e30
N_IN = 3 * MIX_W + 2 * MIX_W + MIX_W + MIX_W + 2 * N_HEADS
GATE_PAD = 128

LANES = 128
SUBLANES = 8
VMEM_BYTES_V7X = 64 * 1024 * 1024


def _vmem_limit(nbytes):
    return int(min(max(2 * nbytes, 16 * 1024 * 1024), VMEM_BYTES_V7X - 8 * 1024 * 1024))


def _params(semantics, vmem_estimate):
    return pltpu.CompilerParams(dimension_semantics=semantics,
                                vmem_limit_bytes=_vmem_limit(vmem_estimate))


def _rms(x, g):
    return x * lax.rsqrt(jnp.mean(x * x, axis=-1, keepdims=True) + RMS_EPS) * g


def _sigmoid(x):
    return 1.0 / (1.0 + jnp.exp(-x))


def _log_sigmoid(x):
    return jnp.minimum(x, 0.0) - jnp.log(1.0 + jnp.exp(-jnp.abs(x)))


def _dot(a, b):
    return jnp.dot(a, b, preferred_element_type=F32)


def _dot_nt(a, b):
    return lax.dot_general(a, b, (((1,), (1,)), ((), ())), preferred_element_type=F32)


N_CLASSES = max(d for _, d in DIL_GROUPS)
IN_WIDTHS = (MIX_W, MIX_W, MIX_W, 2 * MIX_W, MIX_W, MIX_W, GATE_PAD)
IN_OFFS = tuple(int(o) for o in np.cumsum((0,) + IN_WIDTHS))


def _norm_matmul_kernel(x_ref, g_ref, w_ref, *o_refs, splits):
    xn = _rms(x_ref[...], g_ref[...]).astype(BF16)
    for o_ref, (a, b) in zip(o_refs, splits):
        o_ref[...] = _dot(xn, w_ref[:, a:b])


def _norm_matmul(x, g, w, widths, tm):
    m, k = x.shape
    n = w.shape[1]
    assert sum(widths) == n and m % tm == 0
    offs = np.cumsum([0] + list(widths))
    splits = tuple((int(offs[i]), int(offs[i + 1])) for i in range(len(widths)))
    est = 2 * tm * k * 4 + 2 * k * n * 2 + 2 * tm * n * 4
    return pl.pallas_call(
        functools.partial(_norm_matmul_kernel, splits=splits),
        grid=(m // tm,),
        in_specs=[pl.BlockSpec((tm, k), lambda i: (i, 0)),
                  pl.BlockSpec((1, k), lambda i: (0, 0)),
                  pl.BlockSpec((k, n), lambda i: (0, 0))],
        out_specs=[pl.BlockSpec((tm, wd), lambda i: (i, 0)) for wd in widths],
        out_shape=[jax.ShapeDtypeStruct((m, wd), F32) for wd in widths],
        compiler_params=_params(("parallel",), est),
        name="norm_matmul",
    )(x, g.reshape(1, k), w)


def _inproj_prompt_kernel(x_ref, g_ref, w_ref, wvt_ref, perm_ref, cw_ref, cb_ref,
                          qc_ref, kc_ref, vc_ref, kt_ref, vt_ref, mq_ref, mk_ref, mvt_ref, mo_ref, gate_ref, tail_ref,
                          xp_scr, *, per_seq):
    tm = x_ref.shape[0]

    @pl.when(pl.program_id(0) % per_seq == 0)
    def _():
        xp_scr[0:SUBLANES, :] = jnp.zeros((SUBLANES, 2 * MIX_W), F32)

    xn = _rms(x_ref[...], g_ref[...]).astype(BF16)
    proj = lambda idx: _dot(xn, w_ref[:, IN_OFFS[idx]:IN_OFFS[idx + 1]])

    xp_scr[SUBLANES:SUBLANES + tm, :] = proj(3)
    conv = cb_ref[...]
    for j in range(CONV_W):
        off = SUBLANES - (CONV_W - 1) + j
        conv = conv + xp_scr[off:off + tm, :] * cw_ref[j:j + 1, :]
    tail = xp_scr[tm:tm + SUBLANES, :]
    xp_scr[0:SUBLANES, :] = tail
    tail_ref[0] = tail
    act = conv * _sigmoid(conv)
    mq_ref[0] = act[:, :MIX_W].T
    mk_ref[...] = act[:, MIX_W:] * (HEAD_DIM ** -0.5)

    for idx, c_ref, t_ref in ((0, qc_ref, None), (1, kc_ref, kt_ref), (2, vc_ref, vt_ref)):
        y = proj(idx)
        if t_ref is not None:
            t_ref[0] = y.T
        y_cls = _dot(perm_ref[...], y.astype(BF16))
        c_ref[0] = y_cls.reshape(N_CLASSES, tm // N_CLASSES, MIX_W)

    mvt_ref[0] = _dot_nt(wvt_ref[...], xn)
    mo_ref[...] = proj(5)
    gate_ref[...] = proj(6)


def _inproj_prompt(x, g, w, conv_w, conv_b, tm, seq):
    m, k = x.shape
    n = w.shape[1]
    assert n == IN_OFFS[-1] and m % seq == 0 and seq % tm == 0 and tm % (N_CLASSES * SUBLANES) == 0
    b, per_seq = m // seq, seq // tm
    est = (2 * tm * k * 4 + 2 * k * n * 2 + 2 * tm * (n + 4 * MIX_W) * 4 + 3 * tm * MIX_W * 4
           + 3 * (tm + SUBLANES) * 2 * MIX_W * 4)
    cls_spec = lambda: pl.BlockSpec((1, N_CLASSES, tm // N_CLASSES, MIX_W), lambda i: (i // per_seq, 0, i % per_seq, 0))
    cls_shape = jax.ShapeDtypeStruct((b, N_CLASSES, seq // N_CLASSES, MIX_W), F32)
    t_spec = lambda: pl.BlockSpec((1, MIX_W, tm), lambda i: (i // per_seq, 0, i % per_seq))
    t_shape = jax.ShapeDtypeStruct((b, MIX_W, seq), F32)
    nat_spec = lambda wd: pl.BlockSpec((tm, wd), lambda i: (i, 0))
    nat_shape = lambda wd: jax.ShapeDtypeStruct((m, wd), F32)
    cw = jnp.pad(conv_w, ((0, SUBLANES - CONV_W), (0, 0)))
    dst = np.arange(tm)
    perm = np.zeros((tm, tm), np.float32)
    perm[dst, (dst % (tm // N_CLASSES)) * N_CLASSES + dst // (tm // N_CLASSES)] = 1.0
    return pl.pallas_call(
        functools.partial(_inproj_prompt_kernel, per_seq=per_seq),
        grid=(m // tm,),
        in_specs=[pl.BlockSpec((tm, k), lambda i: (i, 0)),
                  pl.BlockSpec((1, k), lambda i: (0, 0)),
                  pl.BlockSpec((k, n), lambda i: (0, 0)),
                  pl.BlockSpec((MIX_W, k), lambda i: (0, 0)),
                  pl.BlockSpec((tm, tm), lambda i: (0, 0)),
                  pl.BlockSpec((SUBLANES, 2 * MIX_W), lambda i: (0, 0)),
                  pl.BlockSpec((1, 2 * MIX_W), lambda i: (0, 0))],
        out_specs=[cls_spec(), cls_spec(), cls_spec(), t_spec(), t_spec(), t_spec(), nat_spec(MIX_W), t_spec(),
                   nat_spec(MIX_W), nat_spec(GATE_PAD),
                   pl.BlockSpec((1, SUBLANES, 2 * MIX_W), lambda i: (i // per_seq, 0, 0))],
        out_shape=[cls_shape, cls_shape, cls_shape, t_shape, t_shape, t_shape, nat_shape(MIX_W), t_shape,
                   nat_shape(MIX_W), nat_shape(GATE_PAD),
                   jax.ShapeDtypeStruct((b, SUBLANES, 2 * MIX_W), F32)],
        scratch_shapes=[pltpu.VMEM((tm + SUBLANES, 2 * MIX_W), F32)],
        compiler_params=_params(("arbitrary",), est),
        name="inproj_prompt",
    )(x, g.reshape(1, k), w, w[:, IN_OFFS[4]:IN_OFFS[5]].T, jnp.asarray(perm, BF16), cw, conv_b.reshape(1, -1))


def _matmul_res_kernel(x_ref, w_ref, r_ref, o_ref):
    o_ref[...] = r_ref[...] + _dot(x_ref[...].astype(BF16), w_ref[...])


def _matmul_res(x, w, res, tm):
    m, k = x.shape
    n = w.shape[1]
    est = 2 * tm * k * 4 + 2 * k * n * 2 + 4 * tm * n * 4
    return pl.pallas_call(
        _matmul_res_kernel,
        grid=(m // tm,),
        in_specs=[pl.BlockSpec((tm, k), lambda i: (i, 0)),
                  pl.BlockSpec((k, n), lambda i: (0, 0)),
                  pl.BlockSpec((tm, n), lambda i: (i, 0))],
        out_specs=pl.BlockSpec((tm, n), lambda i: (i, 0)),
        out_shape=jax.ShapeDtypeStruct((m, n), F32),
        compiler_params=_params(("parallel",), est),
        name="matmul_res",
    )(x, w, res)


def _merge_kernel(att_ref, mlh_ref, mo_ref, bog_ref, ga_ref, gm_ref, w_ref, x_ref, o_ref):
    a_n = _rms(att_ref[...], ga_ref[...]).astype(BF16)
    ml = _sigmoid(mo_ref[...] + bog_ref[...]) * mlh_ref[...]
    m_n = _rms(ml, gm_ref[...]).astype(BF16)
    o_ref[...] = x_ref[...] + _dot(a_n, w_ref[0:MIX_W, :]) + _dot(m_n, w_ref[MIX_W:2 * MIX_W, :])


def _merge(att, mlh, mo, b_og, g_a, g_m, w_out, x, tm):
    m = x.shape[0]
    half = lambda: pl.BlockSpec((tm, MIX_W), lambda i: (i, 0))
    vec = lambda: pl.BlockSpec((1, MIX_W), lambda i: (0, 0))
    est = 6 * tm * MIX_W * 4 + 2 * 2 * MIX_W * D_MODEL * 2 + 4 * tm * D_MODEL * 4
    return pl.pallas_call(
        _merge_kernel,
        grid=(m // tm,),
        in_specs=[half(), half(), half(), vec(), vec(), vec(),
                  pl.BlockSpec((2 * MIX_W, D_MODEL), lambda i: (0, 0)),
                  pl.BlockSpec((tm, D_MODEL), lambda i: (i, 0))],
        out_specs=pl.BlockSpec((tm, D_MODEL), lambda i: (i, 0)),
        out_shape=jax.ShapeDtypeStruct((m, D_MODEL), F32),
        compiler_params=_params(("parallel",), est),
        name="merge",
    )(att, mlh, mo, b_og.reshape(1, -1), g_a.reshape(1, -1), g_m.reshape(1, -1), w_out, x)


def _ffn_kernel(h_ref, gf_ref, wg_ref, wu_ref, wd_ref, gl_ref, o_ref):
    h = h_ref[...]
    hn = _rms(h, gf_ref[...]).astype(BF16)
    gate = _dot(hn, wg_ref[...])
    up = _dot(hn, wu_ref[...])
    act = (gate * _sigmoid(gate) * up).astype(BF16)
    o_ref[...] = _rms(h + _dot(act, wd_ref[...]), gl_ref[...])


def _ffn(h, g_ffn, w_gate, w_up, w_down, g_final, tm):
    m = h.shape[0]
    est = 4 * tm * D_MODEL * 4 + 2 * 3 * D_MODEL * D_FF * 2 + 3 * tm * D_FF * 4
    return pl.pallas_call(
        _ffn_kernel,
        grid=(m // tm,),
        in_specs=[pl.BlockSpec((tm, D_MODEL), lambda i: (i, 0)),
                  pl.BlockSpec((1, D_MODEL), lambda i: (0, 0)),
                  pl.BlockSpec((D_MODEL, D_FF), lambda i: (0, 0)),
                  pl.BlockSpec((D_MODEL, D_FF), lambda i: (0, 0)),
                  pl.BlockSpec((D_FF, D_MODEL), lambda i: (0, 0)),
                  pl.BlockSpec((1, D_MODEL), lambda i: (0, 0))],
        out_specs=pl.BlockSpec((tm, D_MODEL), lambda i: (i, 0)),
        out_shape=jax.ShapeDtypeStruct((m, D_MODEL), F32),
        compiler_params=_params(("parallel",), est),
        name="ffn",
    )(h, g_ffn.reshape(1, -1), w_gate, w_up, w_down, g_final.reshape(1, -1))


X_HALVES = X_HEAD_DIM // LANES
X_ROWS = N_X_HEADS * X_HALVES


def _mem_tile_view(a):
    b = a.shape[0]
    a = a.reshape(b, N_MEM, N_X_HEADS, X_HALVES, LANES)
    return jnp.transpose(a, (0, 1, 3, 2, 4)).reshape(b, N_MEM * X_ROWS, LANES)


def _mem_tile_unview(a):
    b = a.shape[0]
    a = a.reshape(b, N_MEM, X_HALVES, N_X_HEADS, LANES)
    return jnp.transpose(a, (0, 1, 3, 2, 4)).reshape(b, N_MEM, N_X_HEADS, X_HEAD_DIM)


def _mem_proj_kernel(x_ref, g_ref, w_ref, k_ref, v_ref):
    xn = _rms(x_ref[0], g_ref[...]).astype(BF16)
    for idx, o_ref in enumerate((k_ref, v_ref)):
        y = _dot(xn, w_ref[:, idx * D_MODEL:(idx + 1) * D_MODEL])
        for h in range(N_X_HEADS):
            for j in range(X_HALVES):
                lanes = slice(h * X_HEAD_DIM + j * LANES, h * X_HEAD_DIM + (j + 1) * LANES)
                o_ref[0, pl.ds(j * N_X_HEADS + h, N_MEM, stride=X_ROWS), :] = y[:, lanes]


def _mem_proj(mem, g, w_ckv):
    b = mem.shape[0]
    view = jax.ShapeDtypeStruct((b, N_MEM * X_ROWS, LANES), F32)
    est = 2 * N_MEM * D_MODEL * 4 + 2 * D_MODEL * 2 * D_MODEL * 2 + 6 * N_MEM * D_MODEL * 4
    return pl.pallas_call(
        _mem_proj_kernel,
        grid=(b,),
        in_specs=[pl.BlockSpec((1, N_MEM, D_MODEL), lambda i: (i, 0, 0)),
                  pl.BlockSpec((1, D_MODEL), lambda i: (0, 0)),
                  pl.BlockSpec((D_MODEL, 2 * D_MODEL), lambda i: (0, 0))],
        out_specs=[pl.BlockSpec((1, N_MEM * X_ROWS, LANES), lambda i: (i, 0, 0)) for _ in range(2)],
        out_shape=[view, view],
        compiler_params=_params(("parallel",), est),
        name="mem_proj",
    )(mem, g.reshape(1, -1), w_ckv)


def _mem_head(ref, bi, h):
    halves = [ref[bi, pl.ds(j * N_X_HEADS + h, N_MEM, stride=X_ROWS), :] for j in range(X_HALVES)]
    return jnp.concatenate(halves, axis=1).astype(BF16)


def _xattn_kernel(q_ref, k_ref, v_ref, o_ref, *, bt):
    for bi in range(bt):
        for h in range(N_X_HEADS):
            sl = slice(h * X_HEAD_DIM, (h + 1) * X_HEAD_DIM)
            q = (q_ref[bi, :, sl] * (X_HEAD_DIM ** -0.5)).astype(BF16)
            s = _dot_nt(q, _mem_head(k_ref, bi, h))
            p = jnp.exp(s - jnp.max(s, axis=-1, keepdims=True))
            inv_l = 1.0 / jnp.sum(p, axis=-1, keepdims=True)
            o_ref[bi, :, sl] = _dot(p.astype(BF16), _mem_head(v_ref, bi, h)) * inv_l


def _xattn(q, mem_k, mem_v, bt, tq):
    b, t, _ = q.shape
    est = 4 * bt * tq * D_MODEL * 4 + 4 * bt * N_MEM * D_MODEL * 4
    kv = lambda: pl.BlockSpec((bt, N_MEM * X_ROWS, LANES), lambda i, j: (i, 0, 0))
    return pl.pallas_call(
        functools.partial(_xattn_kernel, bt=bt),
        grid=(b // bt, t // tq),
        in_specs=[pl.BlockSpec((bt, tq, D_MODEL), lambda i, j: (i, j, 0)), kv(), kv()],
        out_specs=pl.BlockSpec((bt, tq, D_MODEL), lambda i, j: (i, j, 0)),
        out_shape=jax.ShapeDtypeStruct((b, t, D_MODEL), F32),
        compiler_params=_params(("parallel", "parallel"), est),
        name="xattn",
    )(q, mem_k, mem_v)


def _xattn_block_kernel(att_ref, mlh_ref, mo_ref, x_ref, bog_ref, ga_ref, gm_ref, wout_ref,
                        g_ref, wq_ref, k_ref, v_ref, wo_ref, o_ref):
    a_n = _rms(att_ref[0], ga_ref[...]).astype(BF16)
    m_n = _rms(_sigmoid(mo_ref[0] + bog_ref[...]) * mlh_ref[0], gm_ref[...]).astype(BF16)
    h1 = x_ref[0] + _dot(a_n, wout_ref[0:MIX_W, :]) + _dot(m_n, wout_ref[MIX_W:2 * MIX_W, :])
    q_all = _dot(_rms(h1, g_ref[...]).astype(BF16), wq_ref[...]) * (X_HEAD_DIM ** -0.5)
    o_parts = []
    for h in range(N_X_HEADS):
        sl = slice(h * X_HEAD_DIM, (h + 1) * X_HEAD_DIM)
        s = _dot_nt(q_all[:, sl].astype(BF16), _mem_head(k_ref, 0, h))
        p = jnp.exp(s - jnp.max(s, axis=-1, keepdims=True))
        inv_l = 1.0 / jnp.sum(p, axis=-1, keepdims=True)
        o_parts.append((_dot(p.astype(BF16), _mem_head(v_ref, 0, h)) * inv_l).astype(BF16))
    o_ref[0] = h1 + _dot(jnp.concatenate(o_parts, axis=1), wo_ref[...])


def _xattn_block(att, mlh, mo, x, wts, mem_k, mem_v, tq):
    b, t, _ = x.shape
    est = (6 * tq * MIX_W * 4 + 8 * tq * D_MODEL * 4 + 4 * N_MEM * D_MODEL * 4 + 2 * 3 * D_MODEL * D_MODEL * 2)
    kv = lambda: pl.BlockSpec((1, N_MEM * X_ROWS, LANES), lambda i, j: (i, 0, 0))
    wspec = lambda: pl.BlockSpec((D_MODEL, D_MODEL), lambda i, j: (0, 0))
    half = lambda: pl.BlockSpec((1, tq, MIX_W), lambda i, j: (i, j, 0))
    full = lambda: pl.BlockSpec((1, tq, D_MODEL), lambda i, j: (i, j, 0))
    vec = lambda w: pl.BlockSpec((1, w), lambda i, j: (0, 0))
    row = lambda a: a.reshape(1, -1)
    return pl.pallas_call(
        _xattn_block_kernel,
        grid=(b, t // tq),
        in_specs=[half(), half(), half(), full(), vec(MIX_W), vec(MIX_W), vec(MIX_W), wspec(),
                  vec(D_MODEL), wspec(), kv(), kv(), wspec()],
        out_specs=full(),
        out_shape=jax.ShapeDtypeStruct((b, t, D_MODEL), F32),
        compiler_params=_params(("parallel", "parallel"), est),
        name="xattn_block",
    )(att, mlh, mo, x, row(wts["b_og"]), row(wts["g_attn_out"]), row(wts["g_mlstm_out"]), wts["w_out"],
      row(wts["g_cross"]), wts["w_cq"], mem_k, mem_v, wts["w_co"])


def _alibi_slopes():
    return np.exp2(-8.0 * (np.arange(N_HEADS, dtype=np.float64) + 1.0) / N_HEADS)


def _prompt_bias_tables():
    full, first = [], []
    for _, dil in DIL_GROUPS:
        jn = N_CLASSES // dil
        rn = BAND // jn
        qj, qm = np.divmod(np.arange(BAND), rn)
        kj, km = np.divmod(np.arange(2 * BAND), 2 * rn)
        for tabs, row_offset in ((full, rn), (first, 0)):
            delta = jn * (row_offset + qm[:, None] - km[None, :]) + (qj[:, None] - kj[None, :])
            valid = (delta >= 0) & (delta <= BAND)
            bias = -_alibi_slopes()[:, None, None] * (delta * dil)[None].astype(np.float64)
            tabs.append(np.where(valid[None], bias, NEG))
    return np.stack(full).astype(np.float32), np.stack(first).astype(np.float32)


UNITS_PER_TRIP = 16


def _largest_divisor(n, cap):
    return max(u for u in range(1, cap + 1) if n % u == 0)


def _attn_prompt_kernel(q_ref, k_ref, v_ref, bfull_ref, bfirst_ref, o_ref, og_ref, lg_ref, *, rows_per_class):
    lane = lax.broadcasted_iota(jnp.int32, (1, LANES), 1)
    first_head = lane < HEAD_DIM

    def unit(g, dil, r_d, blk, full):
        jn = N_CLASSES // dil
        rn = BAND // jn
        classes = [r_d + dil * j for j in range(jn)]
        if full:
            m0 = pl.multiple_of(blk * rn, SUBLANES)
            q_rows = pl.ds(m0, rn)
            k_rows = pl.ds(pl.multiple_of(m0 - rn, SUBLANES), 2 * rn)
        else:
            assert blk == 0
            q_rows = pl.ds(0, rn)
            k_rows = pl.ds(0, min(2 * rn, rows_per_class))
        gather = lambda ref, rows: jnp.concatenate([ref[0, c, rows, :] for c in classes], axis=0)
        q = gather(q_ref, q_rows) * (HEAD_DIM ** -0.5)
        k = gather(k_ref, k_rows).astype(BF16)
        v = gather(v_ref, k_rows).astype(BF16)
        if k.shape[0] < 2 * BAND:
            assert jn == 1
            k = jnp.concatenate([k, k], axis=0)
            v = jnp.concatenate([v, v], axis=0)
        outs, lses = [], []
        for h in range(2):
            qh = jnp.where(first_head if h == 0 else jnp.logical_not(first_head), q, 0.0).astype(BF16)
            s = _dot_nt(qh, k) + (bfull_ref[g, h] if full else bfirst_ref[g, h])
            m = jnp.max(s, axis=-1, keepdims=True)
            p = jnp.exp(s - m)
            l = jnp.sum(p, axis=-1, keepdims=True)
            outs.append(_dot(p.astype(BF16), v) * (1.0 / l))
            lses.append(m + jnp.log(l))
        o = jnp.where(first_head, outs[0], outs[1])
        lse = jnp.broadcast_to(jnp.where(first_head, lses[0], lses[1]), o.shape)
        for j, c in enumerate(classes):
            og_ref[g, c, q_rows, :] = o[j * rn:(j + 1) * rn]
            lg_ref[g, c, q_rows, :] = lse[j * rn:(j + 1) * rn]

    for g, (window, dil) in enumerate(DIL_GROUPS):
        assert window // dil == BAND and N_CLASSES % dil == 0
        nblk = rows_per_class * N_CLASSES // (dil * BAND)
        if dil <= UNITS_PER_TRIP:
            for r in range(dil):
                unit(g, dil, r, 0, False)
        else:
            def body1(r, carry, g=g, dil=dil):
                unit(g, dil, r, 0, False)
                return carry
            lax.fori_loop(0, dil, body1, 0, unroll=_largest_divisor(dil, UNITS_PER_TRIP))
        rest = dil * (nblk - 1)
        if rest:
            def body(j, carry, g=g, dil=dil, nblk=nblk):
                unit(g, dil, j // (nblk - 1), j % (nblk - 1) + 1, True)
                return carry
            lax.fori_loop(0, rest, body, 0, unroll=_largest_divisor(rest, UNITS_PER_TRIP))

    def merge(c, carry):
        l0, l1, l2 = lg_ref[0, c], lg_ref[1, c], lg_ref[2, c]
        mx = jnp.maximum(jnp.maximum(l0, l1), l2)
        e0, e1, e2 = jnp.exp(l0 - mx), jnp.exp(l1 - mx), jnp.exp(l2 - mx)
        merged = (e0 * og_ref[0, c] + e1 * og_ref[1, c] + e2 * og_ref[2, c]) * (1.0 / (e0 + e1 + e2))
        o_ref[0, pl.ds(c, rows_per_class, stride=N_CLASSES), :] = merged
        return carry
    lax.fori_loop(0, N_CLASSES, merge, 0, unroll=2)


def _attn_prompt(q, k, v):
    b, _, rpc, _ = q.shape
    s = rpc * N_CLASSES
    bfull, bfirst = (jnp.asarray(t) for t in _prompt_bias_tables())
    ng = len(DIL_GROUPS)
    blk = lambda: pl.BlockSpec((1, N_CLASSES, rpc, LANES), lambda i, p: (i, 0, 0, p))
    est = 8 * s * LANES * 4 + 6 * s * LANES * 4 + 2 * ng * 2 * BAND * 3 * BAND * 4
    return pl.pallas_call(
        functools.partial(_attn_prompt_kernel, rows_per_class=rpc),
        grid=(b, MIX_W // LANES),
        in_specs=[blk(), blk(), blk(),
                  pl.BlockSpec((ng, 2, BAND, 2 * BAND), lambda i, p: (0, p, 0, 0)),
                  pl.BlockSpec((ng, 2, BAND, 2 * BAND), lambda i, p: (0, p, 0, 0))],
        out_specs=pl.BlockSpec((1, s, LANES), lambda i, p: (i, 0, p)),
        out_shape=jax.ShapeDtypeStruct((b, s, MIX_W), F32),
        scratch_shapes=[pltpu.VMEM((ng, N_CLASSES, rpc, LANES), F32) for _ in range(2)],
        compiler_params=_params(("parallel", "parallel"), est),
        name="attn_prompt",
    )(q, k, v, bfull, bfirst)


def _sample_bias_tables(wb, t_new):
    slopes = _alibi_slopes()

    def table(dist):
        mult = np.zeros(dist.shape)
        for window, dil in DIL_GROUPS:
            mult += (dist >= 0) & (dist <= window) & (dist % dil == 0)
        logm = np.where(mult > 0, np.log(np.maximum(mult, 1.0)), NEG)
        tab = -slopes[:, None, None] * dist[None].astype(np.float64) + logm[None]
        tab = np.where((mult > 0)[None], tab, NEG)
        return tab.reshape(N_HEADS * t_new, dist.shape[1]).astype(np.float32)

    t = np.arange(t_new)[:, None]
    return table(wb + t - np.arange(wb)[None, :]), table(t - np.arange(t_new)[None, :])


def _attn_sample_kernel(q_ref, kn_ref, vn_ref, wk_ref, wv_ref, bo_ref, bn_ref,
                        att_ref, ok_ref, ov_ref, *, wb, t_new):
    rows_q = N_HEADS * t_new
    q = q_ref[0] * (HEAD_DIM ** -0.5)
    qm = jnp.concatenate([q] * N_HEADS, axis=0)
    row_head = lax.broadcasted_iota(jnp.int32, (rows_q, MIX_W), 0) // t_new
    lane_head = lax.broadcasted_iota(jnp.int32, (rows_q, MIX_W), 1) // HEAD_DIM
    own = row_head == lane_head
    qm = jnp.where(own, qm, 0.0).astype(BF16)
    k_new, v_new = kn_ref[0], vn_ref[0]
    wk, wv = wk_ref[0], wv_ref[0]
    s_old = _dot(qm, wk.astype(BF16)) + bo_ref[...]
    s_new = _dot_nt(qm, k_new.astype(BF16)) + bn_ref[...]
    m = jnp.maximum(jnp.max(s_old, axis=-1, keepdims=True), jnp.max(s_new, axis=-1, keepdims=True))
    p_old = jnp.exp(s_old - m)
    p_new = jnp.exp(s_new - m)
    inv_l = 1.0 / (jnp.sum(p_old, axis=-1, keepdims=True) + jnp.sum(p_new, axis=-1, keepdims=True))
    o = _dot_nt(p_old.astype(BF16), wv.astype(BF16)) + _dot(p_new.astype(BF16), v_new.astype(BF16))
    o = jnp.where(own, o * inv_l, 0.0)
    att = o[0:t_new]
    for h in range(1, N_HEADS):
        att = att + o[h * t_new:(h + 1) * t_new]
    att_ref[0] = att
    lane = lax.broadcasted_iota(jnp.int32, (MIX_W, LANES), 1)
    is_new = lane >= LANES - t_new
    pad = jnp.zeros((LANES - t_new, MIX_W), F32)
    for w, new, o_ref in ((wk, k_new, ok_ref), (wv, v_new, ov_ref)):
        rolled = pltpu.roll(w, wb - t_new, axis=1)
        o_ref[0] = rolled
        new_t = jnp.concatenate([pad, new], axis=0).T
        o_ref[0, :, wb - LANES:wb] = jnp.where(is_new, new_t, rolled[:, wb - LANES:wb])


def _attn_sample(q, k_new, v_new, win_k_t, win_v_t):
    b, t_new, _ = q.shape
    wb = win_k_t.shape[2]
    assert wb >= DIL_GROUPS[-1][0] and t_new % SUBLANES == 0 and wb % LANES == 0
    b_old, b_new = _sample_bias_tables(wb, t_new)
    small = lambda: pl.BlockSpec((1, t_new, MIX_W), lambda i: (i, 0, 0))
    big = lambda: pl.BlockSpec((1, MIX_W, wb), lambda i: (i, 0, 0))
    est = 8 * wb * MIX_W * 4 + 2 * wb * MIX_W * 2 + 4 * N_HEADS * t_new * wb * 4
    return pl.pallas_call(
        functools.partial(_attn_sample_kernel, wb=wb, t_new=t_new),
        grid=(b,),
        in_specs=[small(), small(), small(), big(), big(),
                  pl.BlockSpec((N_HEADS * t_new, wb), lambda i: (0, 0)),
                  pl.BlockSpec((N_HEADS * t_new, t_new), lambda i: (0, 0))],
        out_specs=[small(), big(), big()],
        out_shape=[jax.ShapeDtypeStruct((b, t_new, MIX_W), F32),
                   jax.ShapeDtypeStruct((b, MIX_W, wb), F32),
                   jax.ShapeDtypeStruct((b, MIX_W, wb), F32)],
        compiler_params=_params(("parallel",), est),
        name="attn_sample",
    )(q, k_new, v_new, win_k_t, win_v_t, jnp.asarray(b_old), jnp.asarray(b_new))


ST_ROWS = HEAD_DIM + SUBLANES


def _mlstm_seq_kernel(qt_ref, k_ref, vt_ref, gcol_ref, mirow_ref, mfrow_ref, bcol_ref, bigrow_ref, bfgrow_ref,
                      h_ref, st_ref, m_ref, *, chunk):
    L = chunk
    hp = lax.Precision.HIGHEST

    @pl.when(pl.program_id(1) == 0)
    def _():
        st_ref[...] = jnp.zeros(st_ref.shape, F32)
        m_ref[...] = jnp.full(m_ref.shape, NEG, F32)

    ss = lax.broadcasted_iota(jnp.int32, (L, L), 0)
    tt = lax.broadcasted_iota(jnp.int32, (L, L), 1)
    causal = ss <= tt
    ig_r = mirow_ref[0] + bigrow_ref[...]
    lf_r = _log_sigmoid(mfrow_ref[0] + bfgrow_ref[...])
    b_r = jnp.dot(lf_r, causal.astype(F32), precision=hp, preferred_element_type=F32)
    lane = lax.broadcasted_iota(jnp.int32, (N_HEADS, L), 1)
    cm = ig_r - b_r
    shift = 1
    while shift < L:
        cm = jnp.maximum(cm, jnp.where(lane >= shift, pltpu.roll(cm, shift, axis=1), NEG))
        shift *= 2
    m_prev = m_ref[0][:, 0:1]
    m_t = b_r + jnp.maximum(m_prev, cm)
    row_all = b_r - m_t
    inter_w = jnp.exp(b_r + m_prev - m_t)
    floor = jnp.exp(-m_t)
    m_new = m_t[:, L - 1:L]
    b_last = b_r[:, L - 1:L]
    w_s = jnp.exp(b_last - b_r + ig_r - m_new)
    decay = jnp.exp(b_last + m_prev - m_new)
    pre_c = gcol_ref[0] + bcol_ref[...]
    bcum_c = jnp.dot((ss >= tt).astype(F32), _log_sigmoid(pre_c), precision=hp, preferred_element_type=F32)
    col_all = pre_c - pltpu.roll(bcum_c, GATE_PAD - N_HEADS, axis=1)

    ones_row = (lax.broadcasted_iota(jnp.int32, (SUBLANES, L), 0) == 0).astype(F32)
    zeros_q = jnp.zeros((HEAD_DIM, L), BF16)
    h_parts, new_state = [], []
    for h in range(N_HEADS):
        pair, half = divmod(h, 2)
        rows = slice(h * HEAD_DIM, (h + 1) * HEAD_DIM)
        q_own = qt_ref[0, rows, :].astype(BF16)
        q_pad = jnp.concatenate([q_own, zeros_q] if half == 0 else [zeros_q, q_own], axis=0)
        k_pair = k_ref[0, :, pair * LANES:(pair + 1) * LANES].astype(BF16)
        v_ext = jnp.concatenate([vt_ref[0, rows, :], ones_row], axis=0)
        st = st_ref[0, h]
        d = jnp.exp(jnp.where(causal, col_all[:, h:h + 1] + row_all[h:h + 1, :], NEG))
        a = (_dot(k_pair, q_pad) * d).astype(BF16)
        tot = _dot(v_ext.astype(BF16), a) + inter_w[h:h + 1, :] * _dot(st.astype(BF16), q_pad)
        den = tot[HEAD_DIM:HEAD_DIM + 1, :]
        h_parts.append(tot[0:HEAD_DIM, :] * (1.0 / jnp.maximum(jnp.abs(den), floor[h:h + 1, :])))
        new_state.append(decay[h:h + 1, :] * st + _dot((v_ext * w_s[h:h + 1, :]).astype(BF16), k_pair))
    h_ref[0] = jnp.concatenate(h_parts, axis=0).T
    for h, st_new in enumerate(new_state):
        st_ref[0, h] = st_new
    m_ref[0] = jnp.broadcast_to(m_new, (N_HEADS, LANES))


def _mlstm_seq(q_t, k, v_t, gates, b_ig, b_fg, chunk):
    b, s, _ = k.shape
    assert s % chunk == 0 and chunk % LANES == 0
    g_rows = jnp.swapaxes(gates[:, :, :2 * N_HEADS], 1, 2)
    bcol = jnp.concatenate([b_ig, b_fg, jnp.zeros((GATE_PAD - 2 * N_HEADS,), F32)]).reshape(1, GATE_PAD)
    ft = lambda: pl.BlockSpec((1, MIX_W, chunk), lambda i, c: (i, 0, c))
    row = lambda: pl.BlockSpec((1, N_HEADS, chunk), lambda i, c: (i, 0, c))
    const = lambda r, w: pl.BlockSpec((r, w), lambda i, c: (0, 0))
    est = 8 * chunk * MIX_W * 4 + 12 * chunk * chunk * 4 + 4 * N_HEADS * ST_ROWS * LANES * 4
    h, st, m = pl.pallas_call(
        functools.partial(_mlstm_seq_kernel, chunk=chunk),
        grid=(b, s // chunk),
        in_specs=[ft(), pl.BlockSpec((1, chunk, MIX_W), lambda i, c: (i, c, 0)), ft(),
                  pl.BlockSpec((1, chunk, GATE_PAD), lambda i, c: (i, c, 0)), row(), row(),
                  const(1, GATE_PAD), const(N_HEADS, 1), const(N_HEADS, 1)],
        out_specs=[pl.BlockSpec((1, chunk, MIX_W), lambda i, c: (i, c, 0)),
                   pl.BlockSpec((1, N_HEADS, ST_ROWS, LANES), lambda i, c: (i, 0, 0, 0)),
                   pl.BlockSpec((1, N_HEADS, LANES), lambda i, c: (i, 0, 0))],
        out_shape=[jax.ShapeDtypeStruct((b, s, MIX_W), F32),
                   jax.ShapeDtypeStruct((b, N_HEADS, ST_ROWS, LANES), F32),
                   jax.ShapeDtypeStruct((b, N_HEADS, LANES), F32)],
        compiler_params=_params(("arbitrary", "arbitrary"), est),
        name="mlstm_seq",
    )(q_t, k, v_t, gates, g_rows[:, :N_HEADS], g_rows[:, N_HEADS:], bcol, b_ig.reshape(-1, 1), b_fg.reshape(-1, 1))
    st = st.reshape(b, N_HEADS // 2, 2, ST_ROWS, 2, HEAD_DIM)
    st = jnp.stack([st[:, :, 0, :, 0, :], st[:, :, 1, :, 1, :]], axis=2).reshape(b, N_HEADS, ST_ROWS, HEAD_DIM)
    c = jnp.swapaxes(st[:, :, :HEAD_DIM, :], 2, 3)
    return h, c, st[:, :, HEAD_DIM, :], m[:, :, 0]


def _mlstm_step_kernel(xq_ref, xk_ref, v_ref, g_ref, sq_ref, sk_ref, cwq_ref, cwk_ref, cbq_ref, cbk_ref, gb_ref,
                       c0_ref, n0_ref, m0_ref, h_ref, c_ref, n_ref, m_ref, q_scr, ik_scr, *, t_new, batch):
    hist = CONV_W - 1
    tile = lambda ref, t: ref[pl.ds(t, batch, stride=t_new), :].T
    xq = [sq_ref[t].T for t in range(hist)] + [tile(xq_ref, t) for t in range(t_new)]
    xk = [sk_ref[t].T for t in range(hist)] + [tile(xk_ref, t) for t in range(t_new)]
    c_ref[...] = c0_ref[...]
    n = [n0_ref[e] for e in range(2)]
    m = [m0_ref[e] for e in range(2)]
    for t in range(t_new):
        cq, ck = cbq_ref[...], cbk_ref[...]
        for j in range(CONV_W):
            cq = cq + xq[t + j] * cwq_ref[j]
            ck = ck + xk[t + j] * cwk_ref[j]
        q_t = cq * _sigmoid(cq)
        k_t = ck * _sigmoid(ck) * (HEAD_DIM ** -0.5)
        v_t = tile(v_ref, t)
        gates = tile(g_ref, t)[0:2 * N_HEADS, :] + gb_ref[...]
        h_pair = []
        for e in range(2):
            head = 2 * pl.program_id(0) + e
            rows = slice(e * HEAD_DIM, (e + 1) * HEAD_DIM)
            ig = _gates_row(gates, head)
            lf = _log_sigmoid(_gates_row(gates, N_HEADS + head))
            m_new = jnp.maximum(lf + m[e], ig)
            f = jnp.exp(lf + m[e] - m_new)
            i = jnp.exp(ig - m_new)
            q_e, k_e, v_e = q_t[rows], k_t[rows], v_t[rows]
            q_scr[...] = q_e
            ik_scr[...] = k_e * i

            def body(kk, num, e=e, f=f, v_e=v_e):
                c_new = f * c_ref[e, kk] + ik_scr[pl.ds(kk, 1), :] * v_e
                c_ref[e, kk] = c_new
                return num + c_new * q_scr[pl.ds(kk, 1), :]
            num = lax.fori_loop(0, HEAD_DIM, body, jnp.zeros((HEAD_DIM, batch), F32), unroll=8)
            n[e] = f * n[e] + k_e * i
            den = jnp.sum(n[e] * q_e, axis=0, keepdims=True)
            h_pair.append(num * (1.0 / jnp.maximum(jnp.abs(den), jnp.exp(-m_new))))
            m[e] = m_new
        h_ref[pl.ds(t, batch, stride=t_new), :] = jnp.concatenate(h_pair, axis=0).T
    for e in range(2):
        n_ref[e] = n[e]
        m_ref[e] = m[e]


def _gates_row(gates, idx):
    sel = lax.broadcasted_iota(jnp.int32, gates.shape, 0) == idx
    return jnp.sum(jnp.where(sel, gates, 0.0), axis=0, keepdims=True)


def _mlstm_step(qk_pre, conv_state, conv_w, conv_b, mv, gates, b_ig, b_fg, c0, n0, m0, t_new):
    rows = qk_pre.shape[0]
    batch = rows // t_new
    assert batch == LANES and t_new % SUBLANES == 0
    hist = CONV_W - 1
    lanes_b = lambda a: jnp.broadcast_to(a[..., None], a.shape + (batch,))
    sc = jnp.swapaxes(conv_state, 0, 1)
    cw, cb = lanes_b(conv_w), lanes_b(conv_b)
    gb = lanes_b(jnp.concatenate([b_ig, b_fg]))
    c_t = jnp.transpose(c0, (1, 2, 3, 0))
    n_t = jnp.transpose(n0, (1, 2, 0))
    m_t = jnp.transpose(m0, (1, 0)).reshape(N_HEADS, 1, batch)
    npair = N_HEADS // 2
    colq = lambda: pl.BlockSpec((rows, LANES), lambda p: (0, p))
    colk = lambda: pl.BlockSpec((rows, LANES), lambda p: (0, npair + p))
    est = (8 * rows * LANES * 4 + 4 * 2 * HEAD_DIM * HEAD_DIM * batch * 4 + 2 * (hist + 2 * CONV_W + 2) * LANES * batch * 4
           + 4 * (hist + t_new) * LANES * batch * 4)
    h, c, n, m = pl.pallas_call(
        functools.partial(_mlstm_step_kernel, t_new=t_new, batch=batch),
        grid=(npair,),
        in_specs=[colq(), colk(), colq(), pl.BlockSpec((rows, GATE_PAD), lambda p: (0, 0)),
                  pl.BlockSpec((hist, batch, LANES), lambda p: (0, 0, p)),
                  pl.BlockSpec((hist, batch, LANES), lambda p: (0, 0, npair + p)),
                  pl.BlockSpec((CONV_W, LANES, batch), lambda p: (0, p, 0)),
                  pl.BlockSpec((CONV_W, LANES, batch), lambda p: (0, npair + p, 0)),
                  pl.BlockSpec((LANES, batch), lambda p: (p, 0)),
                  pl.BlockSpec((LANES, batch), lambda p: (npair + p, 0)),
                  pl.BlockSpec((2 * N_HEADS, batch), lambda p: (0, 0)),
                  pl.BlockSpec((2, HEAD_DIM, HEAD_DIM, batch), lambda p: (p, 0, 0, 0)),
                  pl.BlockSpec((2, HEAD_DIM, batch), lambda p: (p, 0, 0)),
                  pl.BlockSpec((2, 1, batch), lambda p: (p, 0, 0))],
        out_specs=[colq(),
                   pl.BlockSpec((2, HEAD_DIM, HEAD_DIM, batch), lambda p: (p, 0, 0, 0)),
                   pl.BlockSpec((2, HEAD_DIM, batch), lambda p: (p, 0, 0)),
                   pl.BlockSpec((2, 1, batch), lambda p: (p, 0, 0))],
        out_shape=[jax.ShapeDtypeStruct((rows, MIX_W), F32),
                   jax.ShapeDtypeStruct(c_t.shape, F32), jax.ShapeDtypeStruct(n_t.shape, F32),
                   jax.ShapeDtypeStruct(m_t.shape, F32)],
        scratch_shapes=[pltpu.VMEM((HEAD_DIM, batch), F32), pltpu.VMEM((HEAD_DIM, batch), F32)],
        compiler_params=_params(("arbitrary",), est),
        name="mlstm_step",
    )(qk_pre, qk_pre, mv, gates, sc, sc, cw, cw, cb, cb, gb, c_t, n_t, m_t)
    return (h, jnp.transpose(c, (3, 0, 1, 2)), jnp.transpose(n, (2, 0, 1)),
            jnp.transpose(m.reshape(N_HEADS, batch), (1, 0)))


def _mixers_to_output(x2, att, mlh, mo, mem_k, mem_v, wts, bt_x, tq_x, tm):
    b = mem_k.shape[0]
    if tq_x >= 2 * LANES:
        r3 = lambda a: a.reshape(b, -1, a.shape[-1])
        h2 = _xattn_block(r3(att), r3(mlh), r3(mo), r3(x2), wts, mem_k, mem_v, tq_x).reshape(-1, D_MODEL)
    else:
        h1 = _merge(att, mlh, mo, wts["b_og"], wts["g_attn_out"], wts["g_mlstm_out"], wts["w_out"], x2, tm)
        (xq,) = _norm_matmul(h1, wts["g_cross"], wts["w_cq"], (D_MODEL,), tm)
        xo = _xattn(xq.reshape(b, -1, D_MODEL), mem_k, mem_v, bt_x, tq_x)
        h2 = _matmul_res(xo.reshape(-1, D_MODEL), wts["w_co"], h1, tm)
    return _ffn(h2, wts["g_ffn"], wts["w_gate"], wts["w_up"], wts["w_down"], wts["g_final"], tm)


def kernel(x_prompt, x_sample, mem_prompt, cache_win_k, cache_win_v, cache_mem_k, cache_mem_v, state_conv, state_mlstm_C, state_mlstm_n, state_mlstm_m, g_mix, w_in, conv_w, conv_b, b_ig, b_fg, b_og, g_attn_out, g_mlstm_out, w_out, g_cross, g_mem, w_cq, w_ck, w_cv, w_co, g_ffn, w_gate, w_up, w_down, g_final):
    bp, sp, _ = x_prompt.shape
    bs, ts, _ = x_sample.shape
    wts = dict(b_og=b_og, g_attn_out=g_attn_out, g_mlstm_out=g_mlstm_out, g_cross=g_cross, g_ffn=g_ffn,
               g_final=g_final, w_out=w_out.astype(BF16), w_cq=w_cq.astype(BF16), w_co=w_co.astype(BF16),
               w_gate=w_gate.astype(BF16), w_up=w_up.astype(BF16), w_down=w_down.astype(BF16))
    w_in_p = jnp.pad(w_in, ((0, 0), (0, sum(IN_WIDTHS) - N_IN))).astype(BF16)
    w_ckv = jnp.concatenate([w_ck, w_cv], axis=1).astype(BF16)
    tm = 512

    xp2 = x_prompt.reshape(bp * sp, D_MODEL)
    aq_c, ak_c, av_c, ak_t, av_t, mq_t, mk, mv_t, mo, gates, conv_tail = _inproj_prompt(
        xp2, g_mix, w_in_p, conv_w, conv_b, tm, sp)
    r3 = lambda a: a.reshape(bp, sp, -1)
    att = _attn_prompt(aq_c, ak_c, av_c)
    keep = min(DIL_GROUPS[-1][0], sp)
    from_t = lambda a: jnp.transpose(a.reshape(a.shape[0], N_HEADS, HEAD_DIM, a.shape[2]), (0, 3, 1, 2))
    win_k_p = from_t(ak_t)[:, sp - keep:]
    win_v_p = from_t(av_t)[:, sp - keep:]
    mlh, c_p, n_p, m_p = _mlstm_seq(mq_t, r3(mk), mv_t, r3(gates), b_ig, b_fg, chunk=256)
    conv_p = conv_tail[:, SUBLANES - (CONV_W - 1):]
    mem_k_view, mem_v_view = _mem_proj(mem_prompt, g_mem, w_ckv)
    mem_k_p, mem_v_p = _mem_tile_unview(mem_k_view), _mem_tile_unview(mem_v_view)
    y_p = _mixers_to_output(xp2, att.reshape(-1, MIX_W), mlh.reshape(-1, MIX_W), mo, mem_k_view, mem_v_view, wts,
                            bt_x=1, tq_x=1024, tm=tm)

    xs2 = x_sample.reshape(bs * ts, D_MODEL)
    aq, ak, av, qk_pre, mv, mo, gates = _norm_matmul(xs2, g_mix, w_in_p, IN_WIDTHS, tm)
    r3 = lambda a: a.reshape(bs, ts, -1)
    wb = cache_win_k.shape[1]
    to_t = lambda a: jnp.transpose(a, (0, 2, 3, 1)).reshape(a.shape[0], MIX_W, a.shape[1])
    att, win_k_s, win_v_s = _attn_sample(r3(aq), r3(ak), r3(av), to_t(cache_win_k), to_t(cache_win_v))
    mlh, c_s, n_s, m_s = _mlstm_step(qk_pre, state_conv, conv_w, conv_b, mv, gates, b_ig, b_fg,
                                     state_mlstm_C, state_mlstm_n, state_mlstm_m, ts)
    conv_s = jnp.concatenate([state_conv, r3(qk_pre)], axis=1)[:, -(CONV_W - 1):]
    y_s = _mixers_to_output(xs2, att.reshape(-1, MIX_W), mlh, mo, _mem_tile_view(cache_mem_k),
                            _mem_tile_view(cache_mem_v), wts,
                            bt_x=4, tq_x=ts, tm=tm)

    return (y_p.reshape(bp, sp, D_MODEL), y_s.reshape(bs, ts, D_MODEL),
            win_k_p, win_v_p, conv_p, c_p, n_p, m_p, mem_k_p, mem_v_p,
            from_t(win_k_s), from_t(win_v_s), conv_s, c_s, n_s, m_s)
```

```python
import functools

import numpy as np
import jax
import jax.numpy as jnp
from jax import lax
from jax.experimental import pallas as pl
from jax.experimental.pallas import tpu as pltpu

F32 = jnp.float32
BF16 = jnp.bfloat16

D_MODEL = 1024
HEAD_DIM = 64
N_HEADS = 8
MIX_W = N_HEADS * HEAD_DIM
DIL_GROUPS = ((128, 1), (512, 4), (2048, 16))
BAND = 128
CONV_W = 4
N_MEM = 256
N_X_HEADS = 4
X_HEAD_DIM = D_MODEL // N_X_HEADS
D_FF = 2816
RMS_EPS = 1e-6
NEG = -1e30
N_IN = 3 * MIX_W + 2 * MIX_W + MIX_W + MIX_W + 2 * N_HEADS
GATE_PAD = 128

LANES = 128
SUBLANES = 8
VMEM_BYTES_V7X = 64 * 1024 * 1024


def _vmem_limit(nbytes):
    return int(min(max(2 * nbytes, 16 * 1024 * 1024), VMEM_BYTES_V7X - 8 * 1024 * 1024))


def _params(semantics, vmem_estimate):
    return pltpu.CompilerParams(dimension_semantics=semantics,
                                vmem_limit_bytes=_vmem_limit(vmem_estimate))


def _rms(x, g):
    return x * lax.rsqrt(jnp.mean(x * x, axis=-1, keepdims=True) + RMS_EPS) * g


def _sigmoid(x):
    return 1.0 / (1.0 + jnp.exp(-x))


def _log_sigmoid(x):
    return jnp.minimum(x, 0.0) - jnp.log(1.0 + jnp.exp(-jnp.abs(x)))


def _dot(a, b):
    return jnp.dot(a, b, preferred_element_type=F32)


def _dot_nt(a, b):
    return lax.dot_general(a, b, (((1,), (1,)), ((), ())), preferred_element_type=F32)


N_CLASSES = max(d for _, d in DIL_GROUPS)
IN_WIDTHS = (MIX_W, MIX_W, MIX_W, 2 * MIX_W, MIX_W, MIX_W, GATE_PAD)
IN_OFFS = tuple(int(o) for o in np.cumsum((0,) + IN_WIDTHS))


def _norm_matmul_kernel(x_ref, g_ref, w_ref, *o_refs, splits, w_transposed):
    xn = _rms(x_ref[...], g_ref[...]).astype(BF16)
    for o_ref, (a, b) in zip(o_refs, splits):
        o_ref[...] = _dot_nt(xn, w_ref[a:b, :]) if w_transposed else _dot(xn, w_ref[:, a:b])


def _norm_matmul(x, g, w, widths, tm, w_transposed=False):
    m, k = x.shape
    n = w.shape[0] if w_transposed else w.shape[1]
    assert sum(widths) == n and m % tm == 0 and w.shape == ((n, k) if w_transposed else (k, n))
    offs = np.cumsum([0] + list(widths))
    splits = tuple((int(offs[i]), int(offs[i + 1])) for i in range(len(widths)))
    est = 2 * tm * k * 4 + 2 * k * n * 2 + 2 * tm * n * 4
    return pl.pallas_call(
        functools.partial(_norm_matmul_kernel, splits=splits, w_transposed=w_transposed),
        grid=(m // tm,),
        in_specs=[pl.BlockSpec((tm, k), lambda i: (i, 0)),
                  pl.BlockSpec((1, k), lambda i: (0, 0)),
                  pl.BlockSpec(w.shape, lambda i: (0, 0))],
        out_specs=[pl.BlockSpec((tm, wd), lambda i: (i, 0)) for wd in widths],
        out_shape=[jax.ShapeDtypeStruct((m, wd), F32) for wd in widths],
        compiler_params=_params(("parallel",), est),
        name="norm_matmul",
    )(x, g.reshape(1, k), w)


def _inproj_prompt_kernel(x_ref, g_ref, w_ref, perm_ref, cw_ref, cb_ref,
                          qc_ref, kc_ref, vc_ref, kt_ref, vt_ref, mq_ref, mk_ref, mvt_ref, mo_ref, gate_ref, tail_ref,
                          xp_scr, *, per_seq):
    tm = x_ref.shape[0]

    @pl.when(pl.program_id(0) % per_seq == 0)
    def _():
        xp_scr[0:SUBLANES, :] = jnp.zeros((SUBLANES, 2 * MIX_W), F32)

    xn = _rms(x_ref[...], g_ref[...]).astype(BF16)
    proj = lambda idx: _dot_nt(xn, w_ref[IN_OFFS[idx]:IN_OFFS[idx + 1], :])

    xp_scr[SUBLANES:SUBLANES + tm, :] = proj(3)
    conv = cb_ref[...]
    for j in range(CONV_W):
        off = SUBLANES - (CONV_W - 1) + j
        conv = conv + xp_scr[off:off + tm, :] * cw_ref[j:j + 1, :]
    tail = xp_scr[tm:tm + SUBLANES, :]
    xp_scr[0:SUBLANES, :] = tail
    tail_ref[0] = tail
    act = conv * _sigmoid(conv)
    mq_ref[0] = act[:, :MIX_W].T
    mk_ref[...] = act[:, MIX_W:] * (HEAD_DIM ** -0.5)

    for idx, c_ref, t_ref in ((0, qc_ref, None), (1, kc_ref, kt_ref), (2, vc_ref, vt_ref)):
        y = proj(idx)
        if t_ref is not None:
            t_ref[0] = y.T
        y_cls = _dot(perm_ref[...], y.astype(BF16))
        c_ref[0] = y_cls.reshape(N_CLASSES, tm // N_CLASSES, MIX_W)

    mvt_ref[0] = _dot_nt(w_ref[IN_OFFS[4]:IN_OFFS[5], :], xn)
    mo_ref[...] = proj(5)
    gate_ref[...] = proj(6)


def _inproj_prompt(x, g, w_t, conv_w, conv_b, tm, seq):
    m, k = x.shape
    n = w_t.shape[0]
    assert n == IN_OFFS[-1] and m % seq == 0 and seq % tm == 0 and tm % (N_CLASSES * SUBLANES) == 0
    b, per_seq = m // seq, seq // tm
    est = (2 * tm * k * 4 + 2 * k * n * 2 + 2 * tm * (n + 4 * MIX_W) * 4 + 3 * tm * MIX_W * 4
           + 3 * (tm + SUBLANES) * 2 * MIX_W * 4)
    cls_spec = lambda: pl.BlockSpec((1, N_CLASSES, tm // N_CLASSES, MIX_W), lambda i: (i // per_seq, 0, i % per_seq, 0))
    cls_shape = jax.ShapeDtypeStruct((b, N_CLASSES, seq // N_CLASSES, MIX_W), F32)
    t_spec = lambda: pl.BlockSpec((1, MIX_W, tm), lambda i: (i // per_seq, 0, i % per_seq))
    t_shape = jax.ShapeDtypeStruct((b, MIX_W, seq), F32)
    nat_spec = lambda wd: pl.BlockSpec((tm, wd), lambda i: (i, 0))
    nat_shape = lambda wd: jax.ShapeDtypeStruct((m, wd), F32)
    cw = jnp.pad(conv_w, ((0, SUBLANES - CONV_W), (0, 0)))
    dst = np.arange(tm)
    perm = np.zeros((tm, tm), np.float32)
    perm[dst, (dst % (tm // N_CLASSES)) * N_CLASSES + dst // (tm // N_CLASSES)] = 1.0
    return pl.pallas_call(
        functools.partial(_inproj_prompt_kernel, per_seq=per_seq),
        grid=(m // tm,),
        in_specs=[pl.BlockSpec((tm, k), lambda i: (i, 0)),
                  pl.BlockSpec((1, k), lambda i: (0, 0)),
                  pl.BlockSpec((n, k), lambda i: (0, 0)),
                  pl.BlockSpec((tm, tm), lambda i: (0, 0)),
                  pl.BlockSpec((SUBLANES, 2 * MIX_W), lambda i: (0, 0)),
                  pl.BlockSpec((1, 2 * MIX_W), lambda i: (0, 0))],
        out_specs=[cls_spec(), cls_spec(), cls_spec(), t_spec(), t_spec(), t_spec(), nat_spec(MIX_W), t_spec(),
                   nat_spec(MIX_W), nat_spec(GATE_PAD),
                   pl.BlockSpec((1, SUBLANES, 2 * MIX_W), lambda i: (i // per_seq, 0, 0))],
        out_shape=[cls_shape, cls_shape, cls_shape, t_shape, t_shape, t_shape, nat_shape(MIX_W), t_shape,
                   nat_shape(MIX_W), nat_shape(GATE_PAD),
                   jax.ShapeDtypeStruct((b, SUBLANES, 2 * MIX_W), F32)],
        scratch_shapes=[pltpu.VMEM((tm + SUBLANES, 2 * MIX_W), F32)],
        compiler_params=_params(("arbitrary",), est),
        name="inproj_prompt",
    )(x, g.reshape(1, k), w_t, jnp.asarray(perm, BF16), cw, conv_b.reshape(1, -1))


def _matmul_res_kernel(x_ref, w_ref, r_ref, o_ref):
    o_ref[...] = r_ref[...] + _dot(x_ref[...].astype(BF16), w_ref[...])


def _matmul_res(x, w, res, tm):
    m, k = x.shape
    n = w.shape[1]
    est = 2 * tm * k * 4 + 2 * k * n * 2 + 4 * tm * n * 4
    return pl.pallas_call(
        _matmul_res_kernel,
        grid=(m // tm,),
        in_specs=[pl.BlockSpec((tm, k), lambda i: (i, 0)),
                  pl.BlockSpec((k, n), lambda i: (0, 0)),
                  pl.BlockSpec((tm, n), lambda i: (i, 0))],
        out_specs=pl.BlockSpec((tm, n), lambda i: (i, 0)),
        out_shape=jax.ShapeDtypeStruct((m, n), F32),
        compiler_params=_params(("parallel",), est),
        name="matmul_res",
    )(x, w, res)


def _merge_kernel(att_ref, mlh_ref, mo_ref, bog_ref, ga_ref, gm_ref, w_ref, x_ref, o_ref):
    a_n = _rms(att_ref[...], ga_ref[...]).astype(BF16)
    ml = _sigmoid(mo_ref[...] + bog_ref[...]) * mlh_ref[...]
    m_n = _rms(ml, gm_ref[...]).astype(BF16)
    o_ref[...] = x_ref[...] + _dot(a_n, w_ref[0:MIX_W, :]) + _dot(m_n, w_ref[MIX_W:2 * MIX_W, :])


def _merge(att, mlh, mo, b_og, g_a, g_m, w_out, x, tm):
    m = x.shape[0]
    half = lambda: pl.BlockSpec((tm, MIX_W), lambda i: (i, 0))
    vec = lambda: pl.BlockSpec((1, MIX_W), lambda i: (0, 0))
    est = 6 * tm * MIX_W * 4 + 2 * 2 * MIX_W * D_MODEL * 2 + 4 * tm * D_MODEL * 4
    return pl.pallas_call(
        _merge_kernel,
        grid=(m // tm,),
        in_specs=[half(), half(), half(), vec(), vec(), vec(),
                  pl.BlockSpec((2 * MIX_W, D_MODEL), lambda i: (0, 0)),
                  pl.BlockSpec((tm, D_MODEL), lambda i: (i, 0))],
        out_specs=pl.BlockSpec((tm, D_MODEL), lambda i: (i, 0)),
        out_shape=jax.ShapeDtypeStruct((m, D_MODEL), F32),
        compiler_params=_params(("parallel",), est),
        name="merge",
    )(att, mlh, mo, b_og.reshape(1, -1), g_a.reshape(1, -1), g_m.reshape(1, -1), w_out, x)


def _ffn_kernel(h_ref, gf_ref, wg_ref, wu_ref, wd_ref, gl_ref, o_ref):
    h = h_ref[...]
    hn = _rms(h, gf_ref[...]).astype(BF16)
    gate = _dot(hn, wg_ref[...])
    up = _dot(hn, wu_ref[...])
    act = (gate * _sigmoid(gate) * up).astype(BF16)
    o_ref[...] = _rms(h + _dot(act, wd_ref[...]), gl_ref[...])


def _ffn(h, g_ffn, w_gate, w_up, w_down, g_final, tm):
    m = h.shape[0]
    est = 4 * tm * D_MODEL * 4 + 2 * 3 * D_MODEL * D_FF * 2 + 3 * tm * D_FF * 4
    return pl.pallas_call(
        _ffn_kernel,
        grid=(m // tm,),
        in_specs=[pl.BlockSpec((tm, D_MODEL), lambda i: (i, 0)),
                  pl.BlockSpec((1, D_MODEL), lambda i: (0, 0)),
                  pl.BlockSpec((D_MODEL, D_FF), lambda i: (0, 0)),
                  pl.BlockSpec((D_MODEL, D_FF), lambda i: (0, 0)),
                  pl.BlockSpec((D_FF, D_MODEL), lambda i: (0, 0)),
                  pl.BlockSpec((1, D_MODEL), lambda i: (0, 0))],
        out_specs=pl.BlockSpec((tm, D_MODEL), lambda i: (i, 0)),
        out_shape=jax.ShapeDtypeStruct((m, D_MODEL), F32),
        compiler_params=_params(("parallel",), est),
        name="ffn",
    )(h, g_ffn.reshape(1, -1), w_gate, w_up, w_down, g_final.reshape(1, -1))


X_HALVES = X_HEAD_DIM // LANES
X_ROWS = N_X_HEADS * X_HALVES


def _mem_tile_view(a):
    b = a.shape[0]
    a = a.reshape(b, N_MEM, N_X_HEADS, X_HALVES, LANES)
    return jnp.transpose(a, (0, 1, 3, 2, 4)).reshape(b, N_MEM * X_ROWS, LANES)


def _mem_tile_unview(a):
    b = a.shape[0]
    a = a.reshape(b, N_MEM, X_HALVES, N_X_HEADS, LANES)
    return jnp.transpose(a, (0, 1, 3, 2, 4)).reshape(b, N_MEM, N_X_HEADS, X_HEAD_DIM)


def _mem_proj_kernel(x_ref, g_ref, w_ref, k_ref, v_ref):
    xn = _rms(x_ref[0], g_ref[...]).astype(BF16)
    for idx, o_ref in enumerate((k_ref, v_ref)):
        y = _dot(xn, w_ref[:, idx * D_MODEL:(idx + 1) * D_MODEL])
        for h in range(N_X_HEADS):
            for j in range(X_HALVES):
                lanes = slice(h * X_HEAD_DIM + j * LANES, h * X_HEAD_DIM + (j + 1) * LANES)
                o_ref[0, pl.ds(j * N_X_HEADS + h, N_MEM, stride=X_ROWS), :] = y[:, lanes]


def _mem_proj(mem, g, w_ckv):
    b = mem.shape[0]
    view = jax.ShapeDtypeStruct((b, N_MEM * X_ROWS, LANES), F32)
    est = 2 * N_MEM * D_MODEL * 4 + 2 * D_MODEL * 2 * D_MODEL * 2 + 6 * N_MEM * D_MODEL * 4
    return pl.pallas_call(
        _mem_proj_kernel,
        grid=(b,),
        in_specs=[pl.BlockSpec((1, N_MEM, D_MODEL), lambda i: (i, 0, 0)),
                  pl.BlockSpec((1, D_MODEL), lambda i: (0, 0)),
                  pl.BlockSpec((D_MODEL, 2 * D_MODEL), lambda i: (0, 0))],
        out_specs=[pl.BlockSpec((1, N_MEM * X_ROWS, LANES), lambda i: (i, 0, 0)) for _ in range(2)],
        out_shape=[view, view],
        compiler_params=_params(("parallel",), est),
        name="mem_proj",
    )(mem, g.reshape(1, -1), w_ckv)


def _mem_head(ref, bi, h):
    halves = [ref[bi, pl.ds(j * N_X_HEADS + h, N_MEM, stride=X_ROWS), :] for j in range(X_HALVES)]
    return jnp.concatenate(halves, axis=1).astype(BF16)


def _xattn_kernel(q_ref, k_ref, v_ref, o_ref, *, bt):
    for bi in range(bt):
        for h in range(N_X_HEADS):
            sl = slice(h * X_HEAD_DIM, (h + 1) * X_HEAD_DIM)
            q = (q_ref[bi, :, sl] * (X_HEAD_DIM ** -0.5)).astype(BF16)
            s = _dot_nt(q, _mem_head(k_ref, bi, h))
            p = jnp.exp(s - jnp.max(s, axis=-1, keepdims=True))
            inv_l = 1.0 / jnp.sum(p, axis=-1, keepdims=True)
            o_ref[bi, :, sl] = _dot(p.astype(BF16), _mem_head(v_ref, bi, h)) * inv_l


def _xattn(q, mem_k, mem_v, bt, tq):
    b, t, _ = q.shape
    est = 4 * bt * tq * D_MODEL * 4 + 4 * bt * N_MEM * D_MODEL * 4
    kv = lambda: pl.BlockSpec((bt, N_MEM * X_ROWS, LANES), lambda i, j: (i, 0, 0))
    return pl.pallas_call(
        functools.partial(_xattn_kernel, bt=bt),
        grid=(b // bt, t // tq),
        in_specs=[pl.BlockSpec((bt, tq, D_MODEL), lambda i, j: (i, j, 0)), kv(), kv()],
        out_specs=pl.BlockSpec((bt, tq, D_MODEL), lambda i, j: (i, j, 0)),
        out_shape=jax.ShapeDtypeStruct((b, t, D_MODEL), F32),
        compiler_params=_params(("parallel", "parallel"), est),
        name="xattn",
    )(q, mem_k, mem_v)


def _xattn_block_kernel(att_ref, mlh_ref, mo_ref, x_ref, bog_ref, ga_ref, gm_ref, wout_ref,
                        g_ref, wq_ref, k_ref, v_ref, wo_ref, o_ref):
    a_n = _rms(att_ref[0], ga_ref[...]).astype(BF16)
    m_n = _rms(_sigmoid(mo_ref[0] + bog_ref[...]) * mlh_ref[0], gm_ref[...]).astype(BF16)
    h1 = x_ref[0] + _dot(a_n, wout_ref[0:MIX_W, :]) + _dot(m_n, wout_ref[MIX_W:2 * MIX_W, :])
    q_all = _dot(_rms(h1, g_ref[...]).astype(BF16), wq_ref[...]) * (X_HEAD_DIM ** -0.5)
    o_parts = []
    for h in range(N_X_HEADS):
        sl = slice(h * X_HEAD_DIM, (h + 1) * X_HEAD_DIM)
        s = _dot_nt(q_all[:, sl].astype(BF16), _mem_head(k_ref, 0, h))
        p = jnp.exp(s - jnp.max(s, axis=-1, keepdims=True))
        inv_l = 1.0 / jnp.sum(p, axis=-1, keepdims=True)
        o_parts.append((_dot(p.astype(BF16), _mem_head(v_ref, 0, h)) * inv_l).astype(BF16))
    o_ref[0] = h1 + _dot(jnp.concatenate(o_parts, axis=1), wo_ref[...])


def _xattn_block(att, mlh, mo, x, wts, mem_k, mem_v, tq):
    b, t, _ = x.shape
    est = (6 * tq * MIX_W * 4 + 8 * tq * D_MODEL * 4 + 4 * N_MEM * D_MODEL * 4 + 2 * 3 * D_MODEL * D_MODEL * 2)
    kv = lambda: pl.BlockSpec((1, N_MEM * X_ROWS, LANES), lambda i, j: (i, 0, 0))
    wspec = lambda: pl.BlockSpec((D_MODEL, D_MODEL), lambda i, j: (0, 0))
    half = lambda: pl.BlockSpec((1, tq, MIX_W), lambda i, j: (i, j, 0))
    full = lambda: pl.BlockSpec((1, tq, D_MODEL), lambda i, j: (i, j, 0))
    vec = lambda w: pl.BlockSpec((1, w), lambda i, j: (0, 0))
    row = lambda a: a.reshape(1, -1)
    return pl.pallas_call(
        _xattn_block_kernel,
        grid=(b, t // tq),
        in_specs=[half(), half(), half(), full(), vec(MIX_W), vec(MIX_W), vec(MIX_W), wspec(),
                  vec(D_MODEL), wspec(), kv(), kv(), wspec()],
        out_specs=full(),
        out_shape=jax.ShapeDtypeStruct((b, t, D_MODEL), F32),
        compiler_params=_params(("parallel", "parallel"), est),
        name="xattn_block",
    )(att, mlh, mo, x, row(wts["b_og"]), row(wts["g_attn_out"]), row(wts["g_mlstm_out"]), wts["w_out"],
      row(wts["g_cross"]), wts["w_cq"], mem_k, mem_v, wts["w_co"])


def _alibi_slopes():
    return np.exp2(-8.0 * (np.arange(N_HEADS, dtype=np.float64) + 1.0) / N_HEADS)


def _prompt_bias_tables():
    full, first = [], []
    for _, dil in DIL_GROUPS:
        jn = N_CLASSES // dil
        rn = BAND // jn
        qj, qm = np.divmod(np.arange(BAND), rn)
        kj, km = np.divmod(np.arange(2 * BAND), 2 * rn)
        for tabs, row_offset in ((full, rn), (first, 0)):
            delta = jn * (row_offset + qm[:, None] - km[None, :]) + (qj[:, None] - kj[None, :])
            valid = (delta >= 0) & (delta <= BAND)
            bias = -_alibi_slopes()[:, None, None] * (delta * dil)[None].astype(np.float64)
            tabs.append(np.where(valid[None], bias, NEG))
    return np.stack(full).astype(np.float32), np.stack(first).astype(np.float32)


UNITS_PER_TRIP = 16


def _largest_divisor(n, cap):
    return max(u for u in range(1, cap + 1) if n % u == 0)


def _attn_prompt_kernel(q_ref, k_ref, v_ref, bfull_ref, bfirst_ref, o_ref, og_ref, lg_ref, *, rows_per_class):
    lane = lax.broadcasted_iota(jnp.int32, (1, LANES), 1)
    first_head = lane < HEAD_DIM

    def unit(g, dil, r_d, blk, full):
        jn = N_CLASSES // dil
        rn = BAND // jn
        classes = [r_d + dil * j for j in range(jn)]
        if full:
            m0 = pl.multiple_of(blk * rn, SUBLANES)
            q_rows = pl.ds(m0, rn)
            k_rows = pl.ds(pl.multiple_of(m0 - rn, SUBLANES), 2 * rn)
        else:
            assert blk == 0
            q_rows = pl.ds(0, rn)
            k_rows = pl.ds(0, min(2 * rn, rows_per_class))
        gather = lambda ref, rows: jnp.concatenate([ref[0, c, rows, :] for c in classes], axis=0)
        q = gather(q_ref, q_rows) * (HEAD_DIM ** -0.5)
        k = gather(k_ref, k_rows).astype(BF16)
        v = gather(v_ref, k_rows).astype(BF16)
        if k.shape[0] < 2 * BAND:
            assert jn == 1
            k = jnp.concatenate([k, k], axis=0)
            v = jnp.concatenate([v, v], axis=0)
        outs, lses = [], []
        for h in range(2):
            qh = jnp.where(first_head if h == 0 else jnp.logical_not(first_head), q, 0.0).astype(BF16)
            s = _dot_nt(qh, k) + (bfull_ref[g, h] if full else bfirst_ref[g, h])
            m = jnp.max(s, axis=-1, keepdims=True)
            p = jnp.exp(s - m)
            l = jnp.sum(p, axis=-1, keepdims=True)
            outs.append(_dot(p.astype(BF16), v) * (1.0 / l))
            lses.append(m + jnp.log(l))
        o = jnp.where(first_head, outs[0], outs[1])
        lse = jnp.broadcast_to(jnp.where(first_head, lses[0], lses[1]), o.shape)
        for j, c in enumerate(classes):
            og_ref[g, c, q_rows, :] = o[j * rn:(j + 1) * rn]
            lg_ref[g, c, q_rows, :] = lse[j * rn:(j + 1) * rn]

    for g, (window, dil) in enumerate(DIL_GROUPS):
        assert window // dil == BAND and N_CLASSES % dil == 0
        nblk = rows_per_class * N_CLASSES // (dil * BAND)
        if dil <= UNITS_PER_TRIP:
            for r in range(dil):
                unit(g, dil, r, 0, False)
        else:
            def body1(r, carry, g=g, dil=dil):
                unit(g, dil, r, 0, False)
                return carry
            lax.fori_loop(0, dil, body1, 0, unroll=_largest_divisor(dil, UNITS_PER_TRIP))
        rest = dil * (nblk - 1)
        if rest:
            def body(j, carry, g=g, dil=dil, nblk=nblk):
                unit(g, dil, j // (nblk - 1), j % (nblk - 1) + 1, True)
                return carry
            lax.fori_loop(0, rest, body, 0, unroll=_largest_divisor(rest, UNITS_PER_TRIP))

    def merge(c, carry):
        l0, l1, l2 = lg_ref[0, c], lg_ref[1, c], lg_ref[2, c]
        mx = jnp.maximum(jnp.maximum(l0, l1), l2)
        e0, e1, e2 = jnp.exp(l0 - mx), jnp.exp(l1 - mx), jnp.exp(l2 - mx)
        merged = (e0 * og_ref[0, c] + e1 * og_ref[1, c] + e2 * og_ref[2, c]) * (1.0 / (e0 + e1 + e2))
        o_ref[0, pl.ds(c, rows_per_class, stride=N_CLASSES), :] = merged
        return carry
    lax.fori_loop(0, N_CLASSES, merge, 0, unroll=2)


def _attn_prompt(q, k, v):
    b, _, rpc, _ = q.shape
    s = rpc * N_CLASSES
    bfull, bfirst = (jnp.asarray(t) for t in _prompt_bias_tables())
    ng = len(DIL_GROUPS)
    blk = lambda: pl.BlockSpec((1, N_CLASSES, rpc, LANES), lambda i, p: (i, 0, 0, p))
    est = 8 * s * LANES * 4 + 6 * s * LANES * 4 + 2 * ng * 2 * BAND * 3 * BAND * 4
    return pl.pallas_call(
        functools.partial(_attn_prompt_kernel, rows_per_class=rpc),
        grid=(b, MIX_W // LANES),
        in_specs=[blk(), blk(), blk(),
                  pl.BlockSpec((ng, 2, BAND, 2 * BAND), lambda i, p: (0, p, 0, 0)),
                  pl.BlockSpec((ng, 2, BAND, 2 * BAND), lambda i, p: (0, p, 0, 0))],
        out_specs=pl.BlockSpec((1, s, LANES), lambda i, p: (i, 0, p)),
        out_shape=jax.ShapeDtypeStruct((b, s, MIX_W), F32),
        scratch_shapes=[pltpu.VMEM((ng, N_CLASSES, rpc, LANES), F32) for _ in range(2)],
        compiler_params=_params(("parallel", "parallel"), est),
        name="attn_prompt",
    )(q, k, v, bfull, bfirst)


def _sample_bias_tables(wb, t_new):
    slopes = _alibi_slopes()

    def table(dist):
        mult = np.zeros(dist.shape)
        for window, dil in DIL_GROUPS:
            mult += (dist >= 0) & (dist <= window) & (dist % dil == 0)
        logm = np.where(mult > 0, np.log(np.maximum(mult, 1.0)), NEG)
        tab = -slopes[:, None, None] * dist[None].astype(np.float64) + logm[None]
        tab = np.where((mult > 0)[None], tab, NEG)
        return tab.reshape(N_HEADS * t_new, dist.shape[1]).astype(np.float32)

    t = np.arange(t_new)[:, None]
    return table(wb + t - np.arange(wb)[None, :]), table(t - np.arange(t_new)[None, :])


def _attn_sample_kernel(q_ref, kn_ref, vn_ref, wk_ref, wv_ref, bo_ref, bn_ref,
                        att_ref, ok_ref, ov_ref, *, wb, t_new):
    rows_q = N_HEADS * t_new
    q = q_ref[0] * (HEAD_DIM ** -0.5)
    qm = jnp.concatenate([q] * N_HEADS, axis=0)
    row_head = lax.broadcasted_iota(jnp.int32, (rows_q, MIX_W), 0) // t_new
    lane_head = lax.broadcasted_iota(jnp.int32, (rows_q, MIX_W), 1) // HEAD_DIM
    own = row_head == lane_head
    qm = jnp.where(own, qm, 0.0).astype(BF16)
    k_new, v_new = kn_ref[0], vn_ref[0]
    wk, wv = wk_ref[0], wv_ref[0]
    s_old = _dot(qm, wk.astype(BF16)) + bo_ref[...]
    s_new = _dot_nt(qm, k_new.astype(BF16)) + bn_ref[...]
    m = jnp.maximum(jnp.max(s_old, axis=-1, keepdims=True), jnp.max(s_new, axis=-1, keepdims=True))
    p_old = jnp.exp(s_old - m)
    p_new = jnp.exp(s_new - m)
    inv_l = 1.0 / (jnp.sum(p_old, axis=-1, keepdims=True) + jnp.sum(p_new, axis=-1, keepdims=True))
    o = _dot_nt(p_old.astype(BF16), wv.astype(BF16)) + _dot(p_new.astype(BF16), v_new.astype(BF16))
    o = jnp.where(own, o * inv_l, 0.0)
    att = o[0:t_new]
    for h in range(1, N_HEADS):
        att = att + o[h * t_new:(h + 1) * t_new]
    att_ref[0] = att
    lane = lax.broadcasted_iota(jnp.int32, (MIX_W, LANES), 1)
    is_new = lane >= LANES - t_new
    pad = jnp.zeros((LANES - t_new, MIX_W), F32)
    for w, new, o_ref in ((wk, k_new, ok_ref), (wv, v_new, ov_ref)):
        rolled = pltpu.roll(w, wb - t_new, axis=1)
        o_ref[0] = rolled
        new_t = jnp.concatenate([pad, new], axis=0).T
        o_ref[0, :, wb - LANES:wb] = jnp.where(is_new, new_t, rolled[:, wb - LANES:wb])


def _attn_sample(q, k_new, v_new, win_k_t, win_v_t):
    b, t_new, _ = q.shape
    wb = win_k_t.shape[2]
    assert wb >= DIL_GROUPS[-1][0] and t_new % SUBLANES == 0 and wb % LANES == 0
    b_old, b_new = _sample_bias_tables(wb, t_new)
    small = lambda: pl.BlockSpec((1, t_new, MIX_W), lambda i: (i, 0, 0))
    big = lambda: pl.BlockSpec((1, MIX_W, wb), lambda i: (i, 0, 0))
    est = 8 * wb * MIX_W * 4 + 2 * wb * MIX_W * 2 + 4 * N_HEADS * t_new * wb * 4
    return pl.pallas_call(
        functools.partial(_attn_sample_kernel, wb=wb, t_new=t_new),
        grid=(b,),
        in_specs=[small(), small(), small(), big(), big(),
                  pl.BlockSpec((N_HEADS * t_new, wb), lambda i: (0, 0)),
                  pl.BlockSpec((N_HEADS * t_new, t_new), lambda i: (0, 0))],
        out_specs=[small(), big(), big()],
        out_shape=[jax.ShapeDtypeStruct((b, t_new, MIX_W), F32),
                   jax.ShapeDtypeStruct((b, MIX_W, wb), F32),
                   jax.ShapeDtypeStruct((b, MIX_W, wb), F32)],
        compiler_params=_params(("parallel",), est),
        name="attn_sample",
    )(q, k_new, v_new, win_k_t, win_v_t, jnp.asarray(b_old), jnp.asarray(b_new))


ST_ROWS = HEAD_DIM + SUBLANES


def _mlstm_seq_kernel(qt_ref, k_ref, vt_ref, gcol_ref, mirow_ref, mfrow_ref, bcol_ref, bigrow_ref, bfgrow_ref,
                      h_ref, st_ref, m_ref, *, chunk):
    L = chunk
    hp = lax.Precision.HIGHEST

    @pl.when(pl.program_id(1) == 0)
    def _():
        st_ref[...] = jnp.zeros(st_ref.shape, F32)
        m_ref[...] = jnp.full(m_ref.shape, NEG, F32)

    ss = lax.broadcasted_iota(jnp.int32, (L, L), 0)
    tt = lax.broadcasted_iota(jnp.int32, (L, L), 1)
    causal = ss <= tt
    ig_r = mirow_ref[0] + bigrow_ref[...]
    lf_r = _log_sigmoid(mfrow_ref[0] + bfgrow_ref[...])
    b_r = jnp.dot(lf_r, causal.astype(F32), precision=hp, preferred_element_type=F32)
    lane = lax.broadcasted_iota(jnp.int32, (N_HEADS, L), 1)
    cm = ig_r - b_r
    shift = 1
    while shift < L:
        cm = jnp.maximum(cm, jnp.where(lane >= shift, pltpu.roll(cm, shift, axis=1), NEG))
        shift *= 2
    m_prev = m_ref[0][:, 0:1]
    m_t = b_r + jnp.maximum(m_prev, cm)
    row_all = b_r - m_t
    inter_w = jnp.exp(b_r + m_prev - m_t)
    floor = jnp.exp(-m_t)
    m_new = m_t[:, L - 1:L]
    b_last = b_r[:, L - 1:L]
    w_s = jnp.exp(b_last - b_r + ig_r - m_new)
    decay = jnp.exp(b_last + m_prev - m_new)
    pre_c = gcol_ref[0] + bcol_ref[...]
    bcum_c = jnp.dot((ss >= tt).astype(F32), _log_sigmoid(pre_c), precision=hp, preferred_element_type=F32)
    col_all = pre_c - pltpu.roll(bcum_c, GATE_PAD - N_HEADS, axis=1)

    ones_row = (lax.broadcasted_iota(jnp.int32, (SUBLANES, L), 0) == 0).astype(F32)
    zeros_q = jnp.zeros((HEAD_DIM, L), BF16)
    h_parts, new_state = [], []
    for h in range(N_HEADS):
        pair, half = divmod(h, 2)
        rows = slice(h * HEAD_DIM, (h + 1) * HEAD_DIM)
        q_own = qt_ref[0, rows, :].astype(BF16)
        q_pad = jnp.concatenate([q_own, zeros_q] if half == 0 else [zeros_q, q_own], axis=0)
        k_pair = k_ref[0, :, pair * LANES:(pair + 1) * LANES].astype(BF16)
        v_ext = jnp.concatenate([vt_ref[0, rows, :], ones_row], axis=0)
        st = st_ref[0, h]
        d = jnp.exp(jnp.where(causal, col_all[:, h:h + 1] + row_all[h:h + 1, :], NEG))
        a = (_dot(k_pair, q_pad) * d).astype(BF16)
        tot = _dot(v_ext.astype(BF16), a) + inter_w[h:h + 1, :] * _dot(st.astype(BF16), q_pad)
        den = tot[HEAD_DIM:HEAD_DIM + 1, :]
        h_parts.append(tot[0:HEAD_DIM, :] * (1.0 / jnp.maximum(jnp.abs(den), floor[h:h + 1, :])))
        new_state.append(decay[h:h + 1, :] * st + _dot((v_ext * w_s[h:h + 1, :]).astype(BF16), k_pair))
    h_ref[0] = jnp.concatenate(h_parts, axis=0).T
    for h, st_new in enumerate(new_state):
        st_ref[0, h] = st_new
    m_ref[0] = jnp.broadcast_to(m_new, (N_HEADS, LANES))


def _mlstm_seq(q_t, k, v_t, gates, b_ig, b_fg, chunk):
    b, s, _ = k.shape
    assert s % chunk == 0 and chunk % LANES == 0
    g_rows = jnp.swapaxes(gates[:, :, :2 * N_HEADS], 1, 2)
    bcol = jnp.concatenate([b_ig, b_fg, jnp.zeros((GATE_PAD - 2 * N_HEADS,), F32)]).reshape(1, GATE_PAD)
    ft = lambda: pl.BlockSpec((1, MIX_W, chunk), lambda i, c: (i, 0, c))
    row = lambda: pl.BlockSpec((1, N_HEADS, chunk), lambda i, c: (i, 0, c))
    const = lambda r, w: pl.BlockSpec((r, w), lambda i, c: (0, 0))
    est = 8 * chunk * MIX_W * 4 + 12 * chunk * chunk * 4 + 4 * N_HEADS * ST_ROWS * LANES * 4
    h, st, m = pl.pallas_call(
        functools.partial(_mlstm_seq_kernel, chunk=chunk),
        grid=(b, s // chunk),
        in_specs=[ft(), pl.BlockSpec((1, chunk, MIX_W), lambda i, c: (i, c, 0)), ft(),
                  pl.BlockSpec((1, chunk, GATE_PAD), lambda i, c: (i, c, 0)), row(), row(),
                  const(1, GATE_PAD), const(N_HEADS, 1), const(N_HEADS, 1)],
        out_specs=[pl.BlockSpec((1, chunk, MIX_W), lambda i, c: (i, c, 0)),
                   pl.BlockSpec((1, N_HEADS, ST_ROWS, LANES), lambda i, c: (i, 0, 0, 0)),
                   pl.BlockSpec((1, N_HEADS, LANES), lambda i, c: (i, 0, 0))],
        out_shape=[jax.ShapeDtypeStruct((b, s, MIX_W), F32),
                   jax.ShapeDtypeStruct((b, N_HEADS, ST_ROWS, LANES), F32),
                   jax.ShapeDtypeStruct((b, N_HEADS, LANES), F32)],
        compiler_params=_params(("arbitrary", "arbitrary"), est),
        name="mlstm_seq",
    )(q_t, k, v_t, gates, g_rows[:, :N_HEADS], g_rows[:, N_HEADS:], bcol, b_ig.reshape(-1, 1), b_fg.reshape(-1, 1))
    st = st.reshape(b, N_HEADS // 2, 2, ST_ROWS, 2, HEAD_DIM)
    st = jnp.stack([st[:, :, 0, :, 0, :], st[:, :, 1, :, 1, :]], axis=2).reshape(b, N_HEADS, ST_ROWS, HEAD_DIM)
    c = jnp.swapaxes(st[:, :, :HEAD_DIM, :], 2, 3)
    return h, c, st[:, :, HEAD_DIM, :], m[:, :, 0]


def _mlstm_step_kernel(xq_ref, xk_ref, v_ref, g_ref, sq_ref, sk_ref, cwq_ref, cwk_ref, cbq_ref, cbk_ref, gb_ref,
                       c0_ref, n0_ref, m0_ref, h_ref, c_ref, n_ref, m_ref, q_scr, ik_scr, *, t_new, batch):
    hist = CONV_W - 1
    tile = lambda ref, t: ref[pl.ds(t, batch, stride=t_new), :].T
    xq = [sq_ref[t].T for t in range(hist)] + [tile(xq_ref, t) for t in range(t_new)]
    xk = [sk_ref[t].T for t in range(hist)] + [tile(xk_ref, t) for t in range(t_new)]
    c_ref[...] = c0_ref[...]
    n = [n0_ref[e] for e in range(2)]
    m = [m0_ref[e] for e in range(2)]
    for t in range(t_new):
        cq, ck = cbq_ref[...], cbk_ref[...]
        for j in range(CONV_W):
            cq = cq + xq[t + j] * cwq_ref[j]
            ck = ck + xk[t + j] * cwk_ref[j]
        q_t = cq * _sigmoid(cq)
        k_t = ck * _sigmoid(ck) * (HEAD_DIM ** -0.5)
        v_t = tile(v_ref, t)
        gates = tile(g_ref, t)[0:2 * N_HEADS, :] + gb_ref[...]
        h_pair = []
        for e in range(2):
            head = 2 * pl.program_id(0) + e
            rows = slice(e * HEAD_DIM, (e + 1) * HEAD_DIM)
            ig = _gates_row(gates, head)
            lf = _log_sigmoid(_gates_row(gates, N_HEADS + head))
            m_new = jnp.maximum(lf + m[e], ig)
            f = jnp.exp(lf + m[e] - m_new)
            i = jnp.exp(ig - m_new)
            q_e, k_e, v_e = q_t[rows], k_t[rows], v_t[rows]
            q_scr[...] = q_e
            ik_scr[...] = k_e * i

            def body(kk, num, e=e, f=f, v_e=v_e):
                c_new = f * c_ref[e, kk] + ik_scr[pl.ds(kk, 1), :] * v_e
                c_ref[e, kk] = c_new
                return num + c_new * q_scr[pl.ds(kk, 1), :]
            num = lax.fori_loop(0, HEAD_DIM, body, jnp.zeros((HEAD_DIM, batch), F32), unroll=8)
            n[e] = f * n[e] + k_e * i
            den = jnp.sum(n[e] * q_e, axis=0, keepdims=True)
            h_pair.append(num * (1.0 / jnp.maximum(jnp.abs(den), jnp.exp(-m_new))))
            m[e] = m_new
        h_ref[pl.ds(t, batch, stride=t_new), :] = jnp.concatenate(h_pair, axis=0).T
    for e in range(2):
        n_ref[e] = n[e]
        m_ref[e] = m[e]


def _gates_row(gates, idx):
    sel = lax.broadcasted_iota(jnp.int32, gates.shape, 0) == idx
    return jnp.sum(jnp.where(sel, gates, 0.0), axis=0, keepdims=True)


def _mlstm_step(qk_pre, conv_state, conv_w, conv_b, mv, gates, b_ig, b_fg, c0, n0, m0, t_new):
    rows = qk_pre.shape[0]
    batch = rows // t_new
    assert batch == LANES and t_new % SUBLANES == 0
    hist = CONV_W - 1
    lanes_b = lambda a: jnp.broadcast_to(a[..., None], a.shape + (batch,))
    sc = jnp.swapaxes(conv_state, 0, 1)
    cw, cb = lanes_b(conv_w), lanes_b(conv_b)
    gb = lanes_b(jnp.concatenate([b_ig, b_fg]))
    c_t = jnp.transpose(c0, (1, 2, 3, 0))
    n_t = jnp.transpose(n0, (1, 2, 0))
    m_t = jnp.transpose(m0, (1, 0)).reshape(N_HEADS, 1, batch)
    npair = N_HEADS // 2
    colq = lambda: pl.BlockSpec((rows, LANES), lambda p: (0, p))
    colk = lambda: pl.BlockSpec((rows, LANES), lambda p: (0, npair + p))
    est = (8 * rows * LANES * 4 + 4 * 2 * HEAD_DIM * HEAD_DIM * batch * 4 + 2 * (hist + 2 * CONV_W + 2) * LANES * batch * 4
           + 4 * (hist + t_new) * LANES * batch * 4)
    h, c, n, m = pl.pallas_call(
        functools.partial(_mlstm_step_kernel, t_new=t_new, batch=batch),
        grid=(npair,),
        in_specs=[colq(), colk(), colq(), pl.BlockSpec((rows, GATE_PAD), lambda p: (0, 0)),
                  pl.BlockSpec((hist, batch, LANES), lambda p: (0, 0, p)),
                  pl.BlockSpec((hist, batch, LANES), lambda p: (0, 0, npair + p)),
                  pl.BlockSpec((CONV_W, LANES, batch), lambda p: (0, p, 0)),
                  pl.BlockSpec((CONV_W, LANES, batch), lambda p: (0, npair + p, 0)),
                  pl.BlockSpec((LANES, batch), lambda p: (p, 0)),
                  pl.BlockSpec((LANES, batch), lambda p: (npair + p, 0)),
                  pl.BlockSpec((2 * N_HEADS, batch), lambda p: (0, 0)),
                  pl.BlockSpec((2, HEAD_DIM, HEAD_DIM, batch), lambda p: (p, 0, 0, 0)),
                  pl.BlockSpec((2, HEAD_DIM, batch), lambda p: (p, 0, 0)),
                  pl.BlockSpec((2, 1, batch), lambda p: (p, 0, 0))],
        out_specs=[colq(),
                   pl.BlockSpec((2, HEAD_DIM, HEAD_DIM, batch), lambda p: (p, 0, 0, 0)),
                   pl.BlockSpec((2, HEAD_DIM, batch), lambda p: (p, 0, 0)),
                   pl.BlockSpec((2, 1, batch), lambda p: (p, 0, 0))],
        out_shape=[jax.ShapeDtypeStruct((rows, MIX_W), F32),
                   jax.ShapeDtypeStruct(c_t.shape, F32), jax.ShapeDtypeStruct(n_t.shape, F32),
                   jax.ShapeDtypeStruct(m_t.shape, F32)],
        scratch_shapes=[pltpu.VMEM((HEAD_DIM, batch), F32), pltpu.VMEM((HEAD_DIM, batch), F32)],
        compiler_params=_params(("arbitrary",), est),
        name="mlstm_step",
    )(qk_pre, qk_pre, mv, gates, sc, sc, cw, cw, cb, cb, gb, c_t, n_t, m_t)
    return (h, jnp.transpose(c, (3, 0, 1, 2)), jnp.transpose(n, (2, 0, 1)),
            jnp.transpose(m.reshape(N_HEADS, batch), (1, 0)))


def _mixers_to_output(x2, att, mlh, mo, mem_k, mem_v, wts, bt_x, tq_x, tm):
    b = mem_k.shape[0]
    if tq_x >= 2 * LANES:
        r3 = lambda a: a.reshape(b, -1, a.shape[-1])
        h2 = _xattn_block(r3(att), r3(mlh), r3(mo), r3(x2), wts, mem_k, mem_v, tq_x).reshape(-1, D_MODEL)
    else:
        h1 = _merge(att, mlh, mo, wts["b_og"], wts["g_attn_out"], wts["g_mlstm_out"], wts["w_out"], x2, tm)
        (xq,) = _norm_matmul(h1, wts["g_cross"], wts["w_cq"], (D_MODEL,), tm)
        xo = _xattn(xq.reshape(b, -1, D_MODEL), mem_k, mem_v, bt_x, tq_x)
        h2 = _matmul_res(xo.reshape(-1, D_MODEL), wts["w_co"], h1, tm)
    return _ffn(h2, wts["g_ffn"], wts["w_gate"], wts["w_up"], wts["w_down"], wts["g_final"], tm)


def kernel(x_prompt, x_sample, mem_prompt, cache_win_k, cache_win_v, cache_mem_k, cache_mem_v, state_conv, state_mlstm_C, state_mlstm_n, state_mlstm_m, g_mix, w_in, conv_w, conv_b, b_ig, b_fg, b_og, g_attn_out, g_mlstm_out, w_out, g_cross, g_mem, w_cq, w_ck, w_cv, w_co, g_ffn, w_gate, w_up, w_down, g_final):
    bp, sp, _ = x_prompt.shape
    bs, ts, _ = x_sample.shape
    wts = dict(b_og=b_og, g_attn_out=g_attn_out, g_mlstm_out=g_mlstm_out, g_cross=g_cross, g_ffn=g_ffn,
               g_final=g_final, w_out=w_out.astype(BF16), w_cq=w_cq.astype(BF16), w_co=w_co.astype(BF16),
               w_gate=w_gate.astype(BF16), w_up=w_up.astype(BF16), w_down=w_down.astype(BF16))
    w_in_t = jnp.pad(w_in.T, ((0, sum(IN_WIDTHS) - N_IN), (0, 0))).astype(BF16)
    w_ckv = jnp.concatenate([w_ck, w_cv], axis=1).astype(BF16)
    tm = 512

    xp2 = x_prompt.reshape(bp * sp, D_MODEL)
    aq_c, ak_c, av_c, ak_t, av_t, mq_t, mk, mv_t, mo, gates, conv_tail = _inproj_prompt(
        xp2, g_mix, w_in_t, conv_w, conv_b, tm, sp)
    r3 = lambda a: a.reshape(bp, sp, -1)
    att = _attn_prompt(aq_c, ak_c, av_c)
    keep = min(DIL_GROUPS[-1][0], sp)
    from_t = lambda a: jnp.transpose(a.reshape(a.shape[0], N_HEADS, HEAD_DIM, a.shape[2]), (0, 3, 1, 2))
    win_k_p = from_t(ak_t)[:, sp - keep:]
    win_v_p = from_t(av_t)[:, sp - keep:]
    mlh, c_p, n_p, m_p = _mlstm_seq(mq_t, r3(mk), mv_t, r3(gates), b_ig, b_fg, chunk=256)
    conv_p = conv_tail[:, SUBLANES - (CONV_W - 1):]
    mem_k_view, mem_v_view = _mem_proj(mem_prompt, g_mem, w_ckv)
    mem_k_p, mem_v_p = _mem_tile_unview(mem_k_view), _mem_tile_unview(mem_v_view)
    y_p = _mixers_to_output(xp2, att.reshape(-1, MIX_W), mlh.reshape(-1, MIX_W), mo, mem_k_view, mem_v_view, wts,
                            bt_x=1, tq_x=1024, tm=tm)

    xs2 = x_sample.reshape(bs * ts, D_MODEL)
    aq, ak, av, qk_pre, mv, mo, gates = _norm_matmul(xs2, g_mix, w_in_t, IN_WIDTHS, tm, w_transposed=True)
    r3 = lambda a: a.reshape(bs, ts, -1)
    wb = cache_win_k.shape[1]
    to_t = lambda a: jnp.transpose(a, (0, 2, 3, 1)).reshape(a.shape[0], MIX_W, a.shape[1])
    att, win_k_s, win_v_s = _attn_sample(r3(aq), r3(ak), r3(av), to_t(cache_win_k), to_t(cache_win_v))
    mlh, c_s, n_s, m_s = _mlstm_step(qk_pre, state_conv, conv_w, conv_b, mv, gates, b_ig, b_fg,
                                     state_mlstm_C, state_mlstm_n, state_mlstm_m, ts)
    conv_s = jnp.concatenate([state_conv, r3(qk_pre)], axis=1)[:, -(CONV_W - 1):]
    y_s = _mixers_to_output(xs2, att.reshape(-1, MIX_W), mlh, mo, _mem_tile_view(cache_mem_k),
                            _mem_tile_view(cache_mem_v), wts,
                            bt_x=4, tq_x=ts, tm=tm)

    return (y_p.reshape(bp, sp, D_MODEL), y_s.reshape(bs, ts, D_MODEL),
            win_k_p, win_v_p, conv_p, c_p, n_p, m_p, mem_k_p, mem_v_p,
            from_t(win_k_s), from_t(win_v_s), conv_s, c_s, n_s, m_s)
```

```python
import functools

import numpy as np
import jax
import jax.numpy as jnp
from jax import lax
from jax.experimental import pallas as pl
from jax.experimental.pallas import tpu as pltpu

F32 = jnp.float32
BF16 = jnp.bfloat16

D_MODEL = 1024
HEAD_DIM = 64
N_HEADS = 8
MIX_W = N_HEADS * HEAD_DIM
DIL_GROUPS = ((128, 1), (512, 4), (2048, 16))
BAND = 128
CONV_W = 4
N_MEM = 256
N_X_HEADS = 4
X_HEAD_DIM = D_MODEL // N_X_HEADS
D_FF = 2816
RMS_EPS = 1e-6
NEG = -1e30
N_IN = 3 * MIX_W + 2 * MIX_W + MIX_W + MIX_W + 2 * N_HEADS
GATE_PAD = 128

LANES = 128
SUBLANES = 8
VMEM_BYTES_V7X = 64 * 1024 * 1024


def _vmem_limit(nbytes):
    return int(min(max(2 * nbytes, 16 * 1024 * 1024), VMEM_BYTES_V7X - 8 * 1024 * 1024))


def _params(semantics, vmem_estimate):
    return pltpu.CompilerParams(dimension_semantics=semantics,
                                vmem_limit_bytes=_vmem_limit(vmem_estimate))


def _rms(x, g):
    return x * lax.rsqrt(jnp.mean(x * x, axis=-1, keepdims=True) + RMS_EPS) * g


def _sigmoid(x):
    return 1.0 / (1.0 + jnp.exp(-x))


def _log_sigmoid(x):
    return jnp.minimum(x, 0.0) - jnp.log(1.0 + jnp.exp(-jnp.abs(x)))


def _dot(a, b):
    return jnp.dot(a, b, preferred_element_type=F32)


def _dot_nt(a, b):
    return lax.dot_general(a, b, (((1,), (1,)), ((), ())), preferred_element_type=F32)


N_CLASSES = max(d for _, d in DIL_GROUPS)
IN_WIDTHS = (MIX_W, MIX_W, MIX_W, 2 * MIX_W, MIX_W, MIX_W, GATE_PAD)
IN_OFFS = tuple(int(o) for o in np.cumsum((0,) + IN_WIDTHS))


def _norm_matmul_kernel(x_ref, g_ref, w_ref, *o_refs, splits, w_transposed):
    xn = _rms(x_ref[...], g_ref[...]).astype(BF16)
    for o_ref, (a, b) in zip(o_refs, splits):
        o_ref[...] = _dot_nt(xn, w_ref[a:b, :]) if w_transposed else _dot(xn, w_ref[:, a:b])


def _norm_matmul(x, g, w, widths, tm, w_transposed=False):
    m, k = x.shape
    n = w.shape[0] if w_transposed else w.shape[1]
    assert sum(widths) == n and m % tm == 0 and w.shape == ((n, k) if w_transposed else (k, n))
    offs = np.cumsum([0] + list(widths))
    splits = tuple((int(offs[i]), int(offs[i + 1])) for i in range(len(widths)))
    est = 2 * tm * k * 4 + 2 * k * n * 2 + 2 * tm * n * 4
    return pl.pallas_call(
        functools.partial(_norm_matmul_kernel, splits=splits, w_transposed=w_transposed),
        grid=(m // tm,),
        in_specs=[pl.BlockSpec((tm, k), lambda i: (i, 0)),
                  pl.BlockSpec((1, k), lambda i: (0, 0)),
                  pl.BlockSpec(w.shape, lambda i: (0, 0))],
        out_specs=[pl.BlockSpec((tm, wd), lambda i: (i, 0)) for wd in widths],
        out_shape=[jax.ShapeDtypeStruct((m, wd), F32) for wd in widths],
        compiler_params=_params(("parallel",), est),
        name="norm_matmul",
    )(x, g.reshape(1, k), w)


def _inproj_prompt_kernel(x_ref, g_ref, w_ref, perm_ref, cw_ref, cb_ref,
                          qc_ref, kc_ref, vc_ref, kt_ref, vt_ref, mq_ref, mk_ref, mvt_ref, mo_ref, gate_ref, tail_ref,
                          xp_scr, *, per_seq):
    tm = x_ref.shape[0]

    @pl.when(pl.program_id(0) % per_seq == 0)
    def _():
        xp_scr[0:SUBLANES, :] = jnp.zeros((SUBLANES, 2 * MIX_W), F32)

    xn = _rms(x_ref[...], g_ref[...]).astype(BF16)
    proj = lambda idx: _dot_nt(xn, w_ref[IN_OFFS[idx]:IN_OFFS[idx + 1], :])

    xp_scr[SUBLANES:SUBLANES + tm, :] = proj(3)
    conv = cb_ref[...]
    for j in range(CONV_W):
        off = SUBLANES - (CONV_W - 1) + j
        conv = conv + xp_scr[off:off + tm, :] * cw_ref[j:j + 1, :]
    tail = xp_scr[tm:tm + SUBLANES, :]
    xp_scr[0:SUBLANES, :] = tail
    tail_ref[0] = tail
    act = conv * _sigmoid(conv)
    mq_ref[0] = act[:, :MIX_W].T
    mk_ref[...] = act[:, MIX_W:] * (HEAD_DIM ** -0.5)

    for idx, c_ref, t_ref in ((0, qc_ref, None), (1, kc_ref, kt_ref), (2, vc_ref, vt_ref)):
        y = proj(idx)
        if t_ref is not None:
            t_ref[0] = y.T
        y_cls = _dot(perm_ref[...], y.astype(BF16))
        c_ref[0] = y_cls.reshape(N_CLASSES, tm // N_CLASSES, MIX_W)

    mvt_ref[0] = _dot_nt(w_ref[IN_OFFS[4]:IN_OFFS[5], :], xn)
    mo_ref[...] = proj(5)
    gate_ref[...] = proj(6)


def _inproj_prompt(x, g, w_t, conv_w, conv_b, tm, seq):
    m, k = x.shape
    n = w_t.shape[0]
    assert n == IN_OFFS[-1] and m % seq == 0 and seq % tm == 0 and tm % (N_CLASSES * SUBLANES) == 0
    b, per_seq = m // seq, seq // tm
    est = (2 * tm * k * 4 + 2 * k * n * 2 + 2 * tm * (n + 4 * MIX_W) * 4 + 3 * tm * MIX_W * 4
           + 3 * (tm + SUBLANES) * 2 * MIX_W * 4)
    cls_spec = lambda: pl.BlockSpec((1, N_CLASSES, tm // N_CLASSES, MIX_W), lambda i: (i // per_seq, 0, i % per_seq, 0))
    cls_shape = jax.ShapeDtypeStruct((b, N_CLASSES, seq // N_CLASSES, MIX_W), F32)
    t_spec = lambda: pl.BlockSpec((1, MIX_W, tm), lambda i: (i // per_seq, 0, i % per_seq))
    t_shape = jax.ShapeDtypeStruct((b, MIX_W, seq), F32)
    nat_spec = lambda wd: pl.BlockSpec((tm, wd), lambda i: (i, 0))
    nat_shape = lambda wd: jax.ShapeDtypeStruct((m, wd), F32)
    cw = jnp.pad(conv_w, ((0, SUBLANES - CONV_W), (0, 0)))
    dst = np.arange(tm)
    perm = np.zeros((tm, tm), np.float32)
    perm[dst, (dst % (tm // N_CLASSES)) * N_CLASSES + dst // (tm // N_CLASSES)] = 1.0
    return pl.pallas_call(
        functools.partial(_inproj_prompt_kernel, per_seq=per_seq),
        grid=(m // tm,),
        in_specs=[pl.BlockSpec((tm, k), lambda i: (i, 0)),
                  pl.BlockSpec((1, k), lambda i: (0, 0)),
                  pl.BlockSpec((n, k), lambda i: (0, 0)),
                  pl.BlockSpec((tm, tm), lambda i: (0, 0)),
                  pl.BlockSpec((SUBLANES, 2 * MIX_W), lambda i: (0, 0)),
                  pl.BlockSpec((1, 2 * MIX_W), lambda i: (0, 0))],
        out_specs=[cls_spec(), cls_spec(), cls_spec(), t_spec(), t_spec(), t_spec(), nat_spec(MIX_W), t_spec(),
                   nat_spec(MIX_W), nat_spec(GATE_PAD),
                   pl.BlockSpec((1, SUBLANES, 2 * MIX_W), lambda i: (i // per_seq, 0, 0))],
        out_shape=[cls_shape, cls_shape, cls_shape, t_shape, t_shape, t_shape, nat_shape(MIX_W), t_shape,
                   nat_shape(MIX_W), nat_shape(GATE_PAD),
                   jax.ShapeDtypeStruct((b, SUBLANES, 2 * MIX_W), F32)],
        scratch_shapes=[pltpu.VMEM((tm + SUBLANES, 2 * MIX_W), F32)],
        compiler_params=_params(("arbitrary",), est),
        name="inproj_prompt",
    )(x, g.reshape(1, k), w_t, jnp.asarray(perm, BF16), cw, conv_b.reshape(1, -1))


def _matmul_res_kernel(x_ref, w_ref, r_ref, o_ref):
    o_ref[...] = r_ref[...] + _dot(x_ref[...].astype(BF16), w_ref[...])


def _matmul_res(x, w, res, tm):
    m, k = x.shape
    n = w.shape[1]
    est = 2 * tm * k * 4 + 2 * k * n * 2 + 4 * tm * n * 4
    return pl.pallas_call(
        _matmul_res_kernel,
        grid=(m // tm,),
        in_specs=[pl.BlockSpec((tm, k), lambda i: (i, 0)),
                  pl.BlockSpec((k, n), lambda i: (0, 0)),
                  pl.BlockSpec((tm, n), lambda i: (i, 0))],
        out_specs=pl.BlockSpec((tm, n), lambda i: (i, 0)),
        out_shape=jax.ShapeDtypeStruct((m, n), F32),
        compiler_params=_params(("parallel",), est),
        name="matmul_res",
    )(x, w, res)


def _merge_kernel(att_ref, mlh_ref, mo_ref, bog_ref, ga_ref, gm_ref, w_ref, x_ref, o_ref):
    a_n = _rms(att_ref[...], ga_ref[...]).astype(BF16)
    ml = _sigmoid(mo_ref[...] + bog_ref[...]) * mlh_ref[...]
    m_n = _rms(ml, gm_ref[...]).astype(BF16)
    o_ref[...] = x_ref[...] + _dot(a_n, w_ref[0:MIX_W, :]) + _dot(m_n, w_ref[MIX_W:2 * MIX_W, :])


def _merge(att, mlh, mo, b_og, g_a, g_m, w_out, x, tm):
    m = x.shape[0]
    half = lambda: pl.BlockSpec((tm, MIX_W), lambda i: (i, 0))
    vec = lambda: pl.BlockSpec((1, MIX_W), lambda i: (0, 0))
    est = 6 * tm * MIX_W * 4 + 2 * 2 * MIX_W * D_MODEL * 2 + 4 * tm * D_MODEL * 4
    return pl.pallas_call(
        _merge_kernel,
        grid=(m // tm,),
        in_specs=[half(), half(), half(), vec(), vec(), vec(),
                  pl.BlockSpec((2 * MIX_W, D_MODEL), lambda i: (0, 0)),
                  pl.BlockSpec((tm, D_MODEL), lambda i: (i, 0))],
        out_specs=pl.BlockSpec((tm, D_MODEL), lambda i: (i, 0)),
        out_shape=jax.ShapeDtypeStruct((m, D_MODEL), F32),
        compiler_params=_params(("parallel",), est),
        name="merge",
    )(att, mlh, mo, b_og.reshape(1, -1), g_a.reshape(1, -1), g_m.reshape(1, -1), w_out, x)


def _ffn_kernel(h_ref, gf_ref, wg_ref, wu_ref, wd_ref, gl_ref, o_ref):
    h = h_ref[...]
    hn = _rms(h, gf_ref[...]).astype(BF16)
    gate = _dot(hn, wg_ref[...])
    up = _dot(hn, wu_ref[...])
    act = (gate * _sigmoid(gate) * up).astype(BF16)
    o_ref[...] = _rms(h + _dot(act, wd_ref[...]), gl_ref[...])


def _ffn(h, g_ffn, w_gate, w_up, w_down, g_final, tm):
    m = h.shape[0]
    est = 4 * tm * D_MODEL * 4 + 2 * 3 * D_MODEL * D_FF * 2 + 3 * tm * D_FF * 4
    return pl.pallas_call(
        _ffn_kernel,
        grid=(m // tm,),
        in_specs=[pl.BlockSpec((tm, D_MODEL), lambda i: (i, 0)),
                  pl.BlockSpec((1, D_MODEL), lambda i: (0, 0)),
                  pl.BlockSpec((D_MODEL, D_FF), lambda i: (0, 0)),
                  pl.BlockSpec((D_MODEL, D_FF), lambda i: (0, 0)),
                  pl.BlockSpec((D_FF, D_MODEL), lambda i: (0, 0)),
                  pl.BlockSpec((1, D_MODEL), lambda i: (0, 0))],
        out_specs=pl.BlockSpec((tm, D_MODEL), lambda i: (i, 0)),
        out_shape=jax.ShapeDtypeStruct((m, D_MODEL), F32),
        compiler_params=_params(("parallel",), est),
        name="ffn",
    )(h, g_ffn.reshape(1, -1), w_gate, w_up, w_down, g_final.reshape(1, -1))


X_HALVES = X_HEAD_DIM // LANES
X_ROWS = N_X_HEADS * X_HALVES


def _mem_tile_view(a):
    b = a.shape[0]
    a = a.reshape(b, N_MEM, N_X_HEADS, X_HALVES, LANES)
    return jnp.transpose(a, (0, 1, 3, 2, 4)).reshape(b, N_MEM * X_ROWS, LANES)


def _mem_tile_unview(a):
    b = a.shape[0]
    a = a.reshape(b, N_MEM, X_HALVES, N_X_HEADS, LANES)
    return jnp.transpose(a, (0, 1, 3, 2, 4)).reshape(b, N_MEM, N_X_HEADS, X_HEAD_DIM)


def _mem_proj_kernel(x_ref, g_ref, w_ref, k_ref, v_ref):
    xn = _rms(x_ref[0], g_ref[...]).astype(BF16)
    for idx, o_ref in enumerate((k_ref, v_ref)):
        y = _dot(xn, w_ref[:, idx * D_MODEL:(idx + 1) * D_MODEL])
        for h in range(N_X_HEADS):
            for j in range(X_HALVES):
                lanes = slice(h * X_HEAD_DIM + j * LANES, h * X_HEAD_DIM + (j + 1) * LANES)
                o_ref[0, pl.ds(j * N_X_HEADS + h, N_MEM, stride=X_ROWS), :] = y[:, lanes]


def _mem_proj(mem, g, w_ckv):
    b = mem.shape[0]
    view = jax.ShapeDtypeStruct((b, N_MEM * X_ROWS, LANES), F32)
    est = 2 * N_MEM * D_MODEL * 4 + 2 * D_MODEL * 2 * D_MODEL * 2 + 6 * N_MEM * D_MODEL * 4
    return pl.pallas_call(
        _mem_proj_kernel,
        grid=(b,),
        in_specs=[pl.BlockSpec((1, N_MEM, D_MODEL), lambda i: (i, 0, 0)),
                  pl.BlockSpec((1, D_MODEL), lambda i: (0, 0)),
                  pl.BlockSpec((D_MODEL, 2 * D_MODEL), lambda i: (0, 0))],
        out_specs=[pl.BlockSpec((1, N_MEM * X_ROWS, LANES), lambda i: (i, 0, 0)) for _ in range(2)],
        out_shape=[view, view],
        compiler_params=_params(("parallel",), est),
        name="mem_proj",
    )(mem, g.reshape(1, -1), w_ckv)


def _mem_head(ref, bi, h):
    halves = [ref[bi, pl.ds(j * N_X_HEADS + h, N_MEM, stride=X_ROWS), :] for j in range(X_HALVES)]
    return jnp.concatenate(halves, axis=1).astype(BF16)


def _xattn_kernel(q_ref, k_ref, v_ref, o_ref, *, bt):
    for bi in range(bt):
        for h in range(N_X_HEADS):
            sl = slice(h * X_HEAD_DIM, (h + 1) * X_HEAD_DIM)
            q = (q_ref[bi, :, sl] * (X_HEAD_DIM ** -0.5)).astype(BF16)
            s = _dot_nt(q, _mem_head(k_ref, bi, h))
            p = jnp.exp(s - jnp.max(s, axis=-1, keepdims=True))
            inv_l = 1.0 / jnp.sum(p, axis=-1, keepdims=True)
            o_ref[bi, :, sl] = _dot(p.astype(BF16), _mem_head(v_ref, bi, h)) * inv_l


def _xattn(q, mem_k, mem_v, bt, tq):
    b, t, _ = q.shape
    est = 4 * bt * tq * D_MODEL * 4 + 4 * bt * N_MEM * D_MODEL * 4
    kv = lambda: pl.BlockSpec((bt, N_MEM * X_ROWS, LANES), lambda i, j: (i, 0, 0))
    return pl.pallas_call(
        functools.partial(_xattn_kernel, bt=bt),
        grid=(b // bt, t // tq),
        in_specs=[pl.BlockSpec((bt, tq, D_MODEL), lambda i, j: (i, j, 0)), kv(), kv()],
        out_specs=pl.BlockSpec((bt, tq, D_MODEL), lambda i, j: (i, j, 0)),
        out_shape=jax.ShapeDtypeStruct((b, t, D_MODEL), F32),
        compiler_params=_params(("parallel", "parallel"), est),
        name="xattn",
    )(q, mem_k, mem_v)


def _xattn_block_kernel(att_ref, mlh_ref, mo_ref, x_ref, bog_ref, ga_ref, gm_ref, wout_ref,
                        g_ref, wq_ref, k_ref, v_ref, wo_ref, o_ref):
    a_n = _rms(att_ref[0], ga_ref[...]).astype(BF16)
    m_n = _rms(_sigmoid(mo_ref[0] + bog_ref[...]) * mlh_ref[0], gm_ref[...]).astype(BF16)
    h1 = x_ref[0] + _dot(a_n, wout_ref[0:MIX_W, :]) + _dot(m_n, wout_ref[MIX_W:2 * MIX_W, :])
    q_all = _dot(_rms(h1, g_ref[...]).astype(BF16), wq_ref[...]) * (X_HEAD_DIM ** -0.5)
    o_parts = []
    for h in range(N_X_HEADS):
        sl = slice(h * X_HEAD_DIM, (h + 1) * X_HEAD_DIM)
        s = _dot_nt(q_all[:, sl].astype(BF16), _mem_head(k_ref, 0, h))
        p = jnp.exp(s - jnp.max(s, axis=-1, keepdims=True))
        inv_l = 1.0 / jnp.sum(p, axis=-1, keepdims=True)
        o_parts.append((_dot(p.astype(BF16), _mem_head(v_ref, 0, h)) * inv_l).astype(BF16))
    o_ref[0] = h1 + _dot(jnp.concatenate(o_parts, axis=1), wo_ref[...])


def _xattn_block(att, mlh, mo, x, wts, mem_k, mem_v, tq):
    b, t, _ = x.shape
    est = (6 * tq * MIX_W * 4 + 8 * tq * D_MODEL * 4 + 4 * N_MEM * D_MODEL * 4 + 2 * 3 * D_MODEL * D_MODEL * 2)
    kv = lambda: pl.BlockSpec((1, N_MEM * X_ROWS, LANES), lambda i, j: (i, 0, 0))
    wspec = lambda: pl.BlockSpec((D_MODEL, D_MODEL), lambda i, j: (0, 0))
    half = lambda: pl.BlockSpec((1, tq, MIX_W), lambda i, j: (i, j, 0))
    full = lambda: pl.BlockSpec((1, tq, D_MODEL), lambda i, j: (i, j, 0))
    vec = lambda w: pl.BlockSpec((1, w), lambda i, j: (0, 0))
    row = lambda a: a.reshape(1, -1)
    return pl.pallas_call(
        _xattn_block_kernel,
        grid=(b, t // tq),
        in_specs=[half(), half(), half(), full(), vec(MIX_W), vec(MIX_W), vec(MIX_W), wspec(),
                  vec(D_MODEL), wspec(), kv(), kv(), wspec()],
        out_specs=full(),
        out_shape=jax.ShapeDtypeStruct((b, t, D_MODEL), F32),
        compiler_params=_params(("parallel", "parallel"), est),
        name="xattn_block",
    )(att, mlh, mo, x, row(wts["b_og"]), row(wts["g_attn_out"]), row(wts["g_mlstm_out"]), wts["w_out"],
      row(wts["g_cross"]), wts["w_cq"], mem_k, mem_v, wts["w_co"])


def _alibi_slopes():
    return np.exp2(-8.0 * (np.arange(N_HEADS, dtype=np.float64) + 1.0) / N_HEADS)


def _prompt_bias_tables():
    full, first = [], []
    for _, dil in DIL_GROUPS:
        jn = N_CLASSES // dil
        rn = BAND // jn
        qj, qm = np.divmod(np.arange(BAND), rn)
        kj, km = np.divmod(np.arange(2 * BAND), 2 * rn)
        for tabs, row_offset in ((full, rn), (first, 0)):
            delta = jn * (row_offset + qm[:, None] - km[None, :]) + (qj[:, None] - kj[None, :])
            valid = (delta >= 0) & (delta <= BAND)
            bias = -_alibi_slopes()[:, None, None] * (delta * dil)[None].astype(np.float64)
            tabs.append(np.where(valid[None], bias, NEG))
    return np.stack(full).astype(np.float32), np.stack(first).astype(np.float32)


UNITS_PER_TRIP = 16


def _largest_divisor(n, cap):
    return max(u for u in range(1, cap + 1) if n % u == 0)


def _attn_prompt_kernel(q_ref, k_ref, v_ref, bfull_ref, bfirst_ref, o_ref, og_ref, lg_ref, *, rows_per_class):
    lane = lax.broadcasted_iota(jnp.int32, (1, LANES), 1)
    first_head = lane < HEAD_DIM

    def unit(g, dil, r_d, blk, full):
        jn = N_CLASSES // dil
        rn = BAND // jn
        classes = [r_d + dil * j for j in range(jn)]
        if full:
            m0 = pl.multiple_of(blk * rn, SUBLANES)
            q_rows = pl.ds(m0, rn)
            k_rows = pl.ds(pl.multiple_of(m0 - rn, SUBLANES), 2 * rn)
        else:
            assert blk == 0
            q_rows = pl.ds(0, rn)
            k_rows = pl.ds(0, min(2 * rn, rows_per_class))
        gather = lambda ref, rows: jnp.concatenate([ref[0, c, rows, :] for c in classes], axis=0)
        q = gather(q_ref, q_rows) * (HEAD_DIM ** -0.5)
        k = gather(k_ref, k_rows).astype(BF16)
        v = gather(v_ref, k_rows).astype(BF16)
        if k.shape[0] < 2 * BAND:
            assert jn == 1
            k = jnp.concatenate([k, k], axis=0)
            v = jnp.concatenate([v, v], axis=0)
        outs, lses = [], []
        for h in range(2):
            qh = jnp.where(first_head if h == 0 else jnp.logical_not(first_head), q, 0.0).astype(BF16)
            s = _dot_nt(qh, k) + (bfull_ref[g, h] if full else bfirst_ref[g, h])
            m = jnp.max(s, axis=-1, keepdims=True)
            p = jnp.exp(s - m)
            l = jnp.sum(p, axis=-1, keepdims=True)
            outs.append(_dot(p.astype(BF16), v) * (1.0 / l))
            lses.append(m + jnp.log(l))
        o = jnp.where(first_head, outs[0], outs[1])
        lse = jnp.broadcast_to(jnp.where(first_head, lses[0], lses[1]), o.shape)
        for j, c in enumerate(classes):
            og_ref[g, c, q_rows, :] = o[j * rn:(j + 1) * rn]
            lg_ref[g, c, q_rows, :] = lse[j * rn:(j + 1) * rn]

    for g, (window, dil) in enumerate(DIL_GROUPS):
        assert window // dil == BAND and N_CLASSES % dil == 0
        nblk = rows_per_class * N_CLASSES // (dil * BAND)
        if dil <= UNITS_PER_TRIP:
            for r in range(dil):
                unit(g, dil, r, 0, False)
        else:
            def body1(r, carry, g=g, dil=dil):
                unit(g, dil, r, 0, False)
                return carry
            lax.fori_loop(0, dil, body1, 0, unroll=_largest_divisor(dil, UNITS_PER_TRIP))
        rest = dil * (nblk - 1)
        if rest:
            def body(j, carry, g=g, dil=dil, nblk=nblk):
                unit(g, dil, j // (nblk - 1), j % (nblk - 1) + 1, True)
                return carry
            lax.fori_loop(0, rest, body, 0, unroll=_largest_divisor(rest, UNITS_PER_TRIP))

    def merge(c, carry):
        l0, l1, l2 = lg_ref[0, c], lg_ref[1, c], lg_ref[2, c]
        mx = jnp.maximum(jnp.maximum(l0, l1), l2)
        e0, e1, e2 = jnp.exp(l0 - mx), jnp.exp(l1 - mx), jnp.exp(l2 - mx)
        merged = (e0 * og_ref[0, c] + e1 * og_ref[1, c] + e2 * og_ref[2, c]) * (1.0 / (e0 + e1 + e2))
        o_ref[0, pl.ds(c, rows_per_class, stride=N_CLASSES), :] = merged
        return carry
    lax.fori_loop(0, N_CLASSES, merge, 0, unroll=2)


def _attn_prompt(q, k, v):
    b, _, rpc, _ = q.shape
    s = rpc * N_CLASSES
    bfull, bfirst = (jnp.asarray(t) for t in _prompt_bias_tables())
    ng = len(DIL_GROUPS)
    blk = lambda: pl.BlockSpec((1, N_CLASSES, rpc, LANES), lambda i, p: (i, 0, 0, p))
    est = 8 * s * LANES * 4 + 6 * s * LANES * 4 + 2 * ng * 2 * BAND * 3 * BAND * 4
    return pl.pallas_call(
        functools.partial(_attn_prompt_kernel, rows_per_class=rpc),
        grid=(b, MIX_W // LANES),
        in_specs=[blk(), blk(), blk(),
                  pl.BlockSpec((ng, 2, BAND, 2 * BAND), lambda i, p: (0, p, 0, 0)),
                  pl.BlockSpec((ng, 2, BAND, 2 * BAND), lambda i, p: (0, p, 0, 0))],
        out_specs=pl.BlockSpec((1, s, LANES), lambda i, p: (i, 0, p)),
        out_shape=jax.ShapeDtypeStruct((b, s, MIX_W), F32),
        scratch_shapes=[pltpu.VMEM((ng, N_CLASSES, rpc, LANES), F32) for _ in range(2)],
        compiler_params=_params(("parallel", "parallel"), est),
        name="attn_prompt",
    )(q, k, v, bfull, bfirst)


def _sample_bias_tables(wb, t_new):
    slopes = _alibi_slopes()

    def table(dist):
        mult = np.zeros(dist.shape)
        for window, dil in DIL_GROUPS:
            mult += (dist >= 0) & (dist <= window) & (dist % dil == 0)
        logm = np.where(mult > 0, np.log(np.maximum(mult, 1.0)), NEG)
        tab = -slopes[:, None, None] * dist[None].astype(np.float64) + logm[None]
        tab = np.where((mult > 0)[None], tab, NEG)
        return tab.reshape(N_HEADS * t_new, dist.shape[1]).astype(np.float32)

    t = np.arange(t_new)[:, None]
    return table(wb + t - np.arange(wb)[None, :]), table(t - np.arange(t_new)[None, :])


WINDOW_SLOTS = 3


def _attn_sample_kernel(q_ref, kn_ref, vn_ref, wk_hbm, wv_hbm, bo_ref, bn_ref,
                        att_ref, ok_ref, ov_ref, kbuf, vbuf, sem, *, wb, t_new, n_batch):
    step = pl.program_id(0)
    ahead = WINDOW_SLOTS - 1

    def window_copies(batch, slot):
        return (pltpu.make_async_copy(wk_hbm.at[batch], kbuf.at[slot], sem.at[0, slot]),
                pltpu.make_async_copy(wv_hbm.at[batch], vbuf.at[slot], sem.at[1, slot]))

    @pl.when(step == 0)
    def _():
        for j in range(min(ahead, n_batch)):
            for c in window_copies(j, j):
                c.start()

    @pl.when(step + ahead < n_batch)
    def _():
        for c in window_copies(step + ahead, (step + ahead) % WINDOW_SLOTS):
            c.start()

    slot = step % WINDOW_SLOTS
    for c in window_copies(step, slot):
        c.wait()

    rows_q = N_HEADS * t_new
    q = q_ref[0] * (HEAD_DIM ** -0.5)
    qm = jnp.concatenate([q] * N_HEADS, axis=0)
    row_head = lax.broadcasted_iota(jnp.int32, (rows_q, MIX_W), 0) // t_new
    lane_head = lax.broadcasted_iota(jnp.int32, (rows_q, MIX_W), 1) // HEAD_DIM
    own = row_head == lane_head
    qm = jnp.where(own, qm, 0.0).astype(BF16)
    k_new, v_new = kn_ref[0], vn_ref[0]
    wk, wv = kbuf[slot], vbuf[slot]
    s_old = _dot(qm, wk.astype(BF16)) + bo_ref[...]
    s_new = _dot_nt(qm, k_new.astype(BF16)) + bn_ref[...]
    m = jnp.maximum(jnp.max(s_old, axis=-1, keepdims=True), jnp.max(s_new, axis=-1, keepdims=True))
    p_old = jnp.exp(s_old - m)
    p_new = jnp.exp(s_new - m)
    inv_l = 1.0 / (jnp.sum(p_old, axis=-1, keepdims=True) + jnp.sum(p_new, axis=-1, keepdims=True))
    o = _dot_nt(p_old.astype(BF16), wv.astype(BF16)) + _dot(p_new.astype(BF16), v_new.astype(BF16))
    o = jnp.where(own, o * inv_l, 0.0)
    att = o[0:t_new]
    for h in range(1, N_HEADS):
        att = att + o[h * t_new:(h + 1) * t_new]
    att_ref[0] = att
    lane = lax.broadcasted_iota(jnp.int32, (MIX_W, LANES), 1)
    is_new = lane >= LANES - t_new
    pad = jnp.zeros((LANES - t_new, MIX_W), F32)
    for w, new, o_ref in ((wk, k_new, ok_ref), (wv, v_new, ov_ref)):
        rolled = pltpu.roll(w, wb - t_new, axis=1)
        o_ref[0] = rolled
        new_t = jnp.concatenate([pad, new], axis=0).T
        o_ref[0, :, wb - LANES:wb] = jnp.where(is_new, new_t, rolled[:, wb - LANES:wb])


def _attn_sample(q, k_new, v_new, win_k_t, win_v_t):
    b, t_new, _ = q.shape
    wb = win_k_t.shape[2]
    assert wb >= DIL_GROUPS[-1][0] and t_new % SUBLANES == 0 and wb % LANES == 0
    b_old, b_new = _sample_bias_tables(wb, t_new)
    small = lambda: pl.BlockSpec((1, t_new, MIX_W), lambda i: (i, 0, 0))
    big = lambda: pl.BlockSpec((1, MIX_W, wb), lambda i: (i, 0, 0))
    est = (2 * WINDOW_SLOTS + 4) * wb * MIX_W * 4 + 2 * wb * MIX_W * 2 + 4 * N_HEADS * t_new * wb * 4
    in_hbm = lambda: pl.BlockSpec(memory_space=pl.ANY)
    return pl.pallas_call(
        functools.partial(_attn_sample_kernel, wb=wb, t_new=t_new, n_batch=b),
        grid=(b,),
        in_specs=[small(), small(), small(), in_hbm(), in_hbm(),
                  pl.BlockSpec((N_HEADS * t_new, wb), lambda i: (0, 0)),
                  pl.BlockSpec((N_HEADS * t_new, t_new), lambda i: (0, 0))],
        out_specs=[small(), big(), big()],
        out_shape=[jax.ShapeDtypeStruct((b, t_new, MIX_W), F32),
                   jax.ShapeDtypeStruct((b, MIX_W, wb), F32),
                   jax.ShapeDtypeStruct((b, MIX_W, wb), F32)],
        scratch_shapes=[pltpu.VMEM((WINDOW_SLOTS, MIX_W, wb), F32), pltpu.VMEM((WINDOW_SLOTS, MIX_W, wb), F32),
                        pltpu.SemaphoreType.DMA((2, WINDOW_SLOTS))],
        compiler_params=_params(("arbitrary",), est),
        name="attn_sample",
    )(q, k_new, v_new, win_k_t, win_v_t, jnp.asarray(b_old), jnp.asarray(b_new))


ST_ROWS = HEAD_DIM + SUBLANES


def _mlstm_seq_kernel(qt_ref, k_ref, vt_ref, gcol_ref, mirow_ref, mfrow_ref, bcol_ref, bigrow_ref, bfgrow_ref,
                      h_ref, st_ref, m_ref, *, chunk):
    L = chunk
    hp = lax.Precision.HIGHEST

    @pl.when(pl.program_id(1) == 0)
    def _():
        st_ref[...] = jnp.zeros(st_ref.shape, F32)
        m_ref[...] = jnp.full(m_ref.shape, NEG, F32)

    ss = lax.broadcasted_iota(jnp.int32, (L, L), 0)
    tt = lax.broadcasted_iota(jnp.int32, (L, L), 1)
    causal = ss <= tt
    ig_r = mirow_ref[0] + bigrow_ref[...]
    lf_r = _log_sigmoid(mfrow_ref[0] + bfgrow_ref[...])
    b_r = jnp.dot(lf_r, causal.astype(F32), precision=hp, preferred_element_type=F32)
    lane = lax.broadcasted_iota(jnp.int32, (N_HEADS, L), 1)
    cm = ig_r - b_r
    shift = 1
    while shift < L:
        cm = jnp.maximum(cm, jnp.where(lane >= shift, pltpu.roll(cm, shift, axis=1), NEG))
        shift *= 2
    m_prev = m_ref[0][:, 0:1]
    m_t = b_r + jnp.maximum(m_prev, cm)
    row_all = b_r - m_t
    inter_w = jnp.exp(b_r + m_prev - m_t)
    floor = jnp.exp(-m_t)
    m_new = m_t[:, L - 1:L]
    b_last = b_r[:, L - 1:L]
    w_s = jnp.exp(b_last - b_r + ig_r - m_new)
    decay = jnp.exp(b_last + m_prev - m_new)
    pre_c = gcol_ref[0] + bcol_ref[...]
    bcum_c = jnp.dot((ss >= tt).astype(F32), _log_sigmoid(pre_c), precision=hp, preferred_element_type=F32)
    col_all = pre_c - pltpu.roll(bcum_c, GATE_PAD - N_HEADS, axis=1)

    ones_row = (lax.broadcasted_iota(jnp.int32, (SUBLANES, L), 0) == 0).astype(F32)
    zeros_q = jnp.zeros((HEAD_DIM, L), BF16)
    h_parts, new_state = [], []
    for h in range(N_HEADS):
        pair, half = divmod(h, 2)
        rows = slice(h * HEAD_DIM, (h + 1) * HEAD_DIM)
        q_own = qt_ref[0, rows, :].astype(BF16)
        q_pad = jnp.concatenate([q_own, zeros_q] if half == 0 else [zeros_q, q_own], axis=0)
        k_pair = k_ref[0, :, pair * LANES:(pair + 1) * LANES].astype(BF16)
        v_ext = jnp.concatenate([vt_ref[0, rows, :], ones_row], axis=0)
        st = st_ref[0, h]
        d = jnp.exp(jnp.where(causal, col_all[:, h:h + 1] + row_all[h:h + 1, :], NEG))
        a = (_dot(k_pair, q_pad) * d).astype(BF16)
        tot = _dot(v_ext.astype(BF16), a) + inter_w[h:h + 1, :] * _dot(st.astype(BF16), q_pad)
        den = tot[HEAD_DIM:HEAD_DIM + 1, :]
        h_parts.append(tot[0:HEAD_DIM, :] * (1.0 / jnp.maximum(jnp.abs(den), floor[h:h + 1, :])))
        new_state.append(decay[h:h + 1, :] * st + _dot((v_ext * w_s[h:h + 1, :]).astype(BF16), k_pair))
    h_ref[0] = jnp.concatenate(h_parts, axis=0).T
    for h, st_new in enumerate(new_state):
        st_ref[0, h] = st_new
    m_ref[0] = jnp.broadcast_to(m_new, (N_HEADS, LANES))


def _mlstm_seq(q_t, k, v_t, gates, b_ig, b_fg, chunk):
    b, s, _ = k.shape
    assert s % chunk == 0 and chunk % LANES == 0
    g_rows = jnp.swapaxes(gates[:, :, :2 * N_HEADS], 1, 2)
    bcol = jnp.concatenate([b_ig, b_fg, jnp.zeros((GATE_PAD - 2 * N_HEADS,), F32)]).reshape(1, GATE_PAD)
    ft = lambda: pl.BlockSpec((1, MIX_W, chunk), lambda i, c: (i, 0, c))
    row = lambda: pl.BlockSpec((1, N_HEADS, chunk), lambda i, c: (i, 0, c))
    const = lambda r, w: pl.BlockSpec((r, w), lambda i, c: (0, 0))
    est = 8 * chunk * MIX_W * 4 + 12 * chunk * chunk * 4 + 4 * N_HEADS * ST_ROWS * LANES * 4
    h, st, m = pl.pallas_call(
        functools.partial(_mlstm_seq_kernel, chunk=chunk),
        grid=(b, s // chunk),
        in_specs=[ft(), pl.BlockSpec((1, chunk, MIX_W), lambda i, c: (i, c, 0)), ft(),
                  pl.BlockSpec((1, chunk, GATE_PAD), lambda i, c: (i, c, 0)), row(), row(),
                  const(1, GATE_PAD), const(N_HEADS, 1), const(N_HEADS, 1)],
        out_specs=[pl.BlockSpec((1, chunk, MIX_W), lambda i, c: (i, c, 0)),
                   pl.BlockSpec((1, N_HEADS, ST_ROWS, LANES), lambda i, c: (i, 0, 0, 0)),
                   pl.BlockSpec((1, N_HEADS, LANES), lambda i, c: (i, 0, 0))],
        out_shape=[jax.ShapeDtypeStruct((b, s, MIX_W), F32),
                   jax.ShapeDtypeStruct((b, N_HEADS, ST_ROWS, LANES), F32),
                   jax.ShapeDtypeStruct((b, N_HEADS, LANES), F32)],
        compiler_params=_params(("arbitrary", "arbitrary"), est),
        name="mlstm_seq",
    )(q_t, k, v_t, gates, g_rows[:, :N_HEADS], g_rows[:, N_HEADS:], bcol, b_ig.reshape(-1, 1), b_fg.reshape(-1, 1))
    st = st.reshape(b, N_HEADS // 2, 2, ST_ROWS, 2, HEAD_DIM)
    st = jnp.stack([st[:, :, 0, :, 0, :], st[:, :, 1, :, 1, :]], axis=2).reshape(b, N_HEADS, ST_ROWS, HEAD_DIM)
    c = jnp.swapaxes(st[:, :, :HEAD_DIM, :], 2, 3)
    return h, c, st[:, :, HEAD_DIM, :], m[:, :, 0]


def _mlstm_step_kernel(xq_ref, xk_ref, v_ref, g_ref, sq_ref, sk_ref, cwq_ref, cwk_ref, cbq_ref, cbk_ref, gb_ref,
                       c0_ref, n0_ref, m0_ref, h_ref, c_ref, n_ref, m_ref, q_scr, ik_scr, *, t_new, batch):
    hist = CONV_W - 1
    tile = lambda ref, t: ref[pl.ds(t, batch, stride=t_new), :].T
    xq = [sq_ref[t].T for t in range(hist)] + [tile(xq_ref, t) for t in range(t_new)]
    xk = [sk_ref[t].T for t in range(hist)] + [tile(xk_ref, t) for t in range(t_new)]
    c_ref[...] = c0_ref[...]
    n = [n0_ref[e] for e in range(2)]
    m = [m0_ref[e] for e in range(2)]
    for t in range(t_new):
        cq, ck = cbq_ref[...], cbk_ref[...]
        for j in range(CONV_W):
            cq = cq + xq[t + j] * cwq_ref[j]
            ck = ck + xk[t + j] * cwk_ref[j]
        q_t = cq * _sigmoid(cq)
        k_t = ck * _sigmoid(ck) * (HEAD_DIM ** -0.5)
        v_t = tile(v_ref, t)
        gates = tile(g_ref, t)[0:2 * N_HEADS, :] + gb_ref[...]
        h_pair = []
        for e in range(2):
            head = 2 * pl.program_id(0) + e
            rows = slice(e * HEAD_DIM, (e + 1) * HEAD_DIM)
            ig = _gates_row(gates, head)
            lf = _log_sigmoid(_gates_row(gates, N_HEADS + head))
            m_new = jnp.maximum(lf + m[e], ig)
            f = jnp.exp(lf + m[e] - m_new)
            i = jnp.exp(ig - m_new)
            q_e, k_e, v_e = q_t[rows], k_t[rows], v_t[rows]
            q_scr[...] = q_e
            ik_scr[...] = k_e * i

            def body(kk, num, e=e, f=f, v_e=v_e):
                c_new = f * c_ref[e, kk] + ik_scr[pl.ds(kk, 1), :] * v_e
                c_ref[e, kk] = c_new
                return num + c_new * q_scr[pl.ds(kk, 1), :]
            num = lax.fori_loop(0, HEAD_DIM, body, jnp.zeros((HEAD_DIM, batch), F32), unroll=8)
            n[e] = f * n[e] + k_e * i
            den = jnp.sum(n[e] * q_e, axis=0, keepdims=True)
            h_pair.append(num * (1.0 / jnp.maximum(jnp.abs(den), jnp.exp(-m_new))))
            m[e] = m_new
        h_ref[pl.ds(t, batch, stride=t_new), :] = jnp.concatenate(h_pair, axis=0).T
    for e in range(2):
        n_ref[e] = n[e]
        m_ref[e] = m[e]


def _gates_row(gates, idx):
    sel = lax.broadcasted_iota(jnp.int32, gates.shape, 0) == idx
    return jnp.sum(jnp.where(sel, gates, 0.0), axis=0, keepdims=True)


def _mlstm_step(qk_pre, conv_state, conv_w, conv_b, mv, gates, b_ig, b_fg, c0, n0, m0, t_new):
    rows = qk_pre.shape[0]
    batch = rows // t_new
    assert batch == LANES and t_new % SUBLANES == 0
    hist = CONV_W - 1
    lanes_b = lambda a: jnp.broadcast_to(a[..., None], a.shape + (batch,))
    sc = jnp.swapaxes(conv_state, 0, 1)
    cw, cb = lanes_b(conv_w), lanes_b(conv_b)
    gb = lanes_b(jnp.concatenate([b_ig, b_fg]))
    c_t = jnp.transpose(c0, (1, 2, 3, 0))
    n_t = jnp.transpose(n0, (1, 2, 0))
    m_t = jnp.transpose(m0, (1, 0)).reshape(N_HEADS, 1, batch)
    npair = N_HEADS // 2
    colq = lambda: pl.BlockSpec((rows, LANES), lambda p: (0, p))
    colk = lambda: pl.BlockSpec((rows, LANES), lambda p: (0, npair + p))
    est = (8 * rows * LANES * 4 + 4 * 2 * HEAD_DIM * HEAD_DIM * batch * 4 + 2 * (hist + 2 * CONV_W + 2) * LANES * batch * 4
           + 4 * (hist + t_new) * LANES * batch * 4)
    h, c, n, m = pl.pallas_call(
        functools.partial(_mlstm_step_kernel, t_new=t_new, batch=batch),
        grid=(npair,),
        in_specs=[colq(), colk(), colq(), pl.BlockSpec((rows, GATE_PAD), lambda p: (0, 0)),
                  pl.BlockSpec((hist, batch, LANES), lambda p: (0, 0, p)),
                  pl.BlockSpec((hist, batch, LANES), lambda p: (0, 0, npair + p)),
                  pl.BlockSpec((CONV_W, LANES, batch), lambda p: (0, p, 0)),
                  pl.BlockSpec((CONV_W, LANES, batch), lambda p: (0, npair + p, 0)),
                  pl.BlockSpec((LANES, batch), lambda p: (p, 0)),
                  pl.BlockSpec((LANES, batch), lambda p: (npair + p, 0)),
                  pl.BlockSpec((2 * N_HEADS, batch), lambda p: (0, 0)),
                  pl.BlockSpec((2, HEAD_DIM, HEAD_DIM, batch), lambda p: (p, 0, 0, 0)),
                  pl.BlockSpec((2, HEAD_DIM, batch), lambda p: (p, 0, 0)),
                  pl.BlockSpec((2, 1, batch), lambda p: (p, 0, 0))],
        out_specs=[colq(),
                   pl.BlockSpec((2, HEAD_DIM, HEAD_DIM, batch), lambda p: (p, 0, 0, 0)),
                   pl.BlockSpec((2, HEAD_DIM, batch), lambda p: (p, 0, 0)),
                   pl.BlockSpec((2, 1, batch), lambda p: (p, 0, 0))],
        out_shape=[jax.ShapeDtypeStruct((rows, MIX_W), F32),
                   jax.ShapeDtypeStruct(c_t.shape, F32), jax.ShapeDtypeStruct(n_t.shape, F32),
                   jax.ShapeDtypeStruct(m_t.shape, F32)],
        scratch_shapes=[pltpu.VMEM((HEAD_DIM, batch), F32), pltpu.VMEM((HEAD_DIM, batch), F32)],
        compiler_params=_params(("arbitrary",), est),
        name="mlstm_step",
    )(qk_pre, qk_pre, mv, gates, sc, sc, cw, cw, cb, cb, gb, c_t, n_t, m_t)
    return (h, jnp.transpose(c, (3, 0, 1, 2)), jnp.transpose(n, (2, 0, 1)),
            jnp.transpose(m.reshape(N_HEADS, batch), (1, 0)))


def _mixers_to_output(x2, att, mlh, mo, mem_k, mem_v, wts, bt_x, tq_x, tm):
    b = mem_k.shape[0]
    if tq_x >= 2 * LANES:
        r3 = lambda a: a.reshape(b, -1, a.shape[-1])
        h2 = _xattn_block(r3(att), r3(mlh), r3(mo), r3(x2), wts, mem_k, mem_v, tq_x).reshape(-1, D_MODEL)
    else:
        h1 = _merge(att, mlh, mo, wts["b_og"], wts["g_attn_out"], wts["g_mlstm_out"], wts["w_out"], x2, tm)
        (xq,) = _norm_matmul(h1, wts["g_cross"], wts["w_cq"], (D_MODEL,), tm)
        xo = _xattn(xq.reshape(b, -1, D_MODEL), mem_k, mem_v, bt_x, tq_x)
        h2 = _matmul_res(xo.reshape(-1, D_MODEL), wts["w_co"], h1, tm)
    return _ffn(h2, wts["g_ffn"], wts["w_gate"], wts["w_up"], wts["w_down"], wts["g_final"], tm)


def kernel(x_prompt, x_sample, mem_prompt, cache_win_k, cache_win_v, cache_mem_k, cache_mem_v, state_conv, state_mlstm_C, state_mlstm_n, state_mlstm_m, g_mix, w_in, conv_w, conv_b, b_ig, b_fg, b_og, g_attn_out, g_mlstm_out, w_out, g_cross, g_mem, w_cq, w_ck, w_cv, w_co, g_ffn, w_gate, w_up, w_down, g_final):
    bp, sp, _ = x_prompt.shape
    bs, ts, _ = x_sample.shape
    wts = dict(b_og=b_og, g_attn_out=g_attn_out, g_mlstm_out=g_mlstm_out, g_cross=g_cross, g_ffn=g_ffn,
               g_final=g_final, w_out=w_out.astype(BF16), w_cq=w_cq.astype(BF16), w_co=w_co.astype(BF16),
               w_gate=w_gate.astype(BF16), w_up=w_up.astype(BF16), w_down=w_down.astype(BF16))
    w_in_t = jnp.pad(w_in.T, ((0, sum(IN_WIDTHS) - N_IN), (0, 0))).astype(BF16)
    w_ckv = jnp.concatenate([w_ck, w_cv], axis=1).astype(BF16)
    tm = 512

    xp2 = x_prompt.reshape(bp * sp, D_MODEL)
    aq_c, ak_c, av_c, ak_t, av_t, mq_t, mk, mv_t, mo, gates, conv_tail = _inproj_prompt(
        xp2, g_mix, w_in_t, conv_w, conv_b, tm, sp)
    r3 = lambda a: a.reshape(bp, sp, -1)
    att = _attn_prompt(aq_c, ak_c, av_c)
    keep = min(DIL_GROUPS[-1][0], sp)
    from_t = lambda a: jnp.transpose(a.reshape(a.shape[0], N_HEADS, HEAD_DIM, a.shape[2]), (0, 3, 1, 2))
    win_k_p = from_t(ak_t)[:, sp - keep:]
    win_v_p = from_t(av_t)[:, sp - keep:]
    mlh, c_p, n_p, m_p = _mlstm_seq(mq_t, r3(mk), mv_t, r3(gates), b_ig, b_fg, chunk=256)
    conv_p = conv_tail[:, SUBLANES - (CONV_W - 1):]
    mem_k_view, mem_v_view = _mem_proj(mem_prompt, g_mem, w_ckv)
    mem_k_p, mem_v_p = _mem_tile_unview(mem_k_view), _mem_tile_unview(mem_v_view)
    y_p = _mixers_to_output(xp2, att.reshape(-1, MIX_W), mlh.reshape(-1, MIX_W), mo, mem_k_view, mem_v_view, wts,
                            bt_x=1, tq_x=1024, tm=tm)

    xs2 = x_sample.reshape(bs * ts, D_MODEL)
    aq, ak, av, qk_pre, mv, mo, gates = _norm_matmul(xs2, g_mix, w_in_t, IN_WIDTHS, tm, w_transposed=True)
    r3 = lambda a: a.reshape(bs, ts, -1)
    wb = cache_win_k.shape[1]
    to_t = lambda a: jnp.transpose(a, (0, 2, 3, 1)).reshape(a.shape[0], MIX_W, a.shape[1])
    att, win_k_s, win_v_s = _attn_sample(r3(aq), r3(ak), r3(av), to_t(cache_win_k), to_t(cache_win_v))
    mlh, c_s, n_s, m_s = _mlstm_step(qk_pre, state_conv, conv_w, conv_b, mv, gates, b_ig, b_fg,
                                     state_mlstm_C, state_mlstm_n, state_mlstm_m, ts)
    conv_s = jnp.concatenate([state_conv, r3(qk_pre)], axis=1)[:, -(CONV_W - 1):]
    y_s = _mixers_to_output(xs2, att.reshape(-1, MIX_W), mlh, mo, _mem_tile_view(cache_mem_k),
                            _mem_tile_view(cache_mem_v), wts,
                            bt_x=4, tq_x=ts, tm=tm)

    return (y_p.reshape(bp, sp, D_MODEL), y_s.reshape(bs, ts, D_MODEL),
            win_k_p, win_v_p, conv_p, c_p, n_p, m_p, mem_k_p, mem_v_p,
            from_t(win_k_s), from_t(win_v_s), conv_s, c_s, n_s, m_s)
```

```python
import functools

import numpy as np
import jax
import jax.numpy as jnp
from jax import lax
from jax.experimental import pallas as pl
from jax.experimental.pallas import tpu as pltpu

F32 = jnp.float32
BF16 = jnp.bfloat16

D_MODEL = 1024
HEAD_DIM = 64
N_HEADS = 8
MIX_W = N_HEADS * HEAD_DIM
DIL_GROUPS = ((128, 1), (512, 4), (2048, 16))
BAND = 128
CONV_W = 4
N_MEM = 256
N_X_HEADS = 4
X_HEAD_DIM = D_MODEL // N_X_HEADS
D_FF = 2816
RMS_EPS = 1e-6
NEG = -1e30
N_IN = 3 * MIX_W + 2 * MIX_W + MIX_W + MIX_W + 2 * N_HEADS
GATE_PAD = 128

LANES = 128
SUBLANES = 8
VMEM_BYTES_V7X = 64 * 1024 * 1024


def _vmem_limit(nbytes):
    return int(min(max(2 * nbytes, 16 * 1024 * 1024), VMEM_BYTES_V7X - 8 * 1024 * 1024))


def _params(semantics, vmem_estimate):
    return pltpu.CompilerParams(dimension_semantics=semantics,
                                vmem_limit_bytes=_vmem_limit(vmem_estimate))


def _rms(x, g):
    return x * lax.rsqrt(jnp.mean(x * x, axis=-1, keepdims=True) + RMS_EPS) * g


def _sigmoid(x):
    return 1.0 / (1.0 + jnp.exp(-x))


def _log_sigmoid(x):
    return jnp.minimum(x, 0.0) - jnp.log(1.0 + jnp.exp(-jnp.abs(x)))


def _dot(a, b):
    return jnp.dot(a, b, preferred_element_type=F32)


def _dot_nt(a, b):
    return lax.dot_general(a, b, (((1,), (1,)), ((), ())), preferred_element_type=F32)


N_CLASSES = max(d for _, d in DIL_GROUPS)
IN_WIDTHS = (MIX_W, MIX_W, MIX_W, 2 * MIX_W, MIX_W, MIX_W, GATE_PAD)
IN_OFFS = tuple(int(o) for o in np.cumsum((0,) + IN_WIDTHS))


def _norm_matmul_kernel(x_ref, g_ref, w_ref, *o_refs, splits, w_transposed):
    xn = _rms(x_ref[...], g_ref[...]).astype(BF16)
    for o_ref, (a, b) in zip(o_refs, splits):
        o_ref[...] = _dot_nt(xn, w_ref[a:b, :]) if w_transposed else _dot(xn, w_ref[:, a:b])


def _norm_matmul(x, g, w, widths, tm, w_transposed=False):
    m, k = x.shape
    n = w.shape[0] if w_transposed else w.shape[1]
    assert sum(widths) == n and m % tm == 0 and w.shape == ((n, k) if w_transposed else (k, n))
    offs = np.cumsum([0] + list(widths))
    splits = tuple((int(offs[i]), int(offs[i + 1])) for i in range(len(widths)))
    est = 2 * tm * k * 4 + 2 * k * n * 2 + 2 * tm * n * 4
    return pl.pallas_call(
        functools.partial(_norm_matmul_kernel, splits=splits, w_transposed=w_transposed),
        grid=(m // tm,),
        in_specs=[pl.BlockSpec((tm, k), lambda i: (i, 0)),
                  pl.BlockSpec((1, k), lambda i: (0, 0)),
                  pl.BlockSpec(w.shape, lambda i: (0, 0))],
        out_specs=[pl.BlockSpec((tm, wd), lambda i: (i, 0)) for wd in widths],
        out_shape=[jax.ShapeDtypeStruct((m, wd), F32) for wd in widths],
        compiler_params=_params(("parallel",), est),
        name="norm_matmul",
    )(x, g.reshape(1, k), w)


def _inproj_prompt_kernel(x_ref, g_ref, w_ref, perm_ref, cw_ref, cb_ref,
                          qc_ref, kc_ref, vc_ref, kt_ref, vt_ref, mq_ref, mk_ref, mvt_ref, mo_ref, gate_ref, tail_ref,
                          xp_scr, *, per_seq):
    tm = x_ref.shape[0]

    @pl.when(pl.program_id(0) % per_seq == 0)
    def _():
        xp_scr[0:SUBLANES, :] = jnp.zeros((SUBLANES, 2 * MIX_W), F32)

    xn = _rms(x_ref[...], g_ref[...]).astype(BF16)
    proj = lambda idx: _dot_nt(xn, w_ref[IN_OFFS[idx]:IN_OFFS[idx + 1], :])

    xp_scr[SUBLANES:SUBLANES + tm, :] = proj(3)
    conv = cb_ref[...]
    for j in range(CONV_W):
        off = SUBLANES - (CONV_W - 1) + j
        conv = conv + xp_scr[off:off + tm, :] * cw_ref[j:j + 1, :]
    tail = xp_scr[tm:tm + SUBLANES, :]
    xp_scr[0:SUBLANES, :] = tail
    tail_ref[0] = tail
    act = conv * _sigmoid(conv)
    mq_ref[0] = act[:, :MIX_W].T
    mk_ref[...] = act[:, MIX_W:] * (HEAD_DIM ** -0.5)

    for idx, c_ref, t_ref in ((0, qc_ref, None), (1, kc_ref, kt_ref), (2, vc_ref, vt_ref)):
        y = proj(idx)
        if t_ref is not None:
            t_ref[0] = y.T
        y_cls = _dot(perm_ref[...], y.astype(BF16))
        c_ref[0] = y_cls.reshape(N_CLASSES, tm // N_CLASSES, MIX_W).astype(BF16)

    mvt_ref[0] = _dot_nt(w_ref[IN_OFFS[4]:IN_OFFS[5], :], xn)
    mo_ref[...] = proj(5)
    gate_ref[...] = proj(6)


def _inproj_prompt(x, g, w_t, conv_w, conv_b, tm, seq):
    m, k = x.shape
    n = w_t.shape[0]
    assert n == IN_OFFS[-1] and m % seq == 0 and seq % tm == 0 and tm % (N_CLASSES * SUBLANES) == 0
    b, per_seq = m // seq, seq // tm
    est = (2 * tm * k * 4 + 2 * k * n * 2 + 2 * tm * (n + 4 * MIX_W) * 4 + 3 * tm * MIX_W * 4
           + 3 * (tm + SUBLANES) * 2 * MIX_W * 4)
    cls_spec = lambda: pl.BlockSpec((1, N_CLASSES, tm // N_CLASSES, MIX_W), lambda i: (i // per_seq, 0, i % per_seq, 0))
    cls_shape = jax.ShapeDtypeStruct((b, N_CLASSES, seq // N_CLASSES, MIX_W), BF16)
    t_spec = lambda: pl.BlockSpec((1, MIX_W, tm), lambda i: (i // per_seq, 0, i % per_seq))
    t_shape = jax.ShapeDtypeStruct((b, MIX_W, seq), F32)
    nat_spec = lambda wd: pl.BlockSpec((tm, wd), lambda i: (i, 0))
    nat_shape = lambda wd: jax.ShapeDtypeStruct((m, wd), F32)
    cw = jnp.pad(conv_w, ((0, SUBLANES - CONV_W), (0, 0)))
    dst = np.arange(tm)
    perm = np.zeros((tm, tm), np.float32)
    perm[dst, (dst % (tm // N_CLASSES)) * N_CLASSES + dst // (tm // N_CLASSES)] = 1.0
    return pl.pallas_call(
        functools.partial(_inproj_prompt_kernel, per_seq=per_seq),
        grid=(m // tm,),
        in_specs=[pl.BlockSpec((tm, k), lambda i: (i, 0)),
                  pl.BlockSpec((1, k), lambda i: (0, 0)),
                  pl.BlockSpec((n, k), lambda i: (0, 0)),
                  pl.BlockSpec((tm, tm), lambda i: (0, 0)),
                  pl.BlockSpec((SUBLANES, 2 * MIX_W), lambda i: (0, 0)),
                  pl.BlockSpec((1, 2 * MIX_W), lambda i: (0, 0))],
        out_specs=[cls_spec(), cls_spec(), cls_spec(), t_spec(), t_spec(), t_spec(), nat_spec(MIX_W), t_spec(),
                   nat_spec(MIX_W), nat_spec(GATE_PAD),
                   pl.BlockSpec((1, SUBLANES, 2 * MIX_W), lambda i: (i // per_seq, 0, 0))],
        out_shape=[cls_shape, cls_shape, cls_shape, t_shape, t_shape, t_shape, nat_shape(MIX_W), t_shape,
                   nat_shape(MIX_W), nat_shape(GATE_PAD),
                   jax.ShapeDtypeStruct((b, SUBLANES, 2 * MIX_W), F32)],
        scratch_shapes=[pltpu.VMEM((tm + SUBLANES, 2 * MIX_W), F32)],
        compiler_params=_params(("arbitrary",), est),
        name="inproj_prompt",
    )(x, g.reshape(1, k), w_t, jnp.asarray(perm, BF16), cw, conv_b.reshape(1, -1))


def _matmul_res_kernel(x_ref, w_ref, r_ref, o_ref):
    o_ref[...] = r_ref[...] + _dot(x_ref[...].astype(BF16), w_ref[...])


def _matmul_res(x, w, res, tm):
    m, k = x.shape
    n = w.shape[1]
    est = 2 * tm * k * 4 + 2 * k * n * 2 + 4 * tm * n * 4
    return pl.pallas_call(
        _matmul_res_kernel,
        grid=(m // tm,),
        in_specs=[pl.BlockSpec((tm, k), lambda i: (i, 0)),
                  pl.BlockSpec((k, n), lambda i: (0, 0)),
                  pl.BlockSpec((tm, n), lambda i: (i, 0))],
        out_specs=pl.BlockSpec((tm, n), lambda i: (i, 0)),
        out_shape=jax.ShapeDtypeStruct((m, n), F32),
        compiler_params=_params(("parallel",), est),
        name="matmul_res",
    )(x, w, res)


def _merge_kernel(att_ref, mlh_ref, mo_ref, bog_ref, ga_ref, gm_ref, w_ref, x_ref, o_ref):
    a_n = _rms(att_ref[...], ga_ref[...]).astype(BF16)
    ml = _sigmoid(mo_ref[...] + bog_ref[...]) * mlh_ref[...]
    m_n = _rms(ml, gm_ref[...]).astype(BF16)
    o_ref[...] = x_ref[...] + _dot(a_n, w_ref[0:MIX_W, :]) + _dot(m_n, w_ref[MIX_W:2 * MIX_W, :])


def _merge(att, mlh, mo, b_og, g_a, g_m, w_out, x, tm):
    m = x.shape[0]
    half = lambda: pl.BlockSpec((tm, MIX_W), lambda i: (i, 0))
    vec = lambda: pl.BlockSpec((1, MIX_W), lambda i: (0, 0))
    est = 6 * tm * MIX_W * 4 + 2 * 2 * MIX_W * D_MODEL * 2 + 4 * tm * D_MODEL * 4
    return pl.pallas_call(
        _merge_kernel,
        grid=(m // tm,),
        in_specs=[half(), half(), half(), vec(), vec(), vec(),
                  pl.BlockSpec((2 * MIX_W, D_MODEL), lambda i: (0, 0)),
                  pl.BlockSpec((tm, D_MODEL), lambda i: (i, 0))],
        out_specs=pl.BlockSpec((tm, D_MODEL), lambda i: (i, 0)),
        out_shape=jax.ShapeDtypeStruct((m, D_MODEL), F32),
        compiler_params=_params(("parallel",), est),
        name="merge",
    )(att, mlh, mo, b_og.reshape(1, -1), g_a.reshape(1, -1), g_m.reshape(1, -1), w_out, x)


def _ffn_kernel(h_ref, gf_ref, wg_ref, wu_ref, wd_ref, gl_ref, o_ref):
    h = h_ref[...]
    hn = _rms(h, gf_ref[...]).astype(BF16)
    gate = _dot(hn, wg_ref[...])
    up = _dot(hn, wu_ref[...])
    act = (gate * _sigmoid(gate) * up).astype(BF16)
    o_ref[...] = _rms(h + _dot(act, wd_ref[...]), gl_ref[...])


def _ffn(h, g_ffn, w_gate, w_up, w_down, g_final, tm):
    m = h.shape[0]
    est = 4 * tm * D_MODEL * 4 + 2 * 3 * D_MODEL * D_FF * 2 + 3 * tm * D_FF * 4
    return pl.pallas_call(
        _ffn_kernel,
        grid=(m // tm,),
        in_specs=[pl.BlockSpec((tm, D_MODEL), lambda i: (i, 0)),
                  pl.BlockSpec((1, D_MODEL), lambda i: (0, 0)),
                  pl.BlockSpec((D_MODEL, D_FF), lambda i: (0, 0)),
                  pl.BlockSpec((D_MODEL, D_FF), lambda i: (0, 0)),
                  pl.BlockSpec((D_FF, D_MODEL), lambda i: (0, 0)),
                  pl.BlockSpec((1, D_MODEL), lambda i: (0, 0))],
        out_specs=pl.BlockSpec((tm, D_MODEL), lambda i: (i, 0)),
        out_shape=jax.ShapeDtypeStruct((m, D_MODEL), F32),
        compiler_params=_params(("parallel",), est),
        name="ffn",
    )(h, g_ffn.reshape(1, -1), w_gate, w_up, w_down, g_final.reshape(1, -1))


X_HALVES = X_HEAD_DIM // LANES
X_ROWS = N_X_HEADS * X_HALVES


def _mem_tile_view(a):
    b = a.shape[0]
    a = a.reshape(b, N_MEM, N_X_HEADS, X_HALVES, LANES)
    return jnp.transpose(a, (0, 1, 3, 2, 4)).reshape(b, N_MEM * X_ROWS, LANES)


def _mem_tile_unview(a):
    b = a.shape[0]
    a = a.reshape(b, N_MEM, X_HALVES, N_X_HEADS, LANES)
    return jnp.transpose(a, (0, 1, 3, 2, 4)).reshape(b, N_MEM, N_X_HEADS, X_HEAD_DIM)


def _mem_proj_kernel(x_ref, g_ref, w_ref, k_ref, v_ref):
    xn = _rms(x_ref[0], g_ref[...]).astype(BF16)
    for idx, o_ref in enumerate((k_ref, v_ref)):
        y = _dot(xn, w_ref[:, idx * D_MODEL:(idx + 1) * D_MODEL])
        for h in range(N_X_HEADS):
            for j in range(X_HALVES):
                lanes = slice(h * X_HEAD_DIM + j * LANES, h * X_HEAD_DIM + (j + 1) * LANES)
                o_ref[0, pl.ds(j * N_X_HEADS + h, N_MEM, stride=X_ROWS), :] = y[:, lanes]


def _mem_proj(mem, g, w_ckv):
    b = mem.shape[0]
    view = jax.ShapeDtypeStruct((b, N_MEM * X_ROWS, LANES), F32)
    est = 2 * N_MEM * D_MODEL * 4 + 2 * D_MODEL * 2 * D_MODEL * 2 + 6 * N_MEM * D_MODEL * 4
    return pl.pallas_call(
        _mem_proj_kernel,
        grid=(b,),
        in_specs=[pl.BlockSpec((1, N_MEM, D_MODEL), lambda i: (i, 0, 0)),
                  pl.BlockSpec((1, D_MODEL), lambda i: (0, 0)),
                  pl.BlockSpec((D_MODEL, 2 * D_MODEL), lambda i: (0, 0))],
        out_specs=[pl.BlockSpec((1, N_MEM * X_ROWS, LANES), lambda i: (i, 0, 0)) for _ in range(2)],
        out_shape=[view, view],
        compiler_params=_params(("parallel",), est),
        name="mem_proj",
    )(mem, g.reshape(1, -1), w_ckv)


def _mem_head(ref, bi, h):
    halves = [ref[bi, pl.ds(j * N_X_HEADS + h, N_MEM, stride=X_ROWS), :] for j in range(X_HALVES)]
    return jnp.concatenate(halves, axis=1).astype(BF16)


def _xattn_kernel(q_ref, k_ref, v_ref, o_ref, *, bt):
    for bi in range(bt):
        for h in range(N_X_HEADS):
            sl = slice(h * X_HEAD_DIM, (h + 1) * X_HEAD_DIM)
            q = (q_ref[bi, :, sl] * (X_HEAD_DIM ** -0.5)).astype(BF16)
            s = _dot_nt(q, _mem_head(k_ref, bi, h))
            p = jnp.exp(s - jnp.max(s, axis=-1, keepdims=True))
            inv_l = 1.0 / jnp.sum(p, axis=-1, keepdims=True)
            o_ref[bi, :, sl] = _dot(p.astype(BF16), _mem_head(v_ref, bi, h)) * inv_l


def _xattn(q, mem_k, mem_v, bt, tq):
    b, t, _ = q.shape
    est = 4 * bt * tq * D_MODEL * 4 + 4 * bt * N_MEM * D_MODEL * 4
    kv = lambda: pl.BlockSpec((bt, N_MEM * X_ROWS, LANES), lambda i, j: (i, 0, 0))
    return pl.pallas_call(
        functools.partial(_xattn_kernel, bt=bt),
        grid=(b // bt, t // tq),
        in_specs=[pl.BlockSpec((bt, tq, D_MODEL), lambda i, j: (i, j, 0)), kv(), kv()],
        out_specs=pl.BlockSpec((bt, tq, D_MODEL), lambda i, j: (i, j, 0)),
        out_shape=jax.ShapeDtypeStruct((b, t, D_MODEL), F32),
        compiler_params=_params(("parallel", "parallel"), est),
        name="xattn",
    )(q, mem_k, mem_v)


def _xattn_block_kernel(att_ref, mlh_ref, mo_ref, x_ref, bog_ref, ga_ref, gm_ref, wout_ref,
                        g_ref, wq_ref, k_ref, v_ref, wo_ref, o_ref):
    a_n = _rms(att_ref[0], ga_ref[...]).astype(BF16)
    m_n = _rms(_sigmoid(mo_ref[0] + bog_ref[...]) * mlh_ref[0], gm_ref[...]).astype(BF16)
    h1 = x_ref[0] + _dot(a_n, wout_ref[0:MIX_W, :]) + _dot(m_n, wout_ref[MIX_W:2 * MIX_W, :])
    q_all = _dot(_rms(h1, g_ref[...]).astype(BF16), wq_ref[...]) * (X_HEAD_DIM ** -0.5)
    o_parts = []
    for h in range(N_X_HEADS):
        sl = slice(h * X_HEAD_DIM, (h + 1) * X_HEAD_DIM)
        s = _dot_nt(q_all[:, sl].astype(BF16), _mem_head(k_ref, 0, h))
        p = jnp.exp(s - jnp.max(s, axis=-1, keepdims=True))
        inv_l = 1.0 / jnp.sum(p, axis=-1, keepdims=True)
        o_parts.append((_dot(p.astype(BF16), _mem_head(v_ref, 0, h)) * inv_l).astype(BF16))
    o_ref[0] = h1 + _dot(jnp.concatenate(o_parts, axis=1), wo_ref[...])


def _xattn_block(att, mlh, mo, x, wts, mem_k, mem_v, tq):
    b, t, _ = x.shape
    est = (6 * tq * MIX_W * 4 + 8 * tq * D_MODEL * 4 + 4 * N_MEM * D_MODEL * 4 + 2 * 3 * D_MODEL * D_MODEL * 2)
    kv = lambda: pl.BlockSpec((1, N_MEM * X_ROWS, LANES), lambda i, j: (i, 0, 0))
    wspec = lambda: pl.BlockSpec((D_MODEL, D_MODEL), lambda i, j: (0, 0))
    half = lambda: pl.BlockSpec((1, tq, MIX_W), lambda i, j: (i, j, 0))
    full = lambda: pl.BlockSpec((1, tq, D_MODEL), lambda i, j: (i, j, 0))
    vec = lambda w: pl.BlockSpec((1, w), lambda i, j: (0, 0))
    row = lambda a: a.reshape(1, -1)
    return pl.pallas_call(
        _xattn_block_kernel,
        grid=(b, t // tq),
        in_specs=[half(), half(), half(), full(), vec(MIX_W), vec(MIX_W), vec(MIX_W), wspec(),
                  vec(D_MODEL), wspec(), kv(), kv(), wspec()],
        out_specs=full(),
        out_shape=jax.ShapeDtypeStruct((b, t, D_MODEL), F32),
        compiler_params=_params(("parallel", "parallel"), est),
        name="xattn_block",
    )(att, mlh, mo, x, row(wts["b_og"]), row(wts["g_attn_out"]), row(wts["g_mlstm_out"]), wts["w_out"],
      row(wts["g_cross"]), wts["w_cq"], mem_k, mem_v, wts["w_co"])


def _alibi_slopes():
    return np.exp2(-8.0 * (np.arange(N_HEADS, dtype=np.float64) + 1.0) / N_HEADS)


def _prompt_bias_tables():
    full, first = [], []
    for _, dil in DIL_GROUPS:
        jn = N_CLASSES // dil
        rn = BAND // jn
        qj, qm = np.divmod(np.arange(BAND), rn)
        kj, km = np.divmod(np.arange(2 * BAND), 2 * rn)
        for tabs, row_offset in ((full, rn), (first, 0)):
            delta = jn * (row_offset + qm[:, None] - km[None, :]) + (qj[:, None] - kj[None, :])
            valid = (delta >= 0) & (delta <= BAND)
            bias = -_alibi_slopes()[:, None, None] * (delta * dil)[None].astype(np.float64)
            tabs.append(np.where(valid[None], bias, NEG))
    return np.stack(full).astype(np.float32), np.stack(first).astype(np.float32)


UNITS_PER_TRIP = 16


def _largest_divisor(n, cap):
    return max(u for u in range(1, cap + 1) if n % u == 0)


def _attn_prompt_kernel(q_ref, k_ref, v_ref, bfull_ref, bfirst_ref, o_ref, og_ref, lg_ref, *, rows_per_class):
    lane = lax.broadcasted_iota(jnp.int32, (1, LANES), 1)
    first_head = lane < HEAD_DIM

    def unit(g, dil, r_d, blk, full):
        jn = N_CLASSES // dil
        rn = BAND // jn
        classes = [r_d + dil * j for j in range(jn)]
        if full:
            m0 = pl.multiple_of(blk * rn, SUBLANES)
            q_rows = pl.ds(m0, rn)
            k_rows = pl.ds(pl.multiple_of(m0 - rn, SUBLANES), 2 * rn)
        else:
            assert blk == 0
            q_rows = pl.ds(0, rn)
            k_rows = pl.ds(0, min(2 * rn, rows_per_class))
        gather = lambda ref, rows: jnp.concatenate([ref[0, c, rows, :] for c in classes], axis=0)
        q = gather(q_ref, q_rows).astype(F32) * (HEAD_DIM ** -0.5)
        k = gather(k_ref, k_rows)
        v = gather(v_ref, k_rows)
        if k.shape[0] < 2 * BAND:
            assert jn == 1
            k = jnp.concatenate([k, k], axis=0)
            v = jnp.concatenate([v, v], axis=0)
        outs, lses = [], []
        for h in range(2):
            qh = jnp.where(first_head if h == 0 else jnp.logical_not(first_head), q, 0.0).astype(BF16)
            s = _dot_nt(qh, k) + (bfull_ref[g, h] if full else bfirst_ref[g, h])
            m = jnp.max(s, axis=-1, keepdims=True)
            p = jnp.exp(s - m)
            l = jnp.sum(p, axis=-1, keepdims=True)
            outs.append(_dot(p.astype(BF16), v) * (1.0 / l))
            lses.append(m + jnp.log(l))
        o = jnp.where(first_head, outs[0], outs[1])
        lse = jnp.broadcast_to(jnp.where(first_head, lses[0], lses[1]), o.shape)
        for j, c in enumerate(classes):
            og_ref[g, c, q_rows, :] = o[j * rn:(j + 1) * rn]
            lg_ref[g, c, q_rows, :] = lse[j * rn:(j + 1) * rn]

    for g, (window, dil) in enumerate(DIL_GROUPS):
        assert window // dil == BAND and N_CLASSES % dil == 0
        nblk = rows_per_class * N_CLASSES // (dil * BAND)
        if dil <= UNITS_PER_TRIP:
            for r in range(dil):
                unit(g, dil, r, 0, False)
        else:
            def body1(r, carry, g=g, dil=dil):
                unit(g, dil, r, 0, False)
                return carry
            lax.fori_loop(0, dil, body1, 0, unroll=_largest_divisor(dil, UNITS_PER_TRIP))
        rest = dil * (nblk - 1)
        if rest:
            def body(j, carry, g=g, dil=dil, nblk=nblk):
                unit(g, dil, j // (nblk - 1), j % (nblk - 1) + 1, True)
                return carry
            lax.fori_loop(0, rest, body, 0, unroll=_largest_divisor(rest, UNITS_PER_TRIP))

    def merge(c, carry):
        l0, l1, l2 = lg_ref[0, c], lg_ref[1, c], lg_ref[2, c]
        mx = jnp.maximum(jnp.maximum(l0, l1), l2)
        e0, e1, e2 = jnp.exp(l0 - mx), jnp.exp(l1 - mx), jnp.exp(l2 - mx)
        merged = (e0 * og_ref[0, c] + e1 * og_ref[1, c] + e2 * og_ref[2, c]) * (1.0 / (e0 + e1 + e2))
        o_ref[0, pl.ds(c, rows_per_class, stride=N_CLASSES), :] = merged
        return carry
    lax.fori_loop(0, N_CLASSES, merge, 0, unroll=2)


def _attn_prompt(q, k, v):
    b, _, rpc, _ = q.shape
    s = rpc * N_CLASSES
    bfull, bfirst = (jnp.asarray(t) for t in _prompt_bias_tables())
    ng = len(DIL_GROUPS)
    blk = lambda: pl.BlockSpec((1, N_CLASSES, rpc, LANES), lambda i, p: (i, 0, 0, p))
    est = 8 * s * LANES * 4 + 6 * s * LANES * 4 + 2 * ng * 2 * BAND * 3 * BAND * 4
    return pl.pallas_call(
        functools.partial(_attn_prompt_kernel, rows_per_class=rpc),
        grid=(b, MIX_W // LANES),
        in_specs=[blk(), blk(), blk(),
                  pl.BlockSpec((ng, 2, BAND, 2 * BAND), lambda i, p: (0, p, 0, 0)),
                  pl.BlockSpec((ng, 2, BAND, 2 * BAND), lambda i, p: (0, p, 0, 0))],
        out_specs=pl.BlockSpec((1, s, LANES), lambda i, p: (i, 0, p)),
        out_shape=jax.ShapeDtypeStruct((b, s, MIX_W), F32),
        scratch_shapes=[pltpu.VMEM((ng, N_CLASSES, rpc, LANES), F32) for _ in range(2)],
        compiler_params=_params(("parallel", "parallel"), est),
        name="attn_prompt",
    )(q, k, v, bfull, bfirst)


def _sample_bias_tables(wb, t_new):
    slopes = _alibi_slopes()

    def table(dist):
        mult = np.zeros(dist.shape)
        for window, dil in DIL_GROUPS:
            mult += (dist >= 0) & (dist <= window) & (dist % dil == 0)
        logm = np.where(mult > 0, np.log(np.maximum(mult, 1.0)), NEG)
        tab = -slopes[:, None, None] * dist[None].astype(np.float64) + logm[None]
        tab = np.where((mult > 0)[None], tab, NEG)
        return tab.reshape(N_HEADS * t_new, dist.shape[1]).astype(np.float32)

    t = np.arange(t_new)[:, None]
    return table(wb + t - np.arange(wb)[None, :]), table(t - np.arange(t_new)[None, :])


def _attn_sample_kernel(q_ref, kn_ref, vn_ref, wk_ref, wv_ref, bo_ref, bn_ref,
                        att_ref, ok_ref, ov_ref, *, wb, t_new):
    rows_q = N_HEADS * t_new
    q = q_ref[0] * (HEAD_DIM ** -0.5)
    qm = jnp.concatenate([q] * N_HEADS, axis=0)
    row_head = lax.broadcasted_iota(jnp.int32, (rows_q, MIX_W), 0) // t_new
    lane_head = lax.broadcasted_iota(jnp.int32, (rows_q, MIX_W), 1) // HEAD_DIM
    own = row_head == lane_head
    qm = jnp.where(own, qm, 0.0).astype(BF16)
    k_new, v_new = kn_ref[0], vn_ref[0]
    wk, wv = wk_ref[0], wv_ref[0]
    s_old = _dot(qm, wk.astype(BF16)) + bo_ref[...]
    s_new = _dot_nt(qm, k_new.astype(BF16)) + bn_ref[...]
    m = jnp.maximum(jnp.max(s_old, axis=-1, keepdims=True), jnp.max(s_new, axis=-1, keepdims=True))
    p_old = jnp.exp(s_old - m)
    p_new = jnp.exp(s_new - m)
    inv_l = 1.0 / (jnp.sum(p_old, axis=-1, keepdims=True) + jnp.sum(p_new, axis=-1, keepdims=True))
    o = _dot_nt(p_old.astype(BF16), wv.astype(BF16)) + _dot(p_new.astype(BF16), v_new.astype(BF16))
    o = jnp.where(own, o * inv_l, 0.0)
    att = o[0:t_new]
    for h in range(1, N_HEADS):
        att = att + o[h * t_new:(h + 1) * t_new]
    att_ref[0] = att
    lane = lax.broadcasted_iota(jnp.int32, (MIX_W, LANES), 1)
    is_new = lane >= LANES - t_new
    pad = jnp.zeros((LANES - t_new, MIX_W), F32)
    for w, new, o_ref in ((wk, k_new, ok_ref), (wv, v_new, ov_ref)):
        rolled = pltpu.roll(w, wb - t_new, axis=1)
        o_ref[0] = rolled
        new_t = jnp.concatenate([pad, new], axis=0).T
        o_ref[0, :, wb - LANES:wb] = jnp.where(is_new, new_t, rolled[:, wb - LANES:wb])


def _attn_sample(q, k_new, v_new, win_k_t, win_v_t):
    b, t_new, _ = q.shape
    wb = win_k_t.shape[2]
    assert wb >= DIL_GROUPS[-1][0] and t_new % SUBLANES == 0 and wb % LANES == 0
    b_old, b_new = _sample_bias_tables(wb, t_new)
    small = lambda: pl.BlockSpec((1, t_new, MIX_W), lambda i: (i, 0, 0))
    big = lambda: pl.BlockSpec((1, MIX_W, wb), lambda i: (i, 0, 0))
    est = 8 * wb * MIX_W * 4 + 2 * wb * MIX_W * 2 + 4 * N_HEADS * t_new * wb * 4
    return pl.pallas_call(
        functools.partial(_attn_sample_kernel, wb=wb, t_new=t_new),
        grid=(b,),
        in_specs=[small(), small(), small(), big(), big(),
                  pl.BlockSpec((N_HEADS * t_new, wb), lambda i: (0, 0)),
                  pl.BlockSpec((N_HEADS * t_new, t_new), lambda i: (0, 0))],
        out_specs=[small(), big(), big()],
        out_shape=[jax.ShapeDtypeStruct((b, t_new, MIX_W), F32),
                   jax.ShapeDtypeStruct((b, MIX_W, wb), F32),
                   jax.ShapeDtypeStruct((b, MIX_W, wb), F32)],
        compiler_params=_params(("parallel",), est),
        name="attn_sample",
    )(q, k_new, v_new, win_k_t, win_v_t, jnp.asarray(b_old), jnp.asarray(b_new))


ST_ROWS = HEAD_DIM + SUBLANES


def _mlstm_seq_kernel(qt_ref, k_ref, vt_ref, gcol_ref, mirow_ref, mfrow_ref, bcol_ref, bigrow_ref, bfgrow_ref,
                      h_ref, st_ref, m_ref, *, chunk):
    L = chunk
    hp = lax.Precision.HIGHEST

    @pl.when(pl.program_id(1) == 0)
    def _():
        st_ref[...] = jnp.zeros(st_ref.shape, F32)
        m_ref[...] = jnp.full(m_ref.shape, NEG, F32)

    ss = lax.broadcasted_iota(jnp.int32, (L, L), 0)
    tt = lax.broadcasted_iota(jnp.int32, (L, L), 1)
    causal = ss <= tt
    ig_r = mirow_ref[0] + bigrow_ref[...]
    lf_r = _log_sigmoid(mfrow_ref[0] + bfgrow_ref[...])
    b_r = jnp.dot(lf_r, causal.astype(F32), precision=hp, preferred_element_type=F32)
    lane = lax.broadcasted_iota(jnp.int32, (N_HEADS, L), 1)
    cm = ig_r - b_r
    shift = 1
    while shift < L:
        cm = jnp.maximum(cm, jnp.where(lane >= shift, pltpu.roll(cm, shift, axis=1), NEG))
        shift *= 2
    m_prev = m_ref[0][:, 0:1]
    m_t = b_r + jnp.maximum(m_prev, cm)
    row_all = b_r - m_t
    inter_w = jnp.exp(b_r + m_prev - m_t)
    floor = jnp.exp(-m_t)
    m_new = m_t[:, L - 1:L]
    b_last = b_r[:, L - 1:L]
    w_s = jnp.exp(b_last - b_r + ig_r - m_new)
    decay = jnp.exp(b_last + m_prev - m_new)
    pre_c = gcol_ref[0] + bcol_ref[...]
    bcum_c = jnp.dot((ss >= tt).astype(F32), _log_sigmoid(pre_c), precision=hp, preferred_element_type=F32)
    col_all = pre_c - pltpu.roll(bcum_c, GATE_PAD - N_HEADS, axis=1)

    ones_row = (lax.broadcasted_iota(jnp.int32, (SUBLANES, L), 0) == 0).astype(F32)
    zeros_q = jnp.zeros((HEAD_DIM, L), BF16)
    h_parts, new_state = [], []
    for h in range(N_HEADS):
        pair, half = divmod(h, 2)
        rows = slice(h * HEAD_DIM, (h + 1) * HEAD_DIM)
        q_own = qt_ref[0, rows, :].astype(BF16)
        q_pad = jnp.concatenate([q_own, zeros_q] if half == 0 else [zeros_q, q_own], axis=0)
        k_pair = k_ref[0, :, pair * LANES:(pair + 1) * LANES].astype(BF16)
        v_ext = jnp.concatenate([vt_ref[0, rows, :], ones_row], axis=0)
        st = st_ref[0, h]
        d = jnp.exp(jnp.where(causal, col_all[:, h:h + 1] + row_all[h:h + 1, :], NEG))
        a = (_dot(k_pair, q_pad) * d).astype(BF16)
        tot = _dot(v_ext.astype(BF16), a) + inter_w[h:h + 1, :] * _dot(st.astype(BF16), q_pad)
        den = tot[HEAD_DIM:HEAD_DIM + 1, :]
        h_parts.append(tot[0:HEAD_DIM, :] * (1.0 / jnp.maximum(jnp.abs(den), floor[h:h + 1, :])))
        new_state.append(decay[h:h + 1, :] * st + _dot((v_ext * w_s[h:h + 1, :]).astype(BF16), k_pair))
    h_ref[0] = jnp.concatenate(h_parts, axis=0).T
    for h, st_new in enumerate(new_state):
        st_ref[0, h] = st_new
    m_ref[0] = jnp.broadcast_to(m_new, (N_HEADS, LANES))


def _mlstm_seq(q_t, k, v_t, gates, b_ig, b_fg, chunk):
    b, s, _ = k.shape
    assert s % chunk == 0 and chunk % LANES == 0
    g_rows = jnp.swapaxes(gates[:, :, :2 * N_HEADS], 1, 2)
    bcol = jnp.concatenate([b_ig, b_fg, jnp.zeros((GATE_PAD - 2 * N_HEADS,), F32)]).reshape(1, GATE_PAD)
    ft = lambda: pl.BlockSpec((1, MIX_W, chunk), lambda i, c: (i, 0, c))
    row = lambda: pl.BlockSpec((1, N_HEADS, chunk), lambda i, c: (i, 0, c))
    const = lambda r, w: pl.BlockSpec((r, w), lambda i, c: (0, 0))
    est = 8 * chunk * MIX_W * 4 + 12 * chunk * chunk * 4 + 4 * N_HEADS * ST_ROWS * LANES * 4
    h, st, m = pl.pallas_call(
        functools.partial(_mlstm_seq_kernel, chunk=chunk),
        grid=(b, s // chunk),
        in_specs=[ft(), pl.BlockSpec((1, chunk, MIX_W), lambda i, c: (i, c, 0)), ft(),
                  pl.BlockSpec((1, chunk, GATE_PAD), lambda i, c: (i, c, 0)), row(), row(),
                  const(1, GATE_PAD), const(N_HEADS, 1), const(N_HEADS, 1)],
        out_specs=[pl.BlockSpec((1, chunk, MIX_W), lambda i, c: (i, c, 0)),
                   pl.BlockSpec((1, N_HEADS, ST_ROWS, LANES), lambda i, c: (i, 0, 0, 0)),
                   pl.BlockSpec((1, N_HEADS, LANES), lambda i, c: (i, 0, 0))],
        out_shape=[jax.ShapeDtypeStruct((b, s, MIX_W), F32),
                   jax.ShapeDtypeStruct((b, N_HEADS, ST_ROWS, LANES), F32),
                   jax.ShapeDtypeStruct((b, N_HEADS, LANES), F32)],
        compiler_params=_params(("arbitrary", "arbitrary"), est),
        name="mlstm_seq",
    )(q_t, k, v_t, gates, g_rows[:, :N_HEADS], g_rows[:, N_HEADS:], bcol, b_ig.reshape(-1, 1), b_fg.reshape(-1, 1))
    st = st.reshape(b, N_HEADS // 2, 2, ST_ROWS, 2, HEAD_DIM)
    st = jnp.stack([st[:, :, 0, :, 0, :], st[:, :, 1, :, 1, :]], axis=2).reshape(b, N_HEADS, ST_ROWS, HEAD_DIM)
    c = jnp.swapaxes(st[:, :, :HEAD_DIM, :], 2, 3)
    return h, c, st[:, :, HEAD_DIM, :], m[:, :, 0]


def _mlstm_step_kernel(xq_ref, xk_ref, v_ref, g_ref, sq_ref, sk_ref, cwq_ref, cwk_ref, cbq_ref, cbk_ref, gb_ref,
                       c0_ref, n0_ref, m0_ref, h_ref, c_ref, n_ref, m_ref, q_scr, ik_scr, *, t_new, batch):
    hist = CONV_W - 1
    tile = lambda ref, t: ref[pl.ds(t, batch, stride=t_new), :].T
    xq = [sq_ref[t].T for t in range(hist)] + [tile(xq_ref, t) for t in range(t_new)]
    xk = [sk_ref[t].T for t in range(hist)] + [tile(xk_ref, t) for t in range(t_new)]
    c_ref[...] = c0_ref[...]
    n = [n0_ref[e] for e in range(2)]
    m = [m0_ref[e] for e in range(2)]
    for t in range(t_new):
        cq, ck = cbq_ref[...], cbk_ref[...]
        for j in range(CONV_W):
            cq = cq + xq[t + j] * cwq_ref[j]
            ck = ck + xk[t + j] * cwk_ref[j]
        q_t = cq * _sigmoid(cq)
        k_t = ck * _sigmoid(ck) * (HEAD_DIM ** -0.5)
        v_t = tile(v_ref, t)
        gates = tile(g_ref, t)[0:2 * N_HEADS, :] + gb_ref[...]
        h_pair = []
        for e in range(2):
            head = 2 * pl.program_id(0) + e
            rows = slice(e * HEAD_DIM, (e + 1) * HEAD_DIM)
            ig = _gates_row(gates, head)
            lf = _log_sigmoid(_gates_row(gates, N_HEADS + head))
            m_new = jnp.maximum(lf + m[e], ig)
            f = jnp.exp(lf + m[e] - m_new)
            i = jnp.exp(ig - m_new)
            q_e, k_e, v_e = q_t[rows], k_t[rows], v_t[rows]
            q_scr[...] = q_e
            ik_scr[...] = k_e * i

            def body(kk, num, e=e, f=f, v_e=v_e):
                c_new = f * c_ref[e, kk] + ik_scr[pl.ds(kk, 1), :] * v_e
                c_ref[e, kk] = c_new
                return num + c_new * q_scr[pl.ds(kk, 1), :]
            num = lax.fori_loop(0, HEAD_DIM, body, jnp.zeros((HEAD_DIM, batch), F32), unroll=8)
            n[e] = f * n[e] + k_e * i
            den = jnp.sum(n[e] * q_e, axis=0, keepdims=True)
            h_pair.append(num * (1.0 / jnp.maximum(jnp.abs(den), jnp.exp(-m_new))))
            m[e] = m_new
        h_ref[pl.ds(t, batch, stride=t_new), :] = jnp.concatenate(h_pair, axis=0).T
    for e in range(2):
        n_ref[e] = n[e]
        m_ref[e] = m[e]


def _gates_row(gates, idx):
    sel = lax.broadcasted_iota(jnp.int32, gates.shape, 0) == idx
    return jnp.sum(jnp.where(sel, gates, 0.0), axis=0, keepdims=True)


def _mlstm_step(qk_pre, conv_state, conv_w, conv_b, mv, gates, b_ig, b_fg, c0, n0, m0, t_new):
    rows = qk_pre.shape[0]
    batch = rows // t_new
    assert batch == LANES and t_new % SUBLANES == 0
    hist = CONV_W - 1
    lanes_b = lambda a: jnp.broadcast_to(a[..., None], a.shape + (batch,))
    sc = jnp.swapaxes(conv_state, 0, 1)
    cw, cb = lanes_b(conv_w), lanes_b(conv_b)
    gb = lanes_b(jnp.concatenate([b_ig, b_fg]))
    c_t = jnp.transpose(c0, (1, 2, 3, 0))
    n_t = jnp.transpose(n0, (1, 2, 0))
    m_t = jnp.transpose(m0, (1, 0)).reshape(N_HEADS, 1, batch)
    npair = N_HEADS // 2
    colq = lambda: pl.BlockSpec((rows, LANES), lambda p: (0, p))
    colk = lambda: pl.BlockSpec((rows, LANES), lambda p: (0, npair + p))
    est = (8 * rows * LANES * 4 + 4 * 2 * HEAD_DIM * HEAD_DIM * batch * 4 + 2 * (hist + 2 * CONV_W + 2) * LANES * batch * 4
           + 4 * (hist + t_new) * LANES * batch * 4)
    h, c, n, m = pl.pallas_call(
        functools.partial(_mlstm_step_kernel, t_new=t_new, batch=batch),
        grid=(npair,),
        in_specs=[colq(), colk(), colq(), pl.BlockSpec((rows, GATE_PAD), lambda p: (0, 0)),
                  pl.BlockSpec((hist, batch, LANES), lambda p: (0, 0, p)),
                  pl.BlockSpec((hist, batch, LANES), lambda p: (0, 0, npair + p)),
                  pl.BlockSpec((CONV_W, LANES, batch), lambda p: (0, p, 0)),
                  pl.BlockSpec((CONV_W, LANES, batch), lambda p: (0, npair + p, 0)),
                  pl.BlockSpec((LANES, batch), lambda p: (p, 0)),
                  pl.BlockSpec((LANES, batch), lambda p: (npair + p, 0)),
                  pl.BlockSpec((2 * N_HEADS, batch), lambda p: (0, 0)),
                  pl.BlockSpec((2, HEAD_DIM, HEAD_DIM, batch), lambda p: (p, 0, 0, 0)),
                  pl.BlockSpec((2, HEAD_DIM, batch), lambda p: (p, 0, 0)),
                  pl.BlockSpec((2, 1, batch), lambda p: (p, 0, 0))],
        out_specs=[colq(),
                   pl.BlockSpec((2, HEAD_DIM, HEAD_DIM, batch), lambda p: (p, 0, 0, 0)),
                   pl.BlockSpec((2, HEAD_DIM, batch), lambda p: (p, 0, 0)),
                   pl.BlockSpec((2, 1, batch), lambda p: (p, 0, 0))],
        out_shape=[jax.ShapeDtypeStruct((rows, MIX_W), F32),
                   jax.ShapeDtypeStruct(c_t.shape, F32), jax.ShapeDtypeStruct(n_t.shape, F32),
                   jax.ShapeDtypeStruct(m_t.shape, F32)],
        scratch_shapes=[pltpu.VMEM((HEAD_DIM, batch), F32), pltpu.VMEM((HEAD_DIM, batch), F32)],
        compiler_params=_params(("arbitrary",), est),
        name="mlstm_step",
    )(qk_pre, qk_pre, mv, gates, sc, sc, cw, cw, cb, cb, gb, c_t, n_t, m_t)
    return (h, jnp.transpose(c, (3, 0, 1, 2)), jnp.transpose(n, (2, 0, 1)),
            jnp.transpose(m.reshape(N_HEADS, batch), (1, 0)))


def _mixers_to_output(x2, att, mlh, mo, mem_k, mem_v, wts, bt_x, tq_x, tm):
    b = mem_k.shape[0]
    if tq_x >= 2 * LANES:
        r3 = lambda a: a.reshape(b, -1, a.shape[-1])
        h2 = _xattn_block(r3(att), r3(mlh), r3(mo), r3(x2), wts, mem_k, mem_v, tq_x).reshape(-1, D_MODEL)
    else:
        h1 = _merge(att, mlh, mo, wts["b_og"], wts["g_attn_out"], wts["g_mlstm_out"], wts["w_out"], x2, tm)
        (xq,) = _norm_matmul(h1, wts["g_cross"], wts["w_cq"], (D_MODEL,), tm)
        xo = _xattn(xq.reshape(b, -1, D_MODEL), mem_k, mem_v, bt_x, tq_x)
        h2 = _matmul_res(xo.reshape(-1, D_MODEL), wts["w_co"], h1, tm)
    return _ffn(h2, wts["g_ffn"], wts["w_gate"], wts["w_up"], wts["w_down"], wts["g_final"], tm)


def kernel(x_prompt, x_sample, mem_prompt, cache_win_k, cache_win_v, cache_mem_k, cache_mem_v, state_conv, state_mlstm_C, state_mlstm_n, state_mlstm_m, g_mix, w_in, conv_w, conv_b, b_ig, b_fg, b_og, g_attn_out, g_mlstm_out, w_out, g_cross, g_mem, w_cq, w_ck, w_cv, w_co, g_ffn, w_gate, w_up, w_down, g_final):
    bp, sp, _ = x_prompt.shape
    bs, ts, _ = x_sample.shape
    wts = dict(b_og=b_og, g_attn_out=g_attn_out, g_mlstm_out=g_mlstm_out, g_cross=g_cross, g_ffn=g_ffn,
               g_final=g_final, w_out=w_out.astype(BF16), w_cq=w_cq.astype(BF16), w_co=w_co.astype(BF16),
               w_gate=w_gate.astype(BF16), w_up=w_up.astype(BF16), w_down=w_down.astype(BF16))
    w_in_t = jnp.pad(w_in.T, ((0, sum(IN_WIDTHS) - N_IN), (0, 0))).astype(BF16)
    w_ckv = jnp.concatenate([w_ck, w_cv], axis=1).astype(BF16)
    tm = 512

    xp2 = x_prompt.reshape(bp * sp, D_MODEL)
    aq_c, ak_c, av_c, ak_t, av_t, mq_t, mk, mv_t, mo, gates, conv_tail = _inproj_prompt(
        xp2, g_mix, w_in_t, conv_w, conv_b, tm, sp)
    r3 = lambda a: a.reshape(bp, sp, -1)
    att = _attn_prompt(aq_c, ak_c, av_c)
    keep = min(DIL_GROUPS[-1][0], sp)
    from_t = lambda a: jnp.transpose(a.reshape(a.shape[0], N_HEADS, HEAD_DIM, a.shape[2]), (0, 3, 1, 2))
    win_k_p = from_t(ak_t)[:, sp - keep:]
    win_v_p = from_t(av_t)[:, sp - keep:]
    mlh, c_p, n_p, m_p = _mlstm_seq(mq_t, r3(mk), mv_t, r3(gates), b_ig, b_fg, chunk=256)
    conv_p = conv_tail[:, SUBLANES - (CONV_W - 1):]
    mem_k_view, mem_v_view = _mem_proj(mem_prompt, g_mem, w_ckv)
    mem_k_p, mem_v_p = _mem_tile_unview(mem_k_view), _mem_tile_unview(mem_v_view)
    y_p = _mixers_to_output(xp2, att.reshape(-1, MIX_W), mlh.reshape(-1, MIX_W), mo, mem_k_view, mem_v_view, wts,
                            bt_x=1, tq_x=1024, tm=tm)

    xs2 = x_sample.reshape(bs * ts, D_MODEL)
    aq, ak, av, qk_pre, mv, mo, gates = _norm_matmul(xs2, g_mix, w_in_t, IN_WIDTHS, tm, w_transposed=True)
    r3 = lambda a: a.reshape(bs, ts, -1)
    wb = cache_win_k.shape[1]
    to_t = lambda a: jnp.transpose(a, (0, 2, 3, 1)).reshape(a.shape[0], MIX_W, a.shape[1])
    att, win_k_s, win_v_s = _attn_sample(r3(aq), r3(ak), r3(av), to_t(cache_win_k), to_t(cache_win_v))
    mlh, c_s, n_s, m_s = _mlstm_step(qk_pre, state_conv, conv_w, conv_b, mv, gates, b_ig, b_fg,
                                     state_mlstm_C, state_mlstm_n, state_mlstm_m, ts)
    conv_s = jnp.concatenate([state_conv, r3(qk_pre)], axis=1)[:, -(CONV_W - 1):]
    y_s = _mixers_to_output(xs2, att.reshape(-1, MIX_W), mlh, mo, _mem_tile_view(cache_mem_k),
                            _mem_tile_view(cache_mem_v), wts,
                            bt_x=4, tq_x=ts, tm=tm)

    return (y_p.reshape(bp, sp, D_MODEL), y_s.reshape(bs, ts, D_MODEL),
            win_k_p, win_v_p, conv_p, c_p, n_p, m_p, mem_k_p, mem_v_p,
            from_t(win_k_s), from_t(win_v_s), conv_s, c_s, n_s, m_s)
```
